```python
import math
import jax, jax.numpy as jnp
from jax import lax
import numpy as np

D_MODEL = 1024
BATCH = 8
SEQ = 2048
DEPTH = 1
DEC_BATCH = 128
DEC_SEQ = 4
PAST_LEN = 16384
PAGE_SIZE = 128

N_META = 16
CHUNK = 128
M_HEADS = 4
M_DK = D_MODEL // 8
M_DV = D_MODEL // 4
M_QK = M_HEADS * M_DK
M_V = M_HEADS * M_DV
M_CONV = 4
S_INNER = 2 * D_MODEL
S_HEADDIM = 64
S_HEADS = S_INNER // S_HEADDIM
S_GROUPS = 4
S_STATE = 128
S_CONV = 4
S_XBC = S_INNER + 2 * S_GROUPS * S_STATE
D_FF = 2816
F_CONV = 3
ALPHA = (2 * DEPTH) ** 0.25
BETA = (8 * DEPTH) ** -0.25
LN_EPS = 1e-5
RMS_EPS = 1e-5
IN_SPLITS = (2 * M_QK, M_V, M_V, M_HEADS, M_HEADS, S_INNER, S_XBC, S_HEADS, 2 * D_MODEL)
D_IN = 2 * M_QK + 2 * M_V + 2 * M_HEADS + S_INNER + S_XBC + S_HEADS + 2 * D_MODEL

kernel_name = 'mlstm_ssd_gated_hybrid_step'


def layer_norm(x, g, b):
    xf = x.astype(jnp.float32)
    mu = jnp.mean(xf, -1, keepdims=True)
    var = jnp.mean(jnp.square(xf - mu), -1, keepdims=True)
    y = (xf - mu) * lax.rsqrt(var + LN_EPS) * g.astype(jnp.float32) + b.astype(jnp.float32)
    return y.astype(x.dtype)


def split_cols(u, sizes):
    outs = []
    start = 0
    for s in sizes:
        outs.append(u[..., start:start + s])
        start += s
    return outs


def causal_dwconv(u, buf, w, b):
    width = w.shape[0]
    L = u.shape[1]
    full = jnp.concatenate([buf.astype(u.dtype), u], axis=1)
    y = b.astype(u.dtype) + sum(full[:, j:j + L] * w[j].astype(u.dtype) for j in range(width))
    return y, full[:, L:]


def chunk_len(L):
    return CHUNK if L % CHUNK == 0 else L


def to_chunks(a, cl):
    B, L = a.shape[:2]
    return jnp.moveaxis(a.reshape((B, L // cl, cl) + a.shape[2:]), 1, 0)


def from_chunks(a):
    nc, B, cl = a.shape[:3]
    return jnp.moveaxis(a, 0, 1).reshape((B, nc * cl) + a.shape[3:])


def mlstm_chunked(q, k, v, ig, lf, state):
    cl = chunk_len(q.shape[1])
    causal = jnp.tril(jnp.ones((cl, cl), dtype=bool))

    def step(carry, inp):
        C0, n0, m0 = carry
        qc, kc, vc, igc, lfc = inp
        qf = qc.astype(jnp.float32) * (M_DK ** -0.5)
        kf = kc.astype(jnp.float32)
        vf = vc.astype(jnp.float32)
        bt = jnp.moveaxis(jnp.cumsum(lfc, axis=1), 1, 2)
        it = jnp.moveaxis(igc, 1, 2)
        dmat = jnp.where(causal, bt[..., :, None] - bt[..., None, :] + it[..., None, :], -jnp.inf)
        inter = bt + m0[..., None]
        m_t = jnp.maximum(inter, jnp.max(dmat, -1))
        w_intra = jnp.exp(dmat - m_t[..., None])
        w_inter = jnp.exp(inter - m_t)
        s = jnp.einsum('bthd,bshd->bhts', qf, kf) * w_intra
        num = (jnp.einsum('bhts,bshe->bthe', s, vf)
               + jnp.einsum('bthd,bhde->bthe', qf, C0) * jnp.moveaxis(w_inter, 1, 2)[..., None])
        den = jnp.sum(s, -1) + jnp.einsum('bthd,bhd->bht', qf, n0) * w_inter
        denom = jnp.maximum(jnp.abs(den), jnp.exp(-m_t))
        h = num / jnp.moveaxis(denom, 1, 2)[..., None]
        b_last = bt[..., -1]
        d_last = b_last[..., None] - bt + it
        m_new = jnp.maximum(b_last + m0, jnp.max(d_last, -1))
        w_last = jnp.exp(d_last - m_new[..., None])
        decay = jnp.exp(b_last + m0 - m_new)
        C_new = decay[..., None, None] * C0 + jnp.einsum('bhs,bshd,bshe->bhde', w_last, kf, vf)
        n_new = decay[..., None] * n0 + jnp.einsum('bhs,bshd->bhd', w_last, kf)
        return (C_new, n_new, m_new), h

    state, hs = lax.scan(step, state, tuple(to_chunks(a, cl) for a in (q, k, v, ig, lf)))
    return from_chunks(hs), state


def ssd_chunked(xs, dt, Bm, Cm, A, S0):
    Bsz = xs.shape[0]
    cl = chunk_len(xs.shape[1])
    R = S_HEADS // S_GROUPS
    causal = jnp.tril(jnp.ones((cl, cl), dtype=bool))

    def step(S, inp):
        xc, dtc, Bc, Cc = inp
        xf = xc.astype(jnp.float32)
        Bf = Bc.astype(jnp.float32)
        Cf = Cc.astype(jnp.float32)
        bt = jnp.moveaxis(jnp.cumsum(dtc * A, axis=1), 1, 2)
        seg = jnp.where(causal, bt[..., :, None] - bt[..., None, :], -jnp.inf)
        decay_ts = jnp.exp(seg).reshape(Bsz, S_GROUPS, R, cl, cl)
        cb = jnp.einsum('btgn,bsgn->bgts', Cf, Bf)
        xdt = (xf * dtc[..., None]).reshape(Bsz, cl, S_GROUPS, R, S_HEADDIM)
        y_intra = jnp.einsum('bgrts,bsgrp->btgrp', cb[:, :, None] * decay_ts, xdt)
        Sg = S.reshape(Bsz, S_GROUPS, R, S_HEADDIM, S_STATE)
        decay_t = jnp.exp(bt).reshape(Bsz, S_GROUPS, R, cl)
        y_inter = jnp.einsum('btgn,bgrpn->btgrp', Cf, Sg) * jnp.moveaxis(decay_t, 3, 1)[..., None]
        b_last = bt[..., -1]
        w_end = jnp.exp(b_last[..., None] - bt).reshape(Bsz, S_GROUPS, R, cl)
        S_add = jnp.einsum('bgrs,bsgrp,bsgn->bgrpn', w_end, xdt, Bf).reshape(Bsz, S_HEADS, S_HEADDIM, S_STATE)
        S_new = jnp.exp(b_last)[..., None, None] * S + S_add
        return S_new, (y_intra + y_inter).reshape(Bsz, cl, S_HEADS, S_HEADDIM)

    S, ys = lax.scan(step, S0, tuple(to_chunks(a, cl) for a in (xs, dt, Bm, Cm)))
    return from_chunks(ys), S


def trunk_layer(x, seg_lens, state, w):
    mconv_buf, C0, n0, m0, sconv_buf, S0, fconv_buf = state
    f32 = jnp.float32
    Bsz, L, _ = x.shape
    u = x @ w['w_in']
    qk_pre, v, o_pre, i_pre, f_pre, z, xbc_pre, dt_pre, gate_pre = split_cols(u, IN_SPLITS)

    qk, mconv_new = causal_dwconv(qk_pre, mconv_buf, w['w_mconv'], w['b_mconv'])
    qk = jax.nn.silu(qk)
    q = qk[..., :M_QK].reshape(Bsz, L, M_HEADS, M_DK)
    k = qk[..., M_QK:].reshape(Bsz, L, M_HEADS, M_DK)
    vh = v.reshape(Bsz, L, M_HEADS, M_DV)
    ig = (i_pre + w['b_if'][:M_HEADS]).astype(f32)
    lf = jax.nn.log_sigmoid((f_pre + w['b_if'][M_HEADS:]).astype(f32))
    mstate = (C0.astype(f32), n0.astype(f32), m0.astype(f32))
    hs = []
    start = 0
    for sl in seg_lens:
        sel = slice(start, start + sl)
        h_seg, mstate = mlstm_chunked(q[:, sel], k[:, sel], vh[:, sel], ig[:, sel], lf[:, sel], mstate)
        hs.append(h_seg)
        start += sl
    h = jnp.concatenate(hs, axis=1)
    mu = jnp.mean(h, -1, keepdims=True)
    var = jnp.mean(jnp.square(h - mu), -1, keepdims=True)
    h = ((h - mu) * lax.rsqrt(var + LN_EPS)).reshape(Bsz, L, M_V) * w['mnorm_g'].astype(f32)
    ya = (jax.nn.sigmoid(o_pre) * h.astype(x.dtype)) @ w['w_proj_a']

    xbc, sconv_new = causal_dwconv(xbc_pre, sconv_buf, w['w_sconv'], w['b_sconv'])
    xbc = jax.nn.silu(xbc)
    xs = xbc[..., :S_INNER].reshape(Bsz, L, S_HEADS, S_HEADDIM)
    Bm = xbc[..., S_INNER:S_INNER + S_GROUPS * S_STATE].reshape(Bsz, L, S_GROUPS, S_STATE)
    Cm = xbc[..., S_INNER + S_GROUPS * S_STATE:].reshape(Bsz, L, S_GROUPS, S_STATE)
    dt = jax.nn.softplus((dt_pre + w['dt_bias']).astype(f32))
    A = -jnp.exp(w['A_log'].astype(f32))
    S = S0.astype(f32)
    ys = []
    start = 0
    for sl in seg_lens:
        sel = slice(start, start + sl)
        y_seg, S = ssd_chunked(xs[:, sel], dt[:, sel], Bm[:, sel], Cm[:, sel], A, S)
        ys.append(y_seg)
        start += sl
    y = jnp.concatenate(ys, axis=1) + w['D'].astype(f32)[:, None] * xs.astype(f32)
    y = y.reshape(Bsz, L, S_INNER) * jax.nn.silu(z.astype(f32))
    yg = y.reshape(Bsz, L, S_GROUPS, S_INNER // S_GROUPS)
    yg = yg * lax.rsqrt(jnp.mean(jnp.square(yg), -1, keepdims=True) + RMS_EPS)
    y = yg.reshape(Bsz, L, S_INNER) * w['snorm_g'].astype(f32)
    yb = y.astype(x.dtype) @ w['w_proj_b']

    g = jax.nn.sigmoid(gate_pre).reshape(Bsz, L, 2, D_MODEL)
    mixed = (g[:, :, 0] * ya + g[:, :, 1] * yb) @ w['w_out']
    x1 = layer_norm(ALPHA * x + mixed, w['ln1_g'], w['ln1_b'])

    up = x1 @ w['w_up']
    upc, fconv_new = causal_dwconv(up, fconv_buf, w['w_fconv'], w['b_fconv'])
    ff = (jax.nn.silu(upc[..., :D_FF]) * upc[..., D_FF:]) @ w['w_down']
    x2 = layer_norm(ALPHA * x1 + ff, w['ln2_g'], w['ln2_b'])
    C1, n1, m1 = mstate
    return x2, (mconv_new, C1, n1, m1, sconv_new, S, fconv_new)


def setup_inputs(seed: int = 0) -> dict:
    key = jax.random.key(seed)
    ks = iter(jax.random.split(key, 48))
    nrm = lambda shape, scale: jax.random.normal(next(ks), shape, jnp.float32) * scale
    x_prompt = nrm((BATCH, SEQ, D_MODEL), 1.0)
    x_sample = nrm((DEC_BATCH, DEC_SEQ, D_MODEL), 1.0)
    state_mlstm_conv = nrm((DEPTH, DEC_BATCH, M_CONV - 1, 2 * M_QK), 1.0)
    state_mlstm_C = nrm((DEPTH, DEC_BATCH, M_HEADS, M_DK, M_DV), 0.1)
    state_mlstm_n = nrm((DEPTH, DEC_BATCH, M_HEADS, M_DK), 0.5)
    state_mlstm_m = nrm((DEPTH, DEC_BATCH, M_HEADS), 1.0)
    state_ssm_conv = nrm((DEPTH, DEC_BATCH, S_CONV - 1, S_XBC), 1.0)
    state_ssm = nrm((DEPTH, DEC_BATCH, S_HEADS, S_HEADDIM, S_STATE), 0.1)
    state_ffn_conv = nrm((DEPTH, DEC_BATCH, F_CONV - 1, 2 * D_FF), 1.0)
    meta_tokens = nrm((N_META, D_MODEL), 1.0)
    ln0_g = 1.0 + nrm((D_MODEL,), 0.02)
    ln0_b = nrm((D_MODEL,), 0.02)
    w_in = nrm((DEPTH, D_MODEL, D_IN), D_MODEL ** -0.5)
    f_init = jnp.broadcast_to(jnp.linspace(3.0, 6.0, M_HEADS, dtype=jnp.float32), (DEPTH, M_HEADS))
    b_mlstm_if = jnp.concatenate([nrm((DEPTH, M_HEADS), 0.1), f_init + nrm((DEPTH, M_HEADS), 0.1)], axis=-1)
    w_mlstm_conv = nrm((DEPTH, M_CONV, 2 * M_QK), M_CONV ** -0.5)
    b_mlstm_conv = nrm((DEPTH, 2 * M_QK), 0.02)
    mlstm_norm_g = 1.0 + nrm((DEPTH, M_V), 0.02)
    w_proj_a = nrm((DEPTH, M_V, D_MODEL), M_V ** -0.5)
    w_ssm_conv = nrm((DEPTH, S_CONV, S_XBC), S_CONV ** -0.5)
    b_ssm_conv = nrm((DEPTH, S_XBC), 0.02)
    dt0 = jnp.exp(jax.random.uniform(next(ks), (DEPTH, S_HEADS), jnp.float32, math.log(1e-3), math.log(1e-1)))
    ssm_dt_bias = dt0 + jnp.log(-jnp.expm1(-dt0))
    ssm_A_log = jnp.log(jax.random.uniform(next(ks), (DEPTH, S_HEADS), jnp.float32, 1.0, 16.0))
    ssm_D = 1.0 + nrm((DEPTH, S_HEADS), 0.1)
    ssm_norm_g = 1.0 + nrm((DEPTH, S_INNER), 0.02)
    w_proj_b = nrm((DEPTH, S_INNER, D_MODEL), S_INNER ** -0.5)
    w_out = nrm((DEPTH, D_MODEL, D_MODEL), BETA * D_MODEL ** -0.5)
    ln1_g = 1.0 + nrm((DEPTH, D_MODEL), 0.02)
    ln1_b = nrm((DEPTH, D_MODEL), 0.02)
    w_up = nrm((DEPTH, D_MODEL, 2 * D_FF), D_MODEL ** -0.5)
    w_ffn_conv = nrm((DEPTH, F_CONV, 2 * D_FF), F_CONV ** -0.5)
    b_ffn_conv = nrm((DEPTH, 2 * D_FF), 0.02)
    w_down = nrm((DEPTH, D_FF, D_MODEL), BETA * D_FF ** -0.5)
    ln2_g = 1.0 + nrm((DEPTH, D_MODEL), 0.02)
    ln2_b = nrm((DEPTH, D_MODEL), 0.02)
    return {'x_prompt': x_prompt, 'x_sample': x_sample,
            'state_mlstm_conv': state_mlstm_conv, 'state_mlstm_C': state_mlstm_C,
            'state_mlstm_n': state_mlstm_n, 'state_mlstm_m': state_mlstm_m,
            'state_ssm_conv': state_ssm_conv, 'state_ssm': state_ssm, 'state_ffn_conv': state_ffn_conv,
            'meta_tokens': meta_tokens, 'ln0_g': ln0_g, 'ln0_b': ln0_b, 'w_in': w_in,
            'b_mlstm_if': b_mlstm_if, 'w_mlstm_conv': w_mlstm_conv, 'b_mlstm_conv': b_mlstm_conv,
            'mlstm_norm_g': mlstm_norm_g, 'w_proj_a': w_proj_a, 'w_ssm_conv': w_ssm_conv,
            'b_ssm_conv': b_ssm_conv, 'ssm_dt_bias': ssm_dt_bias, 'ssm_A_log': ssm_A_log, 'ssm_D': ssm_D,
            'ssm_norm_g': ssm_norm_g, 'w_proj_b': w_proj_b, 'w_out': w_out, 'ln1_g': ln1_g, 'ln1_b': ln1_b,
            'w_up': w_up, 'w_ffn_conv': w_ffn_conv, 'b_ffn_conv': b_ffn_conv, 'w_down': w_down,
            'ln2_g': ln2_g, 'ln2_b': ln2_b}


def reference(x_prompt, x_sample, state_mlstm_conv, state_mlstm_C, state_mlstm_n, state_mlstm_m,
              state_ssm_conv, state_ssm, state_ffn_conv, meta_tokens, ln0_g, ln0_b, w_in,
              b_mlstm_if, w_mlstm_conv, b_mlstm_conv, mlstm_norm_g, w_proj_a, w_ssm_conv,
              b_ssm_conv, ssm_dt_bias, ssm_A_log, ssm_D, ssm_norm_g, w_proj_b, w_out, ln1_g, ln1_b,
              w_up, w_ffn_conv, b_ffn_conv, w_down, ln2_g, ln2_b):
    f32 = jnp.float32
    Bp = x_prompt.shape[0]
    meta = jnp.broadcast_to(meta_tokens[None].astype(x_prompt.dtype), (Bp, N_META, D_MODEL))
    hp = layer_norm(jnp.concatenate([meta, x_prompt], axis=1), ln0_g, ln0_b)
    hs = layer_norm(x_sample, ln0_g, ln0_b)
    p_new = []
    s_new = []
    for l in range(DEPTH):
        w = {'w_in': w_in[l], 'b_if': b_mlstm_if[l], 'w_mconv': w_mlstm_conv[l], 'b_mconv': b_mlstm_conv[l],
             'mnorm_g': mlstm_norm_g[l], 'w_proj_a': w_proj_a[l], 'w_sconv': w_ssm_conv[l],
             'b_sconv': b_ssm_conv[l], 'dt_bias': ssm_dt_bias[l], 'A_log': ssm_A_log[l], 'D': ssm_D[l],
             'snorm_g': ssm_norm_g[l], 'w_proj_b': w_proj_b[l], 'w_out': w_out[l],
             'ln1_g': ln1_g[l], 'ln1_b': ln1_b[l], 'w_up': w_up[l], 'w_fconv': w_ffn_conv[l],
             'b_fconv': b_ffn_conv[l], 'w_down': w_down[l], 'ln2_g': ln2_g[l], 'ln2_b': ln2_b[l]}
        p0 = (jnp.zeros((Bp, M_CONV - 1, 2 * M_QK), hp.dtype),
              jnp.zeros((Bp, M_HEADS, M_DK, M_DV), f32),
              jnp.zeros((Bp, M_HEADS, M_DK), f32),
              jnp.zeros((Bp, M_HEADS), f32),
              jnp.zeros((Bp, S_CONV - 1, S_XBC), hp.dtype),
              jnp.zeros((Bp, S_HEADS, S_HEADDIM, S_STATE), f32),
              jnp.zeros((Bp, F_CONV - 1, 2 * D_FF), hp.dtype))
        hp, ps = trunk_layer(hp, (N_META, hp.shape[1] - N_META), p0, w)
        s0 = (state_mlstm_conv[l], state_mlstm_C[l], state_mlstm_n[l], state_mlstm_m[l],
              state_ssm_conv[l], state_ssm[l], state_ffn_conv[l])
        hs, ss = trunk_layer(hs, (hs.shape[1],), s0, w)
        p_new.append(ps)
        s_new.append(ss)
    pn = [jnp.stack([t[i] for t in p_new]) for i in range(7)]
    sn = [jnp.stack([t[i] for t in s_new]) for i in range(7)]
    y_prompt = hp[:, N_META:]
    y_sample = hs
    return (y_prompt, y_sample, pn[0], pn[1], pn[2], pn[3], pn[4], pn[5], pn[6],
            sn[0], sn[1], sn[2], sn[3], sn[4], sn[5], sn[6])
```

```python
import functools

import jax
import jax.numpy as jnp
from jax import lax
from jax.experimental import pallas as pl
from jax.experimental.pallas import tpu as pltpu

F32 = jnp.float32
BF16 = jnp.bfloat16

D_MODEL = 1024
N_META = 16
M_HEADS = 4
M_DK = 128
M_DV = 256
M_QK = M_HEADS * M_DK
M_V = M_HEADS * M_DV
M_CONV = 4
S_INNER = 2048
S_HEADDIM = 64
S_HEADS = 32
S_GROUPS = 4
S_STATE = 128
S_CONV = 4
S_XBC = S_INNER + 2 * S_GROUPS * S_STATE
S_GROUP_W = S_INNER // S_GROUPS
S_HEADS_PER_GROUP = S_HEADS // S_GROUPS
D_FF = 2816
F_CONV = 3
ALPHA = 2.0 ** 0.25
LN_EPS = 1e-5
RMS_EPS = 1e-5

LANES = 128
SUBLANES = 8
CHUNK = 128
SEQ_PAD = SUBLANES
DEC_SEQ = 4
VMEM_LIMIT_BYTES = 56 * 1024 * 1024

OFF_QK = 0
OFF_V = OFF_QK + 2 * M_QK
OFF_O = OFF_V + M_V
OFF_Z = OFF_O + M_V
OFF_XBC = OFF_Z + S_INNER
OFF_GATE = OFF_XBC + S_XBC
W_BIG_COLS = OFF_GATE + 2 * D_MODEL
LANE_IG = 0
LANE_LF = M_HEADS
LANE_DTA = 2 * M_HEADS
LANE_DT = LANE_DTA + S_HEADS
LANE_END = LANE_DT + S_HEADS


def _softplus(x):
    return jnp.maximum(x, 0.0) + jnp.log1p(jnp.exp(-jnp.abs(x)))


def _silu(x):
    return x * jax.nn.sigmoid(x)


def _layer_norm(x, g, b):
    mu = jnp.mean(x, axis=-1, keepdims=True)
    xc = x - mu
    var = jnp.mean(xc * xc, axis=-1, keepdims=True)
    return xc * lax.rsqrt(var + LN_EPS) * g + b


def _dot(a, b):
    return jnp.dot(a, b, preferred_element_type=F32)


def _dot_nt(a, b):
    return lax.dot_general(a, b, (((1,), (1,)), ((), ())), preferred_element_type=F32)


def _dot_tn(a, b):
    return lax.dot_general(a, b, (((0,), (0,)), ((), ())), preferred_element_type=F32)


def _conv_seq(u, cbuf, cin_ref, cout_ref, w_ref, b_ref, width, rows, n_valid, first):
    @pl.when(first)
    def _():
        cbuf[0:SUBLANES, :] = cin_ref[...]
    cbuf[SUBLANES:SUBLANES + rows, :] = u
    acc = b_ref[...]
    for j in range(width):
        off = SUBLANES - (width - 1) + j
        acc = acc + cbuf[off:off + rows, :] * w_ref[j:j + 1, :]
    tail = cbuf[n_valid:n_valid + SUBLANES, :]
    cout_ref[...] = tail
    cbuf[0:SUBLANES, :] = tail
    return acc


def _conv_cyc(u, valid, st_ref, upre_ref, w_ref, b_ref, width, rows):
    cols = u.shape[1]
    upre_ref[...] = u
    f = (jnp.where(valid, u, 0.0) + st_ref[...]).reshape(rows // SEQ_PAD, SEQ_PAD, cols)
    acc = b_ref[...][None] + f * w_ref[width - 1:width, :][None]
    for k in range(1, width):
        acc = acc + pltpu.roll(f, k, 1) * w_ref[width - 1 - k:width - k, :][None]
    return acc.reshape(rows, cols)


def _in_proj_body(mode, rows, n_valid,
                  x_ref, g0_ref, b0_ref, wbig_ref, wsm_ref, bsm_ref, alog_ref,
                  wmc_ref, bmc_ref, wsc_ref, bsc_ref, cin_qk_ref, cin_xbc_ref,
                  xn_ref, q_ref, k_ref, v_ref, gate_ref, xs_ref, bm_ref, cm_ref,
                  so_ref, sz_ref, sg_ref, cout_qk_ref, cout_xbc_ref, *scratch):
    xn = _layer_norm(x_ref[...], g0_ref[...], b0_ref[...])
    xn_ref[...] = xn
    xb = xn.astype(BF16)

    row = lax.broadcasted_iota(jnp.int32, (rows, 1), 0)
    if mode == "cyc":
        valid = (row % SEQ_PAD) < DEC_SEQ
    else:
        valid = row < n_valid

    g = _dot(xb, wsm_ref[...]) + bsm_ref[...]
    lane = lax.broadcasted_iota(jnp.int32, (1, LANES), 1)
    sp = _softplus(g)
    lsg = -_softplus(-g)
    a_row = jnp.where((lane >= LANE_DTA) & (lane < LANE_DT), -jnp.exp(alog_ref[...]), 0.0)
    tile = jnp.where(lane < LANE_LF, g,
                     jnp.where(lane < LANE_DTA, lsg,
                               jnp.where(lane < LANE_DT, sp * a_row,
                                         jnp.where(lane < LANE_END, sp, 0.0))))
    pad_tile = jnp.where(lane < LANE_LF, -jnp.inf, 0.0)
    gate_ref[...] = jnp.where(valid, tile, pad_tile)

    first = pl.program_id(1) == 0
    u_qk = _dot(xb, wbig_ref[:, OFF_QK:OFF_QK + 2 * M_QK])
    if mode == "cyc":
        c_qk = _conv_cyc(u_qk, valid, cin_qk_ref, cout_qk_ref, wmc_ref, bmc_ref, M_CONV, rows)
    else:
        c_qk = _conv_seq(u_qk, scratch[0], cin_qk_ref, cout_qk_ref, wmc_ref, bmc_ref,
                         M_CONV, rows, n_valid, first)
    qk = _silu(c_qk)
    q_ref[...] = (qk[:, :M_QK] * (M_DK ** -0.5)).astype(q_ref.dtype)
    k_ref[...] = qk[:, M_QK:].astype(k_ref.dtype)

    v_ref[...] = _dot(xb, wbig_ref[:, OFF_V:OFF_V + M_V]).astype(v_ref.dtype)
    so_ref[...] = jax.nn.sigmoid(_dot(xb, wbig_ref[:, OFF_O:OFF_O + M_V]))
    sz_ref[...] = _silu(_dot(xb, wbig_ref[:, OFF_Z:OFF_Z + S_INNER]))

    u_xbc = _dot(xb, wbig_ref[:, OFF_XBC:OFF_XBC + S_XBC])
    if mode == "cyc":
        c_xbc = _conv_cyc(u_xbc, valid, cin_xbc_ref, cout_xbc_ref, wsc_ref, bsc_ref, S_CONV, rows)
    else:
        c_xbc = _conv_seq(u_xbc, scratch[1], cin_xbc_ref, cout_xbc_ref, wsc_ref, bsc_ref,
                          S_CONV, rows, n_valid, first)
    xbc = _silu(c_xbc)
    xs_ref[...] = xbc[:, :S_INNER]
    bm_ref[...] = xbc[:, S_INNER:S_INNER + S_GROUPS * S_STATE].astype(bm_ref.dtype)
    cm_ref[...] = xbc[:, S_INNER + S_GROUPS * S_STATE:].astype(cm_ref.dtype)

    sg_ref[...] = jax.nn.sigmoid(_dot(xb, wbig_ref[:, OFF_GATE:OFF_GATE + 2 * D_MODEL]))


def _const_spec(shape):
    nd = len(shape)
    return pl.BlockSpec(shape, lambda b, c: (0,) * nd, pipeline_mode=pl.Buffered(1))


def _in_proj(x_rows, cin_qk, cin_xbc, wts, *, mode, nseq, rows, n_valid, act_dtype):
    total = x_rows.shape[0]
    nt = total // (nseq * rows)
    row_map = lambda b, c: (b * nt + c, 0)

    def rspec(cols):
        return pl.BlockSpec((rows, cols), row_map)

    if mode == "seq":
        cin_specs = [pl.BlockSpec((None, SUBLANES, 2 * M_QK), lambda b, c: (0, 0, 0)),
                     pl.BlockSpec((None, SUBLANES, S_XBC), lambda b, c: (0, 0, 0))]
        cout_specs = [pl.BlockSpec((None, SUBLANES, 2 * M_QK), lambda b, c: (b, 0, 0)),
                      pl.BlockSpec((None, SUBLANES, S_XBC), lambda b, c: (b, 0, 0))]
        cout_shapes = [jax.ShapeDtypeStruct((nseq, SUBLANES, 2 * M_QK), F32),
                       jax.ShapeDtypeStruct((nseq, SUBLANES, S_XBC), F32)]
        scratch = [pltpu.VMEM((rows + SUBLANES, 2 * M_QK), F32),
                   pltpu.VMEM((rows + SUBLANES, S_XBC), F32)]
    else:
        cin_specs = [rspec(2 * M_QK), rspec(S_XBC)]
        cout_specs = [rspec(2 * M_QK), rspec(S_XBC)]
        cout_shapes = [jax.ShapeDtypeStruct((total, 2 * M_QK), F32),
                       jax.ShapeDtypeStruct((total, S_XBC), F32)]
        scratch = []

    in_specs = [rspec(D_MODEL), _const_spec((1, D_MODEL)), _const_spec((1, D_MODEL)),
                _const_spec((D_MODEL, W_BIG_COLS)), _const_spec((D_MODEL, LANES)),
                _const_spec((1, LANES)), _const_spec((1, LANES)),
                _const_spec((M_CONV, 2 * M_QK)), _const_spec((1, 2 * M_QK)),
                _const_spec((S_CONV, S_XBC)), _const_spec((1, S_XBC))] + cin_specs
    out_cols = [(D_MODEL, F32), (M_QK, act_dtype), (M_QK, act_dtype), (M_V, act_dtype), (LANES, F32),
                (S_INNER, F32), (S_GROUPS * S_STATE, act_dtype), (S_GROUPS * S_STATE, act_dtype),
                (M_V, F32), (S_INNER, F32), (2 * D_MODEL, F32)]
    out_specs = [rspec(c) for c, _ in out_cols] + cout_specs
    out_shapes = [jax.ShapeDtypeStruct((total, c), dt) for c, dt in out_cols] + cout_shapes
    return pl.pallas_call(
        functools.partial(_in_proj_body, mode, rows, n_valid),
        grid=(nseq, nt), in_specs=in_specs, out_specs=out_specs, out_shape=out_shapes,
        scratch_shapes=scratch, name=f"in_proj_{mode}_{nseq}",
        compiler_params=pltpu.CompilerParams(
            dimension_semantics=("arbitrary", "arbitrary"), vmem_limit_bytes=VMEM_LIMIT_BYTES),
    )(x_rows, wts["ln0_g"], wts["ln0_b"], wts["w_big"], wts["w_small"], wts["b_small"], wts["a_log"],
      wts["w_mconv"], wts["b_mconv"], wts["w_sconv"], wts["b_sconv"], cin_qk, cin_xbc)


def _cumsum_rows(x):
    row = lax.broadcasted_iota(jnp.int32, x.shape, 0)
    k = 1
    while k < x.shape[0]:
        x = x + jnp.where(row >= k, pltpu.roll(x, k, 0), 0.0)
        k *= 2
    return x


def _recur_body(rows_blk,
                q_ref, k_ref, v_ref, gate_ref, xs_ref, bm_ref, cm_ref,
                c0_ref, n0_ref, m0_ref, s0_ref,
                h_ref, y_ref, c1_ref, n1_ref, m1_ref, s1_ref):
    @pl.when(pl.program_id(1) == 0)
    def _():
        c1_ref[...] = c0_ref[...]
        n1_ref[...] = n0_ref[...]
        m1_ref[...] = m0_ref[...]
        s1_ref[...] = s0_ref[...]

    pad = CHUNK - rows_blk
    lane = lax.broadcasted_iota(jnp.int32, (1, LANES), 1)

    def load(ref, dtype):
        x = ref[...]
        if pad:
            x = jnp.concatenate([x.astype(F32), jnp.zeros((pad, x.shape[1]), F32)], axis=0)
        return x.astype(dtype)

    gt = gate_ref[...]
    if pad:
        pad_tile = jnp.broadcast_to(jnp.where(lane < LANE_LF, -jnp.inf, 0.0), (pad, LANES))
        gt = jnp.concatenate([gt, pad_tile], axis=0)
    cum_lanes = (lane >= LANE_LF) & (lane < LANE_DT)
    gc = jnp.where(cum_lanes, _cumsum_rows(jnp.where(cum_lanes, gt, 0.0)), gt)
    gct = gc.T

    ti = lax.broadcasted_iota(jnp.int32, (CHUNK, CHUNK), 0)
    si = lax.broadcasted_iota(jnp.int32, (CHUNK, CHUNK), 1)
    causal = si <= ti

    q = load(q_ref, BF16)
    k = load(k_ref, BF16)
    v = load(v_ref, BF16)

    m_row = m1_ref[...]
    m_new_row = m_row
    h_parts = []
    for h in range(M_HEADS):
        qh = q[:, h * M_DK:(h + 1) * M_DK]
        kh = k[:, h * M_DK:(h + 1) * M_DK]
        vh = v[:, h * M_DV:(h + 1) * M_DV]
        bt_c = gc[:, LANE_LF + h:LANE_LF + h + 1]
        ig_c = gc[:, LANE_IG + h:LANE_IG + h + 1]
        bt_r = gct[LANE_LF + h:LANE_LF + h + 1, :]
        ig_r = gct[LANE_IG + h:LANE_IG + h + 1, :]
        m0 = m_row[:, h:h + 1]
        c0 = c1_ref[h]
        n0 = n1_ref[h:h + 1, :]

        dmat = jnp.where(causal, bt_c - bt_r + ig_r, -jnp.inf)
        inter = bt_c + m0
        m_t = jnp.maximum(inter, jnp.max(dmat, axis=1, keepdims=True))
        w_intra = jnp.exp(dmat - m_t)
        w_inter = jnp.exp(inter - m_t)
        s = _dot_nt(qh, kh) * w_intra
        num = _dot(s.astype(BF16), vh) + _dot(qh, c0.astype(BF16)) * w_inter
        den = (jnp.sum(s, axis=1, keepdims=True)
               + jnp.sum(qh.astype(F32) * n0, axis=1, keepdims=True) * w_inter)
        denom = jnp.maximum(jnp.abs(den), jnp.exp(-m_t))
        h_parts.append(num / denom)

        b_last = bt_c[CHUNK - 1:CHUNK, :]
        m_new = jnp.maximum(b_last + m0, jnp.max(b_last - bt_r + ig_r, axis=1, keepdims=True))
        w_last = jnp.exp(b_last - bt_c + ig_c - m_new)
        decay = jnp.exp(b_last + m0 - m_new)
        kw = kh.astype(F32) * w_last
        c1_ref[h] = decay * c0 + _dot_tn(kw.astype(BF16), vh)
        n1_ref[h:h + 1, :] = decay * n0 + jnp.sum(kw, axis=0, keepdims=True)
        m_new_row = jnp.where(lane == h, m_new, m_new_row)
    m1_ref[...] = m_new_row
    h_ref[...] = jnp.concatenate(h_parts, axis=1)[:rows_blk]

    xs = load(xs_ref, F32)
    bm = load(bm_ref, BF16)
    cm = load(cm_ref, BF16)
    lane_sq = lax.broadcasted_iota(jnp.int32, (CHUNK, LANES), 1)
    lo_half = lane_sq < S_HEADDIM
    y_parts = []
    for g in range(S_GROUPS):
        cg = cm[:, g * S_STATE:(g + 1) * S_STATE]
        bg = bm[:, g * S_STATE:(g + 1) * S_STATE]
        sg = s1_ref[g * S_GROUP_W:(g + 1) * S_GROUP_W, :]
        cb = _dot_nt(cg, bg)
        y_inter = _dot_nt(cg, sg.astype(BF16))
        xg = xs[:, g * S_GROUP_W:(g + 1) * S_GROUP_W]
        xgt = xg.T
        a_blocks = []
        for p in range(S_HEADS_PER_GROUP // 2):
            ms, es = [], []
            for r in (2 * p, 2 * p + 1):
                hh = g * S_HEADS_PER_GROUP + r
                bt_c = gc[:, LANE_DTA + hh:LANE_DTA + hh + 1]
                bt_r = gct[LANE_DTA + hh:LANE_DTA + hh + 1, :]
                dt_r = gct[LANE_DT + hh:LANE_DT + hh + 1, :]
                dec = jnp.exp(jnp.where(causal, bt_c - bt_r, -jnp.inf))
                ms.append((cb * dec * dt_r).astype(BF16))
                es.append(jnp.exp(bt_c))
                b_last = bt_c[CHUNK - 1:CHUNK, :]
                wd = jnp.exp(b_last - bt_r) * dt_r
                a_blocks.append(xgt[r * S_HEADDIM:(r + 1) * S_HEADDIM, :] * wd)
                rows_s = slice(g * S_GROUP_W + r * S_HEADDIM, g * S_GROUP_W + (r + 1) * S_HEADDIM)
                s1_ref[rows_s, :] = jnp.exp(b_last) * s1_ref[rows_s, :]
            xp = xg[:, p * LANES:(p + 1) * LANES].astype(BF16)
            zero = jnp.zeros_like(xp)
            rhs = jnp.concatenate([jnp.where(lo_half, xp, zero), jnp.where(lo_half, zero, xp)], axis=0)
            lhs = jnp.concatenate(ms, axis=1)
            y_intra = _dot(lhs, rhs)
            e_pair = jnp.where(lo_half, es[0], es[1])
            y_parts.append(y_intra + y_inter[:, p * LANES:(p + 1) * LANES] * e_pair)
        a_mat = jnp.concatenate(a_blocks, axis=0).astype(BF16)
        rows_g = slice(g * S_GROUP_W, (g + 1) * S_GROUP_W)
        s1_ref[rows_g, :] = s1_ref[rows_g, :] + _dot(a_mat, bg)
    y_ref[...] = jnp.concatenate(y_parts, axis=1)[:rows_blk]


def _recur(acts, c0, n0, m0, s0, *, nseq, nchunk, rows_blk, shared_init):
    q, k, v, gate, xs, bm, cm = acts
    total = q.shape[0]
    row_map = lambda b, c: (b * nchunk + c, 0)

    def rspec(cols):
        return pl.BlockSpec((rows_blk, cols), row_map)

    def sspec(shape, shared):
        nd = len(shape)
        if shared:
            return pl.BlockSpec((None,) + shape, lambda b, c: (0,) * (nd + 1))
        return pl.BlockSpec((None,) + shape, lambda b, c: (b,) + (0,) * nd)

    st_shapes = [(M_HEADS, M_DK, M_DV), (M_HEADS, M_DK), (1, LANES), (S_INNER, S_STATE)]
    in_specs = ([rspec(M_QK), rspec(M_QK), rspec(M_V), rspec(LANES), rspec(S_INNER),
                 rspec(S_GROUPS * S_STATE), rspec(S_GROUPS * S_STATE)]
                + [sspec(s, shared_init) for s in st_shapes])
    out_specs = [rspec(M_V), rspec(S_INNER)] + [sspec(s, False) for s in st_shapes]
    out_shapes = ([jax.ShapeDtypeStruct((total, M_V), F32), jax.ShapeDtypeStruct((total, S_INNER), F32)]
                  + [jax.ShapeDtypeStruct((nseq,) + s, F32) for s in st_shapes])
    return pl.pallas_call(
        functools.partial(_recur_body, rows_blk),
        grid=(nseq, nchunk), in_specs=in_specs, out_specs=out_specs, out_shape=out_shapes,
        name=f"recur_{nseq}",
        compiler_params=pltpu.CompilerParams(
            dimension_semantics=("arbitrary", "arbitrary"), vmem_limit_bytes=VMEM_LIMIT_BYTES),
    )(q, k, v, gate, xs, bm, cm, c0, n0, m0, s0)


def _post_body(h_ref, y_ref, xs_ref, so_ref, sz_ref, sg_ref, xn_ref,
               mg_ref, sng_ref, dexp_ref, wpa_ref, wpb_ref, wout_ref, g1_ref, b1_ref, x1_ref):
    hs = []
    for h in range(M_HEADS):
        hh = h_ref[:, h * M_DV:(h + 1) * M_DV]
        mu = jnp.mean(hh, axis=-1, keepdims=True)
        hc = hh - mu
        var = jnp.mean(hc * hc, axis=-1, keepdims=True)
        hs.append(hc * lax.rsqrt(var + LN_EPS))
    hn = jnp.concatenate(hs, axis=1) * mg_ref[...]
    ya = _dot((so_ref[...] * hn).astype(BF16), wpa_ref[...])

    y = (y_ref[...] + dexp_ref[...] * xs_ref[...]) * sz_ref[...]
    ys = []
    for g in range(S_GROUPS):
        yg = y[:, g * S_GROUP_W:(g + 1) * S_GROUP_W]
        ys.append(yg * lax.rsqrt(jnp.mean(yg * yg, axis=-1, keepdims=True) + RMS_EPS))
    yn = jnp.concatenate(ys, axis=1) * sng_ref[...]
    yb = _dot(yn.astype(BF16), wpb_ref[...])

    gates = sg_ref[...]
    mixed = _dot((gates[:, :D_MODEL] * ya + gates[:, D_MODEL:] * yb).astype(BF16), wout_ref[...])
    x1_ref[...] = _layer_norm(ALPHA * xn_ref[...] + mixed, g1_ref[...], b1_ref[...])


def _post(h, y, xs, so, sz, sg, xn, wts, *, rows):
    total = h.shape[0]
    row_map = lambda i: (i, 0)

    def rspec(cols):
        return pl.BlockSpec((rows, cols), row_map)

    def cspec(shape):
        return pl.BlockSpec(shape, lambda i: (0, 0), pipeline_mode=pl.Buffered(1))

    in_specs = [rspec(M_V), rspec(S_INNER), rspec(S_INNER), rspec(M_V), rspec(S_INNER),
                rspec(2 * D_MODEL), rspec(D_MODEL),
                cspec((1, M_V)), cspec((1, S_INNER)), cspec((1, S_INNER)),
                cspec((M_V, D_MODEL)), cspec((S_INNER, D_MODEL)), cspec((D_MODEL, D_MODEL)),
                cspec((1, D_MODEL)), cspec((1, D_MODEL))]
    return pl.pallas_call(
        _post_body, grid=(total // rows,), in_specs=in_specs, out_specs=rspec(D_MODEL),
        out_shape=jax.ShapeDtypeStruct((total, D_MODEL), F32), name=f"post_{total}",
        compiler_params=pltpu.CompilerParams(
            dimension_semantics=("arbitrary",), vmem_limit_bytes=VMEM_LIMIT_BYTES),
    )(h, y, xs, so, sz, sg, xn, wts["mnorm_g"], wts["snorm_g"], wts["d_exp"],
      wts["w_proj_a"], wts["w_proj_b"], wts["w_out"], wts["ln1_g"], wts["ln1_b"])


def _ffn_body(mode, rows, n_valid,
              x1_ref, wup_ref, wfc_ref, bfc_ref, wdn_ref, g2_ref, b2_ref, cin_ref,
              y_ref, cout_ref, *scratch):
    x1 = x1_ref[...]
    up = _dot(x1.astype(BF16), wup_ref[...])
    if mode == "cyc":
        row = lax.broadcasted_iota(jnp.int32, (rows, 1), 0)
        valid = (row % SEQ_PAD) < DEC_SEQ
        upc = _conv_cyc(up, valid, cin_ref, cout_ref, wfc_ref, bfc_ref, F_CONV, rows)
    else:
        upc = _conv_seq(up, scratch[0], cin_ref, cout_ref, wfc_ref, bfc_ref,
                        F_CONV, rows, n_valid, pl.program_id(1) == 0)
    act = (_silu(upc[:, :D_FF]) * upc[:, D_FF:]).astype(BF16)
    ff = _dot(act, wdn_ref[...])
    y_ref[...] = _layer_norm(ALPHA * x1 + ff, g2_ref[...], b2_ref[...])


def _ffn(x1, cin, wts, *, mode, nseq, rows, n_valid):
    total = x1.shape[0]
    nt = total // (nseq * rows)
    row_map = lambda b, c: (b * nt + c, 0)

    def rspec(cols):
        return pl.BlockSpec((rows, cols), row_map)

    if mode == "seq":
        cin_spec = pl.BlockSpec((None, SUBLANES, 2 * D_FF), lambda b, c: (0, 0, 0))
        cout_spec = pl.BlockSpec((None, SUBLANES, 2 * D_FF), lambda b, c: (b, 0, 0))
        cout_shape = jax.ShapeDtypeStruct((nseq, SUBLANES, 2 * D_FF), F32)
        scratch = [pltpu.VMEM((rows + SUBLANES, 2 * D_FF), F32)]
    else:
        cin_spec = rspec(2 * D_FF)
        cout_spec = rspec(2 * D_FF)
        cout_shape = jax.ShapeDtypeStruct((total, 2 * D_FF), F32)
        scratch = []
    in_specs = [rspec(D_MODEL), _const_spec((D_MODEL, 2 * D_FF)), _const_spec((F_CONV, 2 * D_FF)),
                _const_spec((1, 2 * D_FF)), _const_spec((D_FF, D_MODEL)),
                _const_spec((1, D_MODEL)), _const_spec((1, D_MODEL)), cin_spec]
    return pl.pallas_call(
        functools.partial(_ffn_body, mode, rows, n_valid),
        grid=(nseq, nt), in_specs=in_specs, out_specs=[rspec(D_MODEL), cout_spec],
        out_shape=[jax.ShapeDtypeStruct((total, D_MODEL), F32), cout_shape],
        scratch_shapes=scratch, name=f"ffn_{mode}_{nseq}",
        compiler_params=pltpu.CompilerParams(
            dimension_semantics=("arbitrary", "arbitrary"), vmem_limit_bytes=VMEM_LIMIT_BYTES),
    )(x1, wts["w_up"], wts["w_fconv"], wts["b_fconv"], wts["w_down"], wts["ln2_g"], wts["ln2_b"], cin)


def _layer(x_rows, cins, states, wts, *, mode, nseq, seq_rows, tile_rows, n_valid, rows_blk,
           shared_init, act_dtype):
    cin_qk, cin_xbc, cin_ffn = cins
    (xn, q, k, v, gate, xs, bm, cm, so, sz, sg, cout_qk, cout_xbc) = _in_proj(
        x_rows, cin_qk, cin_xbc, wts, mode=mode, nseq=nseq if mode == "seq" else 1,
        rows=tile_rows, n_valid=n_valid, act_dtype=act_dtype)
    nchunk = seq_rows // rows_blk
    h, y, c1, n1, m1, s1 = _recur((q, k, v, gate, xs, bm, cm), *states, nseq=nseq, nchunk=nchunk,
                                  rows_blk=rows_blk, shared_init=shared_init)
    x1 = _post(h, y, xs, so, sz, sg, xn, wts, rows=tile_rows)
    out, cout_ffn = _ffn(x1, cin_ffn, wts, mode=mode, nseq=nseq if mode == "seq" else 1,
                         rows=tile_rows, n_valid=n_valid)
    return out, (cout_qk, cout_xbc, cout_ffn), (c1, n1, m1, s1)


def _state_rows(state, width):
    n, w, c = state.shape
    return jnp.pad(state, ((0, 0), (SEQ_PAD - w, 0), (0, 0))).reshape(n * SEQ_PAD, c)


def kernel(x_prompt, x_sample, state_mlstm_conv, state_mlstm_C, state_mlstm_n, state_mlstm_m, state_ssm_conv, state_ssm, state_ffn_conv, meta_tokens, ln0_g, ln0_b, w_in, b_mlstm_if, w_mlstm_conv, b_mlstm_conv, mlstm_norm_g, w_proj_a, w_ssm_conv, b_ssm_conv, ssm_dt_bias, ssm_A_log, ssm_D, ssm_norm_g, w_proj_b, w_out, ln1_g, ln1_b, w_up, w_ffn_conv, b_ffn_conv, w_down, ln2_g, ln2_b):
    batch, seq, _ = x_prompt.shape
    dec_batch, dec_seq, _ = x_sample.shape
    assert dec_seq == DEC_SEQ and seq % (2 * CHUNK) == 0 and meta_tokens.shape[0] == N_META

    w = w_in[0]
    o_i = 2 * M_QK + 2 * M_V
    o_z = o_i + 2 * M_HEADS
    o_xbc = o_z + S_INNER
    o_dt = o_xbc + S_XBC
    o_gate = o_dt + S_HEADS
    w_big = jnp.concatenate([w[:, :o_i], w[:, o_z:o_dt], w[:, o_gate:]], axis=1).astype(BF16)
    w_dt = w[:, o_dt:o_gate]
    w_small = jnp.concatenate(
        [w[:, o_i:o_z], w_dt, w_dt, jnp.zeros((D_MODEL, LANES - LANE_END), F32)], axis=1).astype(BF16)
    lane_pad = jnp.zeros((LANES - LANE_END,), F32)
    b_small = jnp.concatenate([b_mlstm_if[0], ssm_dt_bias[0], ssm_dt_bias[0], lane_pad])[None]
    a_log = jnp.concatenate([jnp.zeros((LANE_DTA,), F32), ssm_A_log[0],
                             jnp.zeros((LANES - LANE_DT,), F32)])[None]
    wts = {
        "ln0_g": ln0_g[None], "ln0_b": ln0_b[None], "w_big": w_big, "w_small": w_small,
        "b_small": b_small, "a_log": a_log,
        "w_mconv": w_mlstm_conv[0], "b_mconv": b_mlstm_conv, "w_sconv": w_ssm_conv[0], "b_sconv": b_ssm_conv,
        "mnorm_g": mlstm_norm_g, "snorm_g": ssm_norm_g, "d_exp": jnp.repeat(ssm_D[0], S_HEADDIM)[None],
        "w_proj_a": w_proj_a[0].astype(BF16), "w_proj_b": w_proj_b[0].astype(BF16),
        "w_out": w_out[0].astype(BF16), "ln1_g": ln1_g, "ln1_b": ln1_b,
        "w_up": w_up[0].astype(BF16), "w_fconv": w_ffn_conv[0], "b_fconv": b_ffn_conv,
        "w_down": w_down[0].astype(BF16), "ln2_g": ln2_g, "ln2_b": ln2_b,
    }

    x_meta = jnp.pad(meta_tokens, ((0, CHUNK - N_META), (0, 0)))
    zero_cins = (jnp.zeros((1, SUBLANES, 2 * M_QK), F32), jnp.zeros((1, SUBLANES, S_XBC), F32),
                 jnp.zeros((1, SUBLANES, 2 * D_FF), F32))
    zero_states = (jnp.zeros((1, M_HEADS, M_DK, M_DV), F32), jnp.zeros((1, M_HEADS, M_DK), F32),
                   jnp.zeros((1, 1, LANES), F32), jnp.zeros((1, S_INNER, S_STATE), F32))
    _, meta_couts, meta_states = _layer(
        x_meta, zero_cins, zero_states, wts, mode="seq", nseq=1, seq_rows=CHUNK, tile_rows=CHUNK,
        n_valid=N_META, rows_blk=CHUNK, shared_init=True, act_dtype=BF16)

    tile_rows = 2 * CHUNK
    y_p, p_couts, p_states = _layer(
        x_prompt.reshape(batch * seq, D_MODEL), meta_couts, meta_states, wts, mode="seq", nseq=batch,
        seq_rows=seq, tile_rows=tile_rows, n_valid=tile_rows, rows_blk=CHUNK, shared_init=True,
        act_dtype=BF16)

    x_s = jnp.pad(x_sample, ((0, 0), (0, SEQ_PAD - dec_seq), (0, 0))).reshape(dec_batch * SEQ_PAD, D_MODEL)
    s_cins = (_state_rows(state_mlstm_conv[0], M_CONV), _state_rows(state_ssm_conv[0], S_CONV),
              _state_rows(state_ffn_conv[0], F_CONV))
    m_in = jnp.pad(state_mlstm_m[0], ((0, 0), (0, LANES - M_HEADS)))[:, None, :]
    s_states = (state_mlstm_C[0], state_mlstm_n[0], m_in, state_ssm[0].reshape(dec_batch, S_INNER, S_STATE))
    y_s, s_couts, s_states_new = _layer(
        x_s, s_cins, s_states, wts, mode="cyc", nseq=dec_batch, seq_rows=SEQ_PAD, tile_rows=CHUNK,
        n_valid=CHUNK, rows_blk=SEQ_PAD, shared_init=False, act_dtype=F32)

    def pack_states(couts, states, n, conv_rows):
        cq, cx, cf = conv_rows(couts[0], M_CONV), conv_rows(couts[1], S_CONV), conv_rows(couts[2], F_CONV)
        c1, n1, m1, s1 = states
        return (cq[None], c1[None], n1[None], m1[:, 0, :M_HEADS][None], cx[None],
                s1.reshape(n, S_HEADS, S_HEADDIM, S_STATE)[None], cf[None])

    p_out = pack_states(p_couts, p_states, batch, lambda t, wd: t[:, SUBLANES - (wd - 1):, :])
    s_out = pack_states(
        s_couts, s_states_new, dec_batch,
        lambda t, wd: t.reshape(dec_batch, SEQ_PAD, -1)[:, dec_seq - (wd - 1):dec_seq, :])
    y_prompt = y_p.reshape(batch, seq, D_MODEL)
    y_sample = y_s.reshape(dec_batch, SEQ_PAD, D_MODEL)[:, :dec_seq]
    return (y_prompt, y_sample) + p_out + s_out
```

```python
import functools

import jax
import jax.numpy as jnp
from jax import lax
from jax.experimental import pallas as pl
from jax.experimental.pallas import tpu as pltpu

F32 = jnp.float32
BF16 = jnp.bfloat16

D_MODEL = 1024
N_META = 16
M_HEADS = 4
M_DK = 128
M_DV = 256
M_QK = M_HEADS * M_DK
M_V = M_HEADS * M_DV
M_CONV = 4
S_INNER = 2048
S_HEADDIM = 64
S_HEADS = 32
S_GROUPS = 4
S_STATE = 128
S_CONV = 4
S_BC = S_GROUPS * S_STATE
S_XBC = S_INNER + 2 * S_BC
S_GROUP_W = S_INNER // S_GROUPS
S_HEADS_PER_GROUP = S_HEADS // S_GROUPS
D_FF = 2816
F_CONV = 3
ALPHA = 2.0 ** 0.25
LN_EPS = 1e-5
RMS_EPS = 1e-5

LANES = 128
SUBLANES = 8
BF16_ROWS = 16
CHUNK = 128
SEQ_PAD = SUBLANES
DEC_SEQ = 4
SAMPLE_SEQS_PER_STEP = 4
PACK_COLS = 512
FFN_BLOCK = 256
VMEM_LIMIT_BYTES = 60 * 1024 * 1024

OFF_QK = 0
OFF_V = OFF_QK + 2 * M_QK
OFF_O = OFF_V + M_V
OFF_Z = OFF_O + M_V
OFF_XBC = OFF_Z + S_INNER
OFF_GATE = OFF_XBC + S_XBC
W_BIG_COLS = OFF_GATE + 2 * D_MODEL
LANE_IG = 0
LANE_LF = M_HEADS
LANE_DTA = 2 * M_HEADS
LANE_DT = LANE_DTA + S_HEADS
LANE_END = LANE_DT + S_HEADS

STATE_SHAPES = [(M_HEADS, M_DK, M_DV), (M_HEADS, M_DK), (1, LANES), (S_INNER, S_STATE)]


def _softplus(x):
    return jnp.maximum(x, 0.0) + jnp.log1p(jnp.exp(-jnp.abs(x)))


def _silu(x):
    return x * jax.nn.sigmoid(x)


def _layer_norm(x, g, b):
    mu = jnp.mean(x, axis=-1, keepdims=True)
    xc = x - mu
    var = jnp.mean(xc * xc, axis=-1, keepdims=True)
    return xc * lax.rsqrt(var + LN_EPS) * g + b


def _dot(a, b):
    return jnp.dot(a, b, preferred_element_type=F32)


def _wdot(a, w_ref, lo=None, hi=None):
    w = w_ref[...] if lo is None else w_ref[:, lo:hi]
    return _dot(a, pltpu.bitcast(w, BF16))


def _dot_nt(a, b):
    return lax.dot_general(a, b, (((1,), (1,)), ((), ())), preferred_element_type=F32)


def _dot_tn(a, b):
    return lax.dot_general(a, b, (((0,), (0,)), ((), ())), preferred_element_type=F32)


def _pad_rows(x, rows):
    if x.shape[0] == rows:
        return x
    return jnp.concatenate([x, jnp.zeros((rows - x.shape[0], x.shape[1]), x.dtype)], axis=0)


def _conv_carry(u, carry_ref, w_ref, b_ref, width, n_valid, cols=slice(None)):
    full = jnp.concatenate([carry_ref[:, cols], u], axis=0)
    acc = b_ref[:, cols] + u * w_ref[width - 1:width, cols]
    for k in range(1, width):
        acc = acc + pltpu.roll(full, k, 0)[SUBLANES:] * w_ref[width - 1 - k:width - k, cols]
    carry_ref[:, cols] = u[n_valid - SUBLANES:n_valid]
    return acc


def _conv_cyc(u, valid, st_ref, upre_ref, w_ref, b_ref, width, rows, cols=slice(None)):
    ncol = u.shape[1]
    upre_ref[:, cols] = u
    f = (jnp.where(valid, u, 0.0) + st_ref[:, cols]).reshape(rows // SEQ_PAD, SEQ_PAD, ncol)
    acc = b_ref[:, cols][None] + f * w_ref[width - 1:width, cols][None]
    for k in range(1, width):
        acc = acc + pltpu.roll(f, k, 1) * w_ref[width - 1 - k:width - k, cols][None]
    return acc.reshape(rows, ncol)


def _gate_tile(xb, valid, wsm_ref, bsm_ref, alog_ref):
    g = _wdot(xb, wsm_ref) + bsm_ref[...]
    lane = lax.broadcasted_iota(jnp.int32, (1, LANES), 1)
    sp = _softplus(g)
    lsg = -_softplus(-g)
    a_row = jnp.where((lane >= LANE_DTA) & (lane < LANE_DT), -jnp.exp(alog_ref[...]), 0.0)
    tile = jnp.where(lane < LANE_LF, g,
                     jnp.where(lane < LANE_DTA, lsg,
                               jnp.where(lane < LANE_DT, sp * a_row,
                                         jnp.where(lane < LANE_END, sp, 0.0))))
    if valid is None:
        return tile
    return jnp.where(valid, tile, jnp.where(lane < LANE_LF, -jnp.inf, 0.0))


def _cumsum_rows(x):
    row = lax.broadcasted_iota(jnp.int32, x.shape, 0)
    k = 1
    while k < x.shape[0]:
        x = x + jnp.where(row >= k, pltpu.roll(x, k, 0), 0.0)
        k *= 2
    return x


def _chunk_recur(q, k, v, gt, xs, bm, cm, st_in, st_out, filler=None):
    c_ref, n_ref, m_ref, s_ref = st_in
    c_out, n_out, m_out, s_out = st_out
    tr = q.shape[0]
    lane = lax.broadcasted_iota(jnp.int32, (1, LANES), 1)
    cum_lanes = (lane >= LANE_LF) & (lane < LANE_DT)
    gc = jnp.where(cum_lanes, _cumsum_rows(jnp.where(cum_lanes, gt, 0.0)), gt)
    if tr < CHUNK:
        tail = jnp.where(lane < LANE_LF, -jnp.inf, jnp.where(cum_lanes, gc[tr - 1:tr, :], 0.0))
        gc_full = jnp.concatenate([gc, jnp.broadcast_to(tail, (CHUNK - tr, LANES))], axis=0)
    else:
        gc_full = gc
    gct = gc_full.T
    k_s = _pad_rows(k, CHUNK)
    bm_s = _pad_rows(bm, CHUNK)

    def head_dots(h):
        qh = q[:, h * M_DK:(h + 1) * M_DK]
        c0 = c_ref[h]
        return _dot_nt(qh, k_s[:, h * M_DK:(h + 1) * M_DK]), _dot(qh, c0.astype(BF16)), c0

    def group_dots(g):
        cg = cm[:, g * S_STATE:(g + 1) * S_STATE]
        s_old = s_ref[g * S_GROUP_W:(g + 1) * S_GROUP_W, :]
        return (_dot_nt(cg, bm_s[:, g * S_STATE:(g + 1) * S_STATE]),
                _dot_nt(cg, s_old.astype(BF16)), s_old)

    if filler is not None:
        head_pre = [head_dots(h) for h in range(M_HEADS)]
        group_pre = [group_dots(g) for g in range(S_GROUPS)]
        filler()
        head_dots = head_pre.__getitem__
        group_dots = group_pre.__getitem__

    last = gc[tr - 1:tr, :]
    ti = lax.broadcasted_iota(jnp.int32, (tr, CHUNK), 0)
    si = lax.broadcasted_iota(jnp.int32, (tr, CHUNK), 1)
    causal = si <= ti
    v_s = _pad_rows(v, CHUNK)

    m_row = m_ref[...]
    m_new_row = m_row
    h_parts = []
    for h in range(M_HEADS):
        qh = q[:, h * M_DK:(h + 1) * M_DK]
        vh = v_s[:, h * M_DV:(h + 1) * M_DV]
        bt_c = gc[:, LANE_LF + h:LANE_LF + h + 1]
        ig_c = gc[:, LANE_IG + h:LANE_IG + h + 1]
        bt_r = gct[LANE_LF + h:LANE_LF + h + 1, :]
        ig_r = gct[LANE_IG + h:LANE_IG + h + 1, :]
        m0 = m_row[:, h:h + 1]
        sqk, qc, c0 = head_dots(h)
        n0 = n_ref[h:h + 1, :]

        dmat = jnp.where(causal, bt_c - bt_r + ig_r, -jnp.inf)
        inter = bt_c + m0
        m_t = jnp.maximum(inter, jnp.max(dmat, axis=1, keepdims=True))
        w_intra = jnp.exp(dmat - m_t)
        w_inter = jnp.exp(inter - m_t)
        s = sqk * w_intra
        num = _dot(s.astype(BF16), vh) + qc * w_inter
        den = (jnp.sum(s, axis=1, keepdims=True)
               + jnp.sum(qh.astype(F32) * n0, axis=1, keepdims=True) * w_inter)
        denom = jnp.maximum(jnp.abs(den), jnp.exp(-m_t))
        h_parts.append(num / denom)

        b_last = last[:, LANE_LF + h:LANE_LF + h + 1]
        m_new = jnp.maximum(b_last + m0, jnp.max(b_last - bt_r + ig_r, axis=1, keepdims=True))
        w_last = jnp.exp(b_last - bt_c + ig_c - m_new)
        decay = jnp.exp(b_last + m0 - m_new)
        kw = k[:, h * M_DK:(h + 1) * M_DK].astype(F32) * w_last
        c_out[h] = decay * c0 + _dot_tn(kw.astype(BF16), v[:, h * M_DV:(h + 1) * M_DV])
        n_out[h:h + 1, :] = decay * n0 + jnp.sum(kw, axis=0, keepdims=True)
        m_new_row = jnp.where(lane == h, m_new, m_new_row)
    m_out[...] = m_new_row

    lo_t = lax.broadcasted_iota(jnp.int32, (tr, LANES), 1) < S_HEADDIM
    lo_s = lax.broadcasted_iota(jnp.int32, (CHUNK, LANES), 1) < S_HEADDIM
    xs_b = xs.astype(BF16)
    y_parts = []
    for g in range(S_GROUPS):
        rows_g = slice(g * S_GROUP_W, (g + 1) * S_GROUP_W)
        cb, y_inter, s_old = group_dots(g)
        wd_parts, dec_parts = [], []
        for p in range(S_HEADS_PER_GROUP // 2):
            ms, es, wds = [], [], []
            for r in (2 * p, 2 * p + 1):
                hh = g * S_HEADS_PER_GROUP + r
                bt_c = gc[:, LANE_DTA + hh:LANE_DTA + hh + 1]
                dt_c = gc[:, LANE_DT + hh:LANE_DT + hh + 1]
                bt_r = gct[LANE_DTA + hh:LANE_DTA + hh + 1, :]
                dt_r = gct[LANE_DT + hh:LANE_DT + hh + 1, :]
                b_last = last[:, LANE_DTA + hh:LANE_DTA + hh + 1]
                dec = jnp.exp(jnp.where(causal, bt_c - bt_r, -jnp.inf))
                ms.append((cb * dec * dt_r).astype(BF16))
                es.append(jnp.exp(bt_c))
                wds.append(jnp.exp(b_last - bt_c) * dt_c)
                dec_parts.append(jnp.broadcast_to(jnp.exp(b_last), (S_HEADDIM, 1)))
            col = g * S_GROUP_W + p * LANES
            xp = _pad_rows(xs_b[:, col:col + LANES], CHUNK)
            zero = jnp.zeros_like(xp)
            rhs = jnp.concatenate([jnp.where(lo_s, xp, zero), jnp.where(lo_s, zero, xp)], axis=0)
            y_intra = _dot(jnp.concatenate(ms, axis=1), rhs)
            y_parts.append(y_intra + y_inter[:, p * LANES:(p + 1) * LANES] * jnp.where(lo_t, es[0], es[1]))
            wd_parts.append(jnp.where(lo_t, wds[0], wds[1]))
        a_mat = (xs[:, rows_g] * jnp.concatenate(wd_parts, axis=1)).astype(BF16)
        s_add = _dot_tn(a_mat, bm[:, g * S_STATE:(g + 1) * S_STATE])
        s_out[rows_g, :] = jnp.concatenate(dec_parts, axis=0) * s_old + s_add
    return jnp.concatenate(h_parts, axis=1), jnp.concatenate(y_parts, axis=1)


def _post_math(h, y, xs, so, sz, sg, xn, mg_ref, sng_ref, dexp_ref, wpa_ref, wpb_ref, wout_ref,
               g1_ref, b1_ref):
    hs = []
    for i in range(M_HEADS):
        hh = h[:, i * M_DV:(i + 1) * M_DV]
        mu = jnp.mean(hh, axis=-1, keepdims=True)
        hc = hh - mu
        var = jnp.mean(hc * hc, axis=-1, keepdims=True)
        hs.append(hc * lax.rsqrt(var + LN_EPS))
    hn = jnp.concatenate(hs, axis=1) * mg_ref[...]
    ya = _wdot((so * hn).astype(BF16), wpa_ref)

    y = (y + dexp_ref[...] * xs) * sz
    ys = []
    for g in range(S_GROUPS):
        yg = y[:, g * S_GROUP_W:(g + 1) * S_GROUP_W]
        ys.append(yg * lax.rsqrt(jnp.mean(yg * yg, axis=-1, keepdims=True) + RMS_EPS))
    yn = jnp.concatenate(ys, axis=1) * sng_ref[...]
    yb = _wdot(yn.astype(BF16), wpb_ref)

    mixed = _wdot((sg[:, :D_MODEL] * ya + sg[:, D_MODEL:] * yb).astype(BF16), wout_ref)
    return _layer_norm(ALPHA * xn + mixed, g1_ref[...], b1_ref[...])


def _const_spec(shape):
    nd = len(shape)
    return pl.BlockSpec(shape, lambda *_: (0,) * nd, pipeline_mode=pl.Buffered(1))


def _pack_body(w_ref, o_ref):
    o_ref[...] = pltpu.bitcast(w_ref[...].astype(BF16), jnp.uint32)


def _pack_weight(w, name):
    k, n = w.shape
    bn = min(n, PACK_COLS)
    return pl.pallas_call(
        _pack_body, grid=(n // bn,), in_specs=[pl.BlockSpec((k, bn), lambda j: (0, j))],
        out_specs=pl.BlockSpec((k // 2, bn), lambda j: (0, j)),
        out_shape=jax.ShapeDtypeStruct((k // 2, n), jnp.uint32), name=f"pack_{name}",
        compiler_params=_params(1),
    )(w)


def _params(dims):
    return pltpu.CompilerParams(dimension_semantics=("arbitrary",) * dims,
                                vmem_limit_bytes=VMEM_LIMIT_BYTES)


def _mixer_body(rows, n_valid,
                x_ref, g0_ref, b0_ref, wbig_ref, wsm_ref, bsm_ref, alog_ref,
                wmc_ref, bmc_ref, wsc_ref, bsc_ref, cin_qk_ref, cin_xbc_ref,
                c0_ref, n0_ref, m0_ref, s0_ref,
                mg_ref, sng_ref, dexp_ref, wpa_ref, wpb_ref, wout_ref, g1_ref, b1_ref,
                x1_ref, cout_qk_ref, cout_xbc_ref, c1_ref, n1_ref, m1_ref, s1_ref):
    @pl.when(pl.program_id(1) == 0)
    def _():
        cout_qk_ref[...] = cin_qk_ref[...]
        cout_xbc_ref[...] = cin_xbc_ref[...]
        c1_ref[...] = c0_ref[...]
        n1_ref[...] = n0_ref[...]
        m1_ref[...] = m0_ref[...]
        s1_ref[...] = s0_ref[...]

    xn = _layer_norm(x_ref[...], g0_ref[...], b0_ref[...])
    xb = xn.astype(BF16)
    valid = None
    if n_valid < rows:
        valid = lax.broadcasted_iota(jnp.int32, (rows, 1), 0) < n_valid
    gt = _gate_tile(xb, valid, wsm_ref, bsm_ref, alog_ref)

    u_qk = _wdot(xb, wbig_ref, OFF_QK, OFF_QK + 2 * M_QK)
    qk = _silu(_conv_carry(u_qk, cout_qk_ref, wmc_ref, bmc_ref, M_CONV, n_valid))
    q = (qk[:, :M_QK] * (M_DK ** -0.5)).astype(BF16)
    k = qk[:, M_QK:].astype(BF16)
    v = _wdot(xb, wbig_ref, OFF_V, OFF_V + M_V).astype(BF16)
    u_xbc = _wdot(xb, wbig_ref, OFF_XBC, OFF_XBC + S_XBC)
    xbc = _silu(_conv_carry(u_xbc, cout_xbc_ref, wsc_ref, bsc_ref, S_CONV, n_valid))
    xs = xbc[:, :S_INNER]
    bm = xbc[:, S_INNER:S_INNER + S_BC].astype(BF16)
    cm = xbc[:, S_INNER + S_BC:].astype(BF16)

    dense = {}
    fillers = [("o", OFF_O, M_V), ("z", OFF_Z, S_INNER), ("g", OFF_GATE, 2 * D_MODEL)]
    nchunk = rows // CHUNK
    hs, ys = [], []
    for ci in range(nchunk):
        sl = slice(ci * CHUNK, (ci + 1) * CHUNK)
        state = (c1_ref, n1_ref, m1_ref, s1_ref)
        mine = fillers[ci::nchunk]

        def filler(mine=mine):
            for name, off, width in mine:
                dense[name] = _wdot(xb, wbig_ref, off, off + width)

        h, y = _chunk_recur(q[sl], k[sl], v[sl], gt[sl], xs[sl], bm[sl], cm[sl], state, state, filler)
        hs.append(h)
        ys.append(y)
    h = jnp.concatenate(hs, axis=0)
    y = jnp.concatenate(ys, axis=0)

    so = jax.nn.sigmoid(dense["o"])
    sz = _silu(dense["z"])
    sg = jax.nn.sigmoid(dense["g"])
    x1_ref[...] = _post_math(h, y, xs, so, sz, sg, xn, mg_ref, sng_ref, dexp_ref,
                             wpa_ref, wpb_ref, wout_ref, g1_ref, b1_ref)


def _mixer(x_rows, cin_qk, cin_xbc, states, wts, *, nseq, rows, n_valid):
    total = x_rows.shape[0]
    nt = total // (nseq * rows)
    rspec = pl.BlockSpec((rows, D_MODEL), lambda b, c: (b * nt + c, 0))

    def per_seq(shape):
        nd = len(shape)
        return pl.BlockSpec((None,) + shape, lambda b, c: (b,) + (0,) * nd)

    def shared(shape):
        nd = len(shape)
        return pl.BlockSpec((None,) + shape, lambda b, c: (0,) * (nd + 1), pipeline_mode=pl.Buffered(1))

    tails = [(SUBLANES, 2 * M_QK), (SUBLANES, S_XBC)]
    in_specs = ([rspec, _const_spec((1, D_MODEL)), _const_spec((1, D_MODEL)),
                 _const_spec((D_MODEL // 2, W_BIG_COLS)), _const_spec((D_MODEL // 2, LANES)),
                 _const_spec((1, LANES)), _const_spec((1, LANES)),
                 _const_spec((M_CONV, 2 * M_QK)), _const_spec((1, 2 * M_QK)),
                 _const_spec((S_CONV, S_XBC)), _const_spec((1, S_XBC))]
                + [shared(s) for s in tails + STATE_SHAPES]
                + [_const_spec((1, M_V)), _const_spec((1, S_INNER)), _const_spec((1, S_INNER)),
                   _const_spec((M_V // 2, D_MODEL)), _const_spec((S_INNER // 2, D_MODEL)),
                   _const_spec((D_MODEL // 2, D_MODEL)), _const_spec((1, D_MODEL)), _const_spec((1, D_MODEL))])
    out_specs = [rspec] + [per_seq(s) for s in tails + STATE_SHAPES]
    out_shapes = ([jax.ShapeDtypeStruct((total, D_MODEL), F32)]
                  + [jax.ShapeDtypeStruct((nseq,) + s, F32) for s in tails + STATE_SHAPES])
    outs = pl.pallas_call(
        functools.partial(_mixer_body, rows, n_valid),
        grid=(nseq, nt), in_specs=in_specs, out_specs=out_specs, out_shape=out_shapes,
        name=f"mixer_{nseq}", compiler_params=_params(2),
    )(x_rows, wts["ln0_g"], wts["ln0_b"], wts["w_big"], wts["w_small"], wts["b_small"], wts["a_log"],
      wts["w_mconv"], wts["b_mconv"], wts["w_sconv"], wts["b_sconv"], cin_qk, cin_xbc, *states,
      wts["mnorm_g"], wts["snorm_g"], wts["d_exp"], wts["w_proj_a"], wts["w_proj_b"], wts["w_out"],
      wts["ln1_g"], wts["ln1_b"])
    return outs[0], outs[1:3], outs[3:]


def _in_proj_body(rows,
                  x_ref, g0_ref, b0_ref, wbig_ref, wsm_ref, bsm_ref, alog_ref,
                  wmc_ref, bmc_ref, wsc_ref, bsc_ref, st_qk_ref, st_xbc_ref,
                  xn_ref, q_ref, k_ref, v_ref, gate_ref, xs_ref, bm_ref, cm_ref,
                  so_ref, sz_ref, sg_ref, upre_qk_ref, upre_xbc_ref):
    xn = _layer_norm(x_ref[...], g0_ref[...], b0_ref[...])
    xn_ref[...] = xn
    xb = xn.astype(BF16)
    valid = (lax.broadcasted_iota(jnp.int32, (rows, 1), 0) % SEQ_PAD) < DEC_SEQ
    gate_ref[...] = _gate_tile(xb, valid, wsm_ref, bsm_ref, alog_ref)

    u_qk = _wdot(xb, wbig_ref, OFF_QK, OFF_QK + 2 * M_QK)
    qk = _silu(_conv_cyc(u_qk, valid, st_qk_ref, upre_qk_ref, wmc_ref, bmc_ref, M_CONV, rows))
    q_ref[...] = qk[:, :M_QK] * (M_DK ** -0.5)
    k_ref[...] = qk[:, M_QK:]
    v_ref[...] = _wdot(xb, wbig_ref, OFF_V, OFF_V + M_V)
    so_ref[...] = jax.nn.sigmoid(_wdot(xb, wbig_ref, OFF_O, OFF_O + M_V))
    sz_ref[...] = _silu(_wdot(xb, wbig_ref, OFF_Z, OFF_Z + S_INNER))
    u_xbc = _wdot(xb, wbig_ref, OFF_XBC, OFF_XBC + S_XBC)
    xbc = _silu(_conv_cyc(u_xbc, valid, st_xbc_ref, upre_xbc_ref, wsc_ref, bsc_ref, S_CONV, rows))
    xs_ref[...] = xbc[:, :S_INNER]
    bm_ref[...] = xbc[:, S_INNER:S_INNER + S_BC]
    cm_ref[...] = xbc[:, S_INNER + S_BC:]
    sg_ref[...] = jax.nn.sigmoid(_wdot(xb, wbig_ref, OFF_GATE, OFF_GATE + 2 * D_MODEL))


def _in_proj(x_rows, st_qk, st_xbc, wts, *, rows):
    total = x_rows.shape[0]

    def rspec(cols):
        return pl.BlockSpec((rows, cols), lambda i: (i, 0))

    in_specs = [rspec(D_MODEL), _const_spec((1, D_MODEL)), _const_spec((1, D_MODEL)),
                _const_spec((D_MODEL // 2, W_BIG_COLS)), _const_spec((D_MODEL // 2, LANES)),
                _const_spec((1, LANES)), _const_spec((1, LANES)),
                _const_spec((M_CONV, 2 * M_QK)), _const_spec((1, 2 * M_QK)),
                _const_spec((S_CONV, S_XBC)), _const_spec((1, S_XBC)),
                rspec(2 * M_QK), rspec(S_XBC)]
    out_cols = [D_MODEL, M_QK, M_QK, M_V, LANES, S_INNER, S_BC, S_BC, M_V, S_INNER, 2 * D_MODEL,
                2 * M_QK, S_XBC]
    return pl.pallas_call(
        functools.partial(_in_proj_body, rows),
        grid=(total // rows,), in_specs=in_specs, out_specs=[rspec(c) for c in out_cols],
        out_shape=[jax.ShapeDtypeStruct((total, c), F32) for c in out_cols],
        name="sample_in_proj", compiler_params=_params(1),
    )(x_rows, wts["ln0_g"], wts["ln0_b"], wts["w_big"], wts["w_small"], wts["b_small"], wts["a_log"],
      wts["w_mconv"], wts["b_mconv"], wts["w_sconv"], wts["b_sconv"], st_qk, st_xbc)


def _sample_recur_body(q_ref, k_ref, v_ref, gate_ref, xs_ref, bm_ref, cm_ref,
                       c0_ref, n0_ref, m0_ref, s0_ref,
                       h_ref, y_ref, c1_ref, n1_ref, m1_ref, s1_ref):
    lane = lax.broadcasted_iota(jnp.int32, (1, LANES), 1)
    gate_pad = jnp.broadcast_to(jnp.where(lane < LANE_LF, -jnp.inf, 0.0), (BF16_ROWS - SEQ_PAD, LANES))

    for i in range(SAMPLE_SEQS_PER_STEP):
        rows = slice(i * SEQ_PAD, (i + 1) * SEQ_PAD)

        def load(ref, dtype):
            return _pad_rows(ref[rows, :], BF16_ROWS).astype(dtype)

        gt = jnp.concatenate([gate_ref[rows, :], gate_pad], axis=0)
        h, y = _chunk_recur(load(q_ref, BF16), load(k_ref, BF16), load(v_ref, BF16), gt,
                            load(xs_ref, F32), load(bm_ref, BF16), load(cm_ref, BF16),
                            (c0_ref.at[i], n0_ref.at[i], m0_ref.at[i], s0_ref.at[i]),
                            (c1_ref.at[i], n1_ref.at[i], m1_ref.at[i], s1_ref.at[i]))
        h_ref[rows, :] = h[:SEQ_PAD]
        y_ref[rows, :] = y[:SEQ_PAD]


def _sample_recur(acts, states, *, nseq):
    q, k, v, gate, xs, bm, cm = acts
    total = q.shape[0]
    nb = SAMPLE_SEQS_PER_STEP

    def rspec(cols):
        return pl.BlockSpec((nb * SEQ_PAD, cols), lambda i: (i, 0))

    def sspec(shape):
        nd = len(shape)
        return pl.BlockSpec((nb,) + shape, lambda i: (i,) + (0,) * nd)

    in_specs = ([rspec(M_QK), rspec(M_QK), rspec(M_V), rspec(LANES), rspec(S_INNER), rspec(S_BC), rspec(S_BC)]
                + [sspec(s) for s in STATE_SHAPES])
    out_specs = [rspec(M_V), rspec(S_INNER)] + [sspec(s) for s in STATE_SHAPES]
    out_shapes = ([jax.ShapeDtypeStruct((total, M_V), F32), jax.ShapeDtypeStruct((total, S_INNER), F32)]
                  + [jax.ShapeDtypeStruct((nseq,) + s, F32) for s in STATE_SHAPES])
    outs = pl.pallas_call(
        _sample_recur_body, grid=(nseq // nb,), in_specs=in_specs, out_specs=out_specs,
        out_shape=out_shapes, name="sample_recur", compiler_params=_params(1),
    )(q, k, v, gate, xs, bm, cm, *states)
    return outs[0], outs[1], outs[2:]


def _post_body(h_ref, y_ref, xs_ref, so_ref, sz_ref, sg_ref, xn_ref,
               mg_ref, sng_ref, dexp_ref, wpa_ref, wpb_ref, wout_ref, g1_ref, b1_ref, x1_ref):
    x1_ref[...] = _post_math(h_ref[...], y_ref[...], xs_ref[...], so_ref[...], sz_ref[...], sg_ref[...],
                             xn_ref[...], mg_ref, sng_ref, dexp_ref, wpa_ref, wpb_ref, wout_ref,
                             g1_ref, b1_ref)


def _post(h, y, xs, so, sz, sg, xn, wts, *, rows):
    total = h.shape[0]

    def rspec(cols):
        return pl.BlockSpec((rows, cols), lambda i: (i, 0))

    in_specs = [rspec(M_V), rspec(S_INNER), rspec(S_INNER), rspec(M_V), rspec(S_INNER),
                rspec(2 * D_MODEL), rspec(D_MODEL),
                _const_spec((1, M_V)), _const_spec((1, S_INNER)), _const_spec((1, S_INNER)),
                _const_spec((M_V // 2, D_MODEL)), _const_spec((S_INNER // 2, D_MODEL)),
                _const_spec((D_MODEL // 2, D_MODEL)), _const_spec((1, D_MODEL)), _const_spec((1, D_MODEL))]
    return pl.pallas_call(
        _post_body, grid=(total // rows,), in_specs=in_specs, out_specs=rspec(D_MODEL),
        out_shape=jax.ShapeDtypeStruct((total, D_MODEL), F32), name="sample_post",
        compiler_params=_params(1),
    )(h, y, xs, so, sz, sg, xn, wts["mnorm_g"], wts["snorm_g"], wts["d_exp"],
      wts["w_proj_a"], wts["w_proj_b"], wts["w_out"], wts["ln1_g"], wts["ln1_b"])


def _ffn_body(mode, rows, n_valid,
              x1_ref, wup_ref, wfc_ref, bfc_ref, wdn_ref, g2_ref, b2_ref, cin_ref,
              y_ref, cout_ref):
    x1 = x1_ref[...]
    x1b = x1.astype(BF16)
    if mode == "cyc":
        valid = (lax.broadcasted_iota(jnp.int32, (rows, 1), 0) % SEQ_PAD) < DEC_SEQ
    else:
        @pl.when(pl.program_id(1) == 0)
        def _():
            cout_ref[...] = cin_ref[...]

    def conv(u, cols):
        if mode == "cyc":
            return _conv_cyc(u, valid, cin_ref, cout_ref, wfc_ref, bfc_ref, F_CONV, rows, cols)
        return _conv_carry(u, cout_ref, wfc_ref, bfc_ref, F_CONV, n_valid, cols)

    def up_proj(j):
        lo = j * FFN_BLOCK
        return (_wdot(x1b, wup_ref, lo, lo + FFN_BLOCK),
                _wdot(x1b, wup_ref, D_FF + lo, D_FF + lo + FFN_BLOCK))

    nblk = D_FF // FFN_BLOCK
    ff = None
    nxt = up_proj(0)
    for j in range(nblk):
        lo = j * FFN_BLOCK
        ua, ub = nxt
        if j + 1 < nblk:
            nxt = up_proj(j + 1)
        va = conv(ua, slice(lo, lo + FFN_BLOCK))
        vb = conv(ub, slice(D_FF + lo, D_FF + lo + FFN_BLOCK))
        act = (_silu(va) * vb).astype(BF16)
        part = _dot(act, pltpu.bitcast(wdn_ref[lo // 2:(lo + FFN_BLOCK) // 2, :], BF16))
        ff = part if ff is None else ff + part
    y_ref[...] = _layer_norm(ALPHA * x1 + ff, g2_ref[...], b2_ref[...])


def _ffn(x1, cin, wts, *, mode, nseq, rows, n_valid):
    total = x1.shape[0]
    nt = total // (nseq * rows)

    def rspec(cols):
        return pl.BlockSpec((rows, cols), lambda b, c: (b * nt + c, 0))

    if mode == "seq":
        cin_spec = pl.BlockSpec((None, SUBLANES, 2 * D_FF), lambda b, c: (0, 0, 0))
        cout_spec = pl.BlockSpec((None, SUBLANES, 2 * D_FF), lambda b, c: (b, 0, 0))
        cout_shape = jax.ShapeDtypeStruct((nseq, SUBLANES, 2 * D_FF), F32)
    else:
        cin_spec = rspec(2 * D_FF)
        cout_spec = rspec(2 * D_FF)
        cout_shape = jax.ShapeDtypeStruct((total, 2 * D_FF), F32)
    in_specs = [rspec(D_MODEL), _const_spec((D_MODEL // 2, 2 * D_FF)), _const_spec((F_CONV, 2 * D_FF)),
                _const_spec((1, 2 * D_FF)), _const_spec((D_FF // 2, D_MODEL)),
                _const_spec((1, D_MODEL)), _const_spec((1, D_MODEL)), cin_spec]
    return pl.pallas_call(
        functools.partial(_ffn_body, mode, rows, n_valid),
        grid=(nseq, nt), in_specs=in_specs, out_specs=[rspec(D_MODEL), cout_spec],
        out_shape=[jax.ShapeDtypeStruct((total, D_MODEL), F32), cout_shape],
        name=f"ffn_{mode}_{nseq}", compiler_params=_params(2),
    )(x1, wts["w_up"], wts["w_fconv"], wts["b_fconv"], wts["w_down"], wts["ln2_g"], wts["ln2_b"], cin)


def _state_rows(state):
    n, w, c = state.shape
    return jnp.pad(state, ((0, 0), (SEQ_PAD - w, 0), (0, 0))).reshape(n * SEQ_PAD, c)


def kernel(x_prompt, x_sample, state_mlstm_conv, state_mlstm_C, state_mlstm_n, state_mlstm_m, state_ssm_conv, state_ssm, state_ffn_conv, meta_tokens, ln0_g, ln0_b, w_in, b_mlstm_if, w_mlstm_conv, b_mlstm_conv, mlstm_norm_g, w_proj_a, w_ssm_conv, b_ssm_conv, ssm_dt_bias, ssm_A_log, ssm_D, ssm_norm_g, w_proj_b, w_out, ln1_g, ln1_b, w_up, w_ffn_conv, b_ffn_conv, w_down, ln2_g, ln2_b):
    batch, seq, _ = x_prompt.shape
    dec_batch, dec_seq, _ = x_sample.shape
    prompt_tile = 2 * CHUNK
    ffn_tile = 2 * CHUNK
    assert dec_seq == DEC_SEQ and seq % ffn_tile == 0 and meta_tokens.shape[0] == N_META
    assert dec_batch % SAMPLE_SEQS_PER_STEP == 0

    w = w_in[0]
    o_i = 2 * M_QK + 2 * M_V
    o_z = o_i + 2 * M_HEADS
    o_xbc = o_z + S_INNER
    o_dt = o_xbc + S_XBC
    o_gate = o_dt + S_HEADS
    w_big = _pack_weight(jnp.concatenate([w[:, :o_i], w[:, o_z:o_dt], w[:, o_gate:]], axis=1), "in")
    w_dt = w[:, o_dt:o_gate]
    w_small = _pack_weight(jnp.concatenate(
        [w[:, o_i:o_z], w_dt, w_dt, jnp.zeros((D_MODEL, LANES - LANE_END), F32)], axis=1), "gates")
    lane_pad = jnp.zeros((LANES - LANE_END,), F32)
    b_small = jnp.concatenate([b_mlstm_if[0], ssm_dt_bias[0], ssm_dt_bias[0], lane_pad])[None]
    a_log = jnp.concatenate([jnp.zeros((LANE_DTA,), F32), ssm_A_log[0],
                             jnp.zeros((LANES - LANE_DT,), F32)])[None]
    wts = {
        "ln0_g": ln0_g[None], "ln0_b": ln0_b[None], "w_big": w_big, "w_small": w_small,
        "b_small": b_small, "a_log": a_log,
        "w_mconv": w_mlstm_conv[0], "b_mconv": b_mlstm_conv, "w_sconv": w_ssm_conv[0], "b_sconv": b_ssm_conv,
        "mnorm_g": mlstm_norm_g, "snorm_g": ssm_norm_g, "d_exp": jnp.repeat(ssm_D[0], S_HEADDIM)[None],
        "w_proj_a": _pack_weight(w_proj_a[0], "proj_a"), "w_proj_b": _pack_weight(w_proj_b[0], "proj_b"),
        "w_out": _pack_weight(w_out[0], "out"), "ln1_g": ln1_g, "ln1_b": ln1_b,
        "w_up": _pack_weight(w_up[0], "up"), "w_fconv": w_ffn_conv[0], "b_fconv": b_ffn_conv,
        "w_down": _pack_weight(w_down[0], "down"), "ln2_g": ln2_g, "ln2_b": ln2_b,
    }

    x_meta = jnp.pad(meta_tokens, ((0, CHUNK - N_META), (0, 0)))
    zero_states = [jnp.zeros((1,) + s, F32) for s in STATE_SHAPES]
    x1_m, tails_m, states_m = _mixer(
        x_meta, jnp.zeros((1, SUBLANES, 2 * M_QK), F32), jnp.zeros((1, SUBLANES, S_XBC), F32),
        zero_states, wts, nseq=1, rows=CHUNK, n_valid=N_META)
    _, tail_ffn_m = _ffn(x1_m, jnp.zeros((1, SUBLANES, 2 * D_FF), F32), wts, mode="seq", nseq=1,
                         rows=CHUNK, n_valid=N_META)

    x1_p, tails_p, states_p = _mixer(
        x_prompt.reshape(batch * seq, D_MODEL), tails_m[0], tails_m[1], states_m, wts,
        nseq=batch, rows=prompt_tile, n_valid=prompt_tile)
    y_p, tail_ffn_p = _ffn(x1_p, tail_ffn_m, wts, mode="seq", nseq=batch, rows=ffn_tile, n_valid=ffn_tile)

    x_s = jnp.pad(x_sample, ((0, 0), (0, SEQ_PAD - dec_seq), (0, 0))).reshape(dec_batch * SEQ_PAD, D_MODEL)
    (xn, q, k, v, gate, xs, bm, cm, so, sz, sg, upre_qk, upre_xbc) = _in_proj(
        x_s, _state_rows(state_mlstm_conv[0]), _state_rows(state_ssm_conv[0]), wts, rows=CHUNK)
    m_in = jnp.pad(state_mlstm_m[0], ((0, 0), (0, LANES - M_HEADS)))[:, None, :]
    h_s, yss_s, states_s = _sample_recur(
        (q, k, v, gate, xs, bm, cm),
        (state_mlstm_C[0], state_mlstm_n[0], m_in, state_ssm[0].reshape(dec_batch, S_INNER, S_STATE)),
        nseq=dec_batch)
    x1_s = _post(h_s, yss_s, xs, so, sz, sg, xn, wts, rows=2 * CHUNK)
    y_s, up_s = _ffn(x1_s, _state_rows(state_ffn_conv[0]), wts, mode="cyc", nseq=1, rows=2 * CHUNK,
                     n_valid=2 * CHUNK)

    def pack_states(convs, states, n):
        c1, n1, m1, s1 = states
        return (convs[0][None], c1[None], n1[None], m1[:, 0, :M_HEADS][None], convs[1][None],
                s1.reshape(n, S_HEADS, S_HEADDIM, S_STATE)[None], convs[2][None])

    def tail_rows(t, width):
        return t[:, SUBLANES - (width - 1):, :]

    def sample_rows(t, width):
        return t.reshape(dec_batch, SEQ_PAD, -1)[:, dec_seq - (width - 1):dec_seq, :]

    p_out = pack_states((tail_rows(tails_p[0], M_CONV), tail_rows(tails_p[1], S_CONV),
                         tail_rows(tail_ffn_p, F_CONV)), states_p, batch)
    s_out = pack_states((sample_rows(upre_qk, M_CONV), sample_rows(upre_xbc, S_CONV),
                         sample_rows(up_s, F_CONV)), states_s, dec_batch)
    y_prompt = y_p.reshape(batch, seq, D_MODEL)
    y_sample = y_s.reshape(dec_batch, SEQ_PAD, D_MODEL)[:, :dec_seq]
    return (y_prompt, y_sample) + p_out + s_out
```

```python
import functools

import jax
import jax.numpy as jnp
from jax import lax
from jax.experimental import pallas as pl
from jax.experimental.pallas import tpu as pltpu

F32 = jnp.float32
BF16 = jnp.bfloat16

D_MODEL = 1024
N_META = 16
M_HEADS = 4
M_DK = 128
M_DV = 256
M_QK = M_HEADS * M_DK
M_V = M_HEADS * M_DV
M_CONV = 4
S_INNER = 2048
S_HEADDIM = 64
S_HEADS = 32
S_GROUPS = 4
S_STATE = 128
S_CONV = 4
S_BC = S_GROUPS * S_STATE
S_XBC = S_INNER + 2 * S_BC
S_GROUP_W = S_INNER // S_GROUPS
S_HEADS_PER_GROUP = S_HEADS // S_GROUPS
D_FF = 2816
F_CONV = 3
ALPHA = 2.0 ** 0.25
LN_EPS = 1e-5
RMS_EPS = 1e-5

LANES = 128
SUBLANES = 8
BF16_ROWS = 16
CHUNK = 128
SEQ_PAD = SUBLANES
DEC_SEQ = 4
SAMPLE_SEQS_PER_STEP = 4
SAMPLE_BLOCK = 32
PACK_COLS = 512
PACK_ROWS = 128
FFN_BLOCK = 256
FFN_AHEAD = 3
FILLER_SLOTS = 2
VMEM_LIMIT_BYTES = 60 * 1024 * 1024

OFF_QK = 0
OFF_V = OFF_QK + 2 * M_QK
OFF_O = OFF_V + M_V
OFF_Z = OFF_O + M_V
OFF_XBC = OFF_Z + S_INNER
OFF_GATE = OFF_XBC + S_XBC
W_BIG_COLS = OFF_GATE + 2 * D_MODEL
LANE_IG = 0
LANE_LF = M_HEADS
LANE_DTA = 2 * M_HEADS
LANE_DT = LANE_DTA + S_HEADS
LANE_END = LANE_DT + S_HEADS

IN_I = 2 * M_QK + 2 * M_V
IN_Z = IN_I + 2 * M_HEADS
IN_DT = IN_Z + S_INNER + S_XBC
IN_GATE = IN_DT + S_HEADS
D_IN = IN_GATE + 2 * D_MODEL

STATE_SHAPES = [(M_HEADS, M_DK, M_DV), (M_HEADS, M_DK), (1, LANES), (S_INNER, S_STATE)]


def _softplus(x):
    return jnp.maximum(x, 0.0) + jnp.log1p(jnp.exp(-jnp.abs(x)))


def _silu(x):
    return x * jax.nn.sigmoid(x)


def _layer_norm(x, g, b):
    mu = jnp.mean(x, axis=-1, keepdims=True)
    xc = x - mu
    var = jnp.mean(xc * xc, axis=-1, keepdims=True)
    return xc * lax.rsqrt(var + LN_EPS) * g + b


def _dot(a, b):
    return jnp.dot(a, b, preferred_element_type=F32)


def _wdot(a, w_ref, lo=None, hi=None):
    w = w_ref[...] if lo is None else w_ref[:, lo:hi]
    return _dot(a, pltpu.bitcast(w, BF16))


def _dot_nt(a, b):
    return lax.dot_general(a, b, (((1,), (1,)), ((), ())), preferred_element_type=F32)


def _dot_tn(a, b):
    return lax.dot_general(a, b, (((0,), (0,)), ((), ())), preferred_element_type=F32)


def _pad_rows(x, rows):
    if x.shape[0] == rows:
        return x
    return jnp.concatenate([x, jnp.zeros((rows - x.shape[0], x.shape[1]), x.dtype)], axis=0)


def _conv_carry(u, carry_ref, w_ref, b_ref, width, n_valid, cols=slice(None)):
    full = jnp.concatenate([carry_ref[:, cols], u], axis=0)
    acc = b_ref[:, cols] + u * w_ref[width - 1:width, cols]
    for k in range(1, width):
        acc = acc + pltpu.roll(full, k, 0)[SUBLANES:] * w_ref[width - 1 - k:width - k, cols]
    carry_ref[:, cols] = u[n_valid - SUBLANES:n_valid]
    return acc


def _time_major(view_ref, width):
    return jnp.concatenate([view_ref[:, t * width:(t + 1) * width] for t in range(DEC_SEQ)], axis=0)


def _store_seq_major(ref, x, fill=None):
    nseq, _, c = ref.shape
    pad = jnp.zeros((1, c), x.dtype) if fill is None else fill
    ref[:, DEC_SEQ:, :] = jnp.broadcast_to(pad[None], (nseq, SEQ_PAD - DEC_SEQ, c))
    for t in range(DEC_SEQ):
        ref[:, t, :] = x[t * nseq:(t + 1) * nseq]


def _conv_tm(u, st_ref, new_ref, w_ref, b_ref, width, ctot, cols=None):
    cols = slice(0, ctot) if cols is None else cols
    nseq = st_ref.shape[0]
    full = ([st_ref[:, j * ctot + cols.start:j * ctot + cols.stop] for j in range(width - 1)]
            + [u[t * nseq:(t + 1) * nseq] for t in range(DEC_SEQ)])
    outs = []
    for t in range(DEC_SEQ):
        acc = b_ref[:, cols]
        for j in range(width):
            acc = acc + full[t + j] * w_ref[j:j + 1, cols]
        outs.append(acc)
    for j in range(width - 1):
        new_ref[:, j * ctot + cols.start:j * ctot + cols.stop] = full[DEC_SEQ + j]
    return jnp.concatenate(outs, axis=0)


def _gate_tile(xb, valid, wsm_ref, bsm_ref, alog_ref):
    g = _wdot(xb, wsm_ref) + bsm_ref[...]
    lane = lax.broadcasted_iota(jnp.int32, (1, LANES), 1)
    sp = _softplus(g)
    lsg = -_softplus(-g)
    a_row = jnp.where((lane >= LANE_DTA) & (lane < LANE_DT), -jnp.exp(alog_ref[...]), 0.0)
    tile = jnp.where(lane < LANE_LF, g,
                     jnp.where(lane < LANE_DTA, lsg,
                               jnp.where(lane < LANE_DT, sp * a_row,
                                         jnp.where(lane < LANE_END, sp, 0.0))))
    if valid is None:
        return tile
    return jnp.where(valid, tile, jnp.where(lane < LANE_LF, -jnp.inf, 0.0))


def _cumsum_rows(x):
    row = lax.broadcasted_iota(jnp.int32, x.shape, 0)
    k = 1
    while k < x.shape[0]:
        x = x + jnp.where(row >= k, pltpu.roll(x, k, 0), 0.0)
        k *= 2
    return x


def _chunk_recur(q, k, v, gt, xs, bm, cm, st_in, st_out, fillers=None):
    staged = fillers is not None
    fillers = list(fillers or ())

    def fill():
        if fillers:
            fillers.pop(0)()

    def run(vector_stage, matmul_stage, items):
        if staged:
            ctxs = [vector_stage(i) for i in items]
            return [matmul_stage(c) for c in ctxs]
        return [matmul_stage(vector_stage(i)) for i in items]

    c_ref, n_ref, m_ref, s_ref = st_in
    c_out, n_out, m_out, s_out = st_out
    tr = q.shape[0]
    lane = lax.broadcasted_iota(jnp.int32, (1, LANES), 1)
    cum_lanes = (lane >= LANE_LF) & (lane < LANE_DT)
    gc = jnp.where(cum_lanes, _cumsum_rows(jnp.where(cum_lanes, gt, 0.0)), gt)
    if tr < CHUNK:
        tail = jnp.where(lane < LANE_LF, -jnp.inf, jnp.where(cum_lanes, gc[tr - 1:tr, :], 0.0))
        gc_full = jnp.concatenate([gc, jnp.broadcast_to(tail, (CHUNK - tr, LANES))], axis=0)
    else:
        gc_full = gc
    gct = gc_full.T
    k_s = _pad_rows(k, CHUNK)
    bm_s = _pad_rows(bm, CHUNK)

    def head_dots(h):
        qh = q[:, h * M_DK:(h + 1) * M_DK]
        c0 = c_ref[h]
        return _dot_nt(qh, k_s[:, h * M_DK:(h + 1) * M_DK]), _dot(qh, c0.astype(BF16)), c0

    def group_dots(g):
        cg = cm[:, g * S_STATE:(g + 1) * S_STATE]
        s_old = s_ref[g * S_GROUP_W:(g + 1) * S_GROUP_W, :]
        return (_dot_nt(cg, bm_s[:, g * S_STATE:(g + 1) * S_STATE]),
                _dot_nt(cg, s_old.astype(BF16)), s_old)

    if staged:
        head_pre = [head_dots(h) for h in range(M_HEADS)]
        group_pre = [group_dots(g) for g in range(S_GROUPS)]
        head_dots = head_pre.__getitem__
        group_dots = group_pre.__getitem__
    fill()

    last = gc[tr - 1:tr, :]
    ti = lax.broadcasted_iota(jnp.int32, (tr, CHUNK), 0)
    si = lax.broadcasted_iota(jnp.int32, (tr, CHUNK), 1)
    causal = si <= ti
    v_s = _pad_rows(v, CHUNK)
    m_row = m_ref[...]

    def head_vector(h):
        qh = q[:, h * M_DK:(h + 1) * M_DK]
        bt_c = gc[:, LANE_LF + h:LANE_LF + h + 1]
        ig_c = gc[:, LANE_IG + h:LANE_IG + h + 1]
        bt_r = gct[LANE_LF + h:LANE_LF + h + 1, :]
        ig_r = gct[LANE_IG + h:LANE_IG + h + 1, :]
        m0 = m_row[:, h:h + 1]
        sqk, qc, c0 = head_dots(h)
        n0 = n_ref[h:h + 1, :]

        dmat = jnp.where(causal, bt_c - bt_r + ig_r, -jnp.inf)
        inter = bt_c + m0
        m_t = jnp.maximum(inter, jnp.max(dmat, axis=1, keepdims=True))
        w_intra = jnp.exp(dmat - m_t)
        w_inter = jnp.exp(inter - m_t)
        s = sqk * w_intra
        den = (jnp.sum(s, axis=1, keepdims=True)
               + jnp.sum(qh.astype(F32) * n0, axis=1, keepdims=True) * w_inter)
        rdenom = 1.0 / jnp.maximum(jnp.abs(den), jnp.exp(-m_t))

        b_last = last[:, LANE_LF + h:LANE_LF + h + 1]
        m_new = jnp.maximum(b_last + m0, jnp.max(b_last - bt_r + ig_r, axis=1, keepdims=True))
        w_last = jnp.exp(b_last - bt_c + ig_c - m_new)
        decay = jnp.exp(b_last + m0 - m_new)
        kw = k[:, h * M_DK:(h + 1) * M_DK].astype(F32) * w_last
        n_out[h:h + 1, :] = decay * n0 + jnp.sum(kw, axis=0, keepdims=True)
        return h, s.astype(BF16), qc * w_inter, rdenom, kw.astype(BF16), decay * c0, m_new

    def head_matmul(ctx):
        h, s_b, inter_part, rdenom, kw_b, c_decayed, m_new = ctx
        num = _dot(s_b, v_s[:, h * M_DV:(h + 1) * M_DV]) + inter_part
        c_out[h] = c_decayed + _dot_tn(kw_b, v[:, h * M_DV:(h + 1) * M_DV])
        return num * rdenom, m_new

    heads = run(head_vector, head_matmul, range(M_HEADS))
    m_new_row = m_row
    for h, (_, m_new) in enumerate(heads):
        m_new_row = jnp.where(lane == h, m_new, m_new_row)
    m_out[...] = m_new_row
    fill()

    lo_t = lax.broadcasted_iota(jnp.int32, (tr, LANES), 1) < S_HEADDIM
    lo_s = lax.broadcasted_iota(jnp.int32, (CHUNK, LANES), 1) < S_HEADDIM
    xs_b = xs.astype(BF16)

    def group_vector(g):
        rows_g = slice(g * S_GROUP_W, (g + 1) * S_GROUP_W)
        cb, y_inter, s_old = group_dots(g)
        pairs, wd_parts, dec_parts = [], [], []
        for p in range(S_HEADS_PER_GROUP // 2):
            ms, es, wds = [], [], []
            for r in (2 * p, 2 * p + 1):
                hh = g * S_HEADS_PER_GROUP + r
                bt_c = gc[:, LANE_DTA + hh:LANE_DTA + hh + 1]
                dt_c = gc[:, LANE_DT + hh:LANE_DT + hh + 1]
                bt_r = gct[LANE_DTA + hh:LANE_DTA + hh + 1, :]
                dt_r = gct[LANE_DT + hh:LANE_DT + hh + 1, :]
                b_last = last[:, LANE_DTA + hh:LANE_DTA + hh + 1]
                dec = jnp.exp(jnp.where(causal, bt_c - bt_r, -jnp.inf))
                ms.append((cb * dec * dt_r).astype(BF16))
                es.append(jnp.exp(bt_c))
                wds.append(jnp.exp(b_last - bt_c) * dt_c)
                dec_parts.append(jnp.broadcast_to(jnp.exp(b_last), (S_HEADDIM, 1)))
            col = g * S_GROUP_W + p * LANES
            xp = _pad_rows(xs_b[:, col:col + LANES], CHUNK)
            zero = jnp.zeros_like(xp)
            rhs = jnp.concatenate([jnp.where(lo_s, xp, zero), jnp.where(lo_s, zero, xp)], axis=0)
            pairs.append((jnp.concatenate(ms, axis=1), rhs,
                          y_inter[:, p * LANES:(p + 1) * LANES] * jnp.where(lo_t, es[0], es[1])))
            wd_parts.append(jnp.where(lo_t, wds[0], wds[1]))
        a_mat = (xs[:, rows_g] * jnp.concatenate(wd_parts, axis=1)).astype(BF16)
        return g, pairs, a_mat, jnp.concatenate(dec_parts, axis=0) * s_old

    def group_matmul(ctx):
        g, pairs, a_mat, s_decayed = ctx
        ys = [_dot(lhs, rhs) + inter_part for lhs, rhs, inter_part in pairs]
        s_out[g * S_GROUP_W:(g + 1) * S_GROUP_W, :] = (
            s_decayed + _dot_tn(a_mat, bm[:, g * S_STATE:(g + 1) * S_STATE]))
        return jnp.concatenate(ys, axis=1)

    groups = run(group_vector, group_matmul, range(S_GROUPS))
    return jnp.concatenate([hp for hp, _ in heads], axis=1), jnp.concatenate(groups, axis=1)


def _post_math(h, y, xs, so, sz, sg, xn, mg_ref, sng_ref, dexp_ref, wpa_ref, wpb_ref, wout_ref,
               g1_ref, b1_ref):
    hs = []
    for i in range(M_HEADS):
        hh = h[:, i * M_DV:(i + 1) * M_DV]
        mu = jnp.mean(hh, axis=-1, keepdims=True)
        hc = hh - mu
        var = jnp.mean(hc * hc, axis=-1, keepdims=True)
        hs.append(hc * lax.rsqrt(var + LN_EPS))
    hn = jnp.concatenate(hs, axis=1) * mg_ref[...]
    ya = _wdot((so * hn).astype(BF16), wpa_ref)

    y = (y + dexp_ref[...] * xs) * sz
    ys = []
    for g in range(S_GROUPS):
        yg = y[:, g * S_GROUP_W:(g + 1) * S_GROUP_W]
        ys.append(yg * lax.rsqrt(jnp.mean(yg * yg, axis=-1, keepdims=True) + RMS_EPS))
    yn = jnp.concatenate(ys, axis=1) * sng_ref[...]
    yb = _wdot(yn.astype(BF16), wpb_ref)

    mixed = _wdot((sg[:, :D_MODEL] * ya + sg[:, D_MODEL:] * yb).astype(BF16), wout_ref)
    return _layer_norm(ALPHA * xn + mixed, g1_ref[...], b1_ref[...])


def _const_spec(shape):
    nd = len(shape)
    return pl.BlockSpec(shape, lambda *_: (0,) * nd, pipeline_mode=pl.Buffered(1))


def _pack_body(w_ref, o_ref):
    o_ref[...] = pltpu.bitcast(w_ref[...].astype(BF16), jnp.uint32)


def _pack_weight(w, name):
    k, n = w.shape
    bn = min(n, PACK_COLS)
    return pl.pallas_call(
        _pack_body, grid=(n // bn,), in_specs=[pl.BlockSpec((k, bn), lambda j: (0, j))],
        out_specs=pl.BlockSpec((k // 2, bn), lambda j: (0, j)),
        out_shape=jax.ShapeDtypeStruct((k // 2, n), jnp.uint32), name=f"pack_{name}",
        compiler_params=_params(1),
    )(w)


def _pack_in_body(w_ref, big_ref, small_ref):
    big = jnp.concatenate([w_ref[:, :IN_I], w_ref[:, IN_Z:IN_DT], w_ref[:, IN_GATE:]], axis=1)
    big_ref[...] = pltpu.bitcast(big.astype(BF16), jnp.uint32)
    lane = lax.broadcasted_iota(jnp.int32, (1, LANES), 1)
    w_if = w_ref[:, IN_I:IN_I + LANES]
    w_dt = w_ref[:, IN_DT - LANE_DTA:IN_DT - LANE_DTA + LANES]
    small = jnp.where(lane < LANE_DTA, w_if,
                      jnp.where(lane < LANE_DT, w_dt,
                                jnp.where(lane < LANE_END, pltpu.roll(w_dt, S_HEADS, 1), 0.0)))
    small_ref[...] = pltpu.bitcast(small.astype(BF16), jnp.uint32)


def _pack_in(w):
    k = w.shape[0]
    assert w.shape[1] == D_IN and IN_I % LANES == 0 and (IN_DT - LANE_DTA) % LANES == 0
    return pl.pallas_call(
        _pack_in_body, grid=(k // PACK_ROWS,),
        in_specs=[pl.BlockSpec((PACK_ROWS, D_IN), lambda i: (i, 0))],
        out_specs=[pl.BlockSpec((PACK_ROWS // 2, W_BIG_COLS), lambda i: (i, 0)),
                   pl.BlockSpec((PACK_ROWS // 2, LANES), lambda i: (i, 0))],
        out_shape=[jax.ShapeDtypeStruct((k // 2, W_BIG_COLS), jnp.uint32),
                   jax.ShapeDtypeStruct((k // 2, LANES), jnp.uint32)],
        name="pack_in", compiler_params=_params(1),
    )(w)


def _params(dims):
    return pltpu.CompilerParams(dimension_semantics=("arbitrary",) * dims,
                                vmem_limit_bytes=VMEM_LIMIT_BYTES)


def _mixer_body(rows, n_valid,
                x_ref, g0_ref, b0_ref, wbig_ref, wsm_ref, bsm_ref, alog_ref,
                wmc_ref, bmc_ref, wsc_ref, bsc_ref, cin_qk_ref, cin_xbc_ref,
                c0_ref, n0_ref, m0_ref, s0_ref,
                mg_ref, sng_ref, dexp_ref, wpa_ref, wpb_ref, wout_ref, g1_ref, b1_ref,
                x1_ref, cout_qk_ref, cout_xbc_ref, c1_ref, n1_ref, m1_ref, s1_ref):
    @pl.when(pl.program_id(1) == 0)
    def _():
        cout_qk_ref[...] = cin_qk_ref[...]
        cout_xbc_ref[...] = cin_xbc_ref[...]
        c1_ref[...] = c0_ref[...]
        n1_ref[...] = n0_ref[...]
        m1_ref[...] = m0_ref[...]
        s1_ref[...] = s0_ref[...]

    xn = _layer_norm(x_ref[...], g0_ref[...], b0_ref[...])
    xb = xn.astype(BF16)
    valid = None
    if n_valid < rows:
        valid = lax.broadcasted_iota(jnp.int32, (rows, 1), 0) < n_valid
    gt = _gate_tile(xb, valid, wsm_ref, bsm_ref, alog_ref)

    u_qk = _wdot(xb, wbig_ref, OFF_QK, OFF_QK + 2 * M_QK)
    qk = _silu(_conv_carry(u_qk, cout_qk_ref, wmc_ref, bmc_ref, M_CONV, n_valid))
    q = (qk[:, :M_QK] * (M_DK ** -0.5)).astype(BF16)
    k = qk[:, M_QK:].astype(BF16)
    v = _wdot(xb, wbig_ref, OFF_V, OFF_V + M_V).astype(BF16)
    u_xbc = _wdot(xb, wbig_ref, OFF_XBC, OFF_XBC + S_XBC)
    xbc = _silu(_conv_carry(u_xbc, cout_xbc_ref, wsc_ref, bsc_ref, S_CONV, n_valid))
    xs = xbc[:, :S_INNER]
    bm = xbc[:, S_INNER:S_INNER + S_BC].astype(BF16)
    cm = xbc[:, S_INNER + S_BC:].astype(BF16)

    offs = [OFF_O, OFF_Z, OFF_Z + D_MODEL, OFF_GATE, OFF_GATE + D_MODEL]
    dense = {}
    nchunk = rows // CHUNK
    nslot = nchunk * FILLER_SLOTS
    hs, ys = [], []
    for ci in range(nchunk):
        sl = slice(ci * CHUNK, (ci + 1) * CHUNK)
        state = (c1_ref, n1_ref, m1_ref, s1_ref)

        def filler(slot):
            def issue():
                for off in offs[slot::nslot]:
                    dense[off] = _wdot(xb, wbig_ref, off, off + D_MODEL)
            return issue

        h, y = _chunk_recur(q[sl], k[sl], v[sl], gt[sl], xs[sl], bm[sl], cm[sl], state, state,
                            [filler(ci * FILLER_SLOTS + i) for i in range(FILLER_SLOTS)])
        hs.append(h)
        ys.append(y)
    h = jnp.concatenate(hs, axis=0)
    y = jnp.concatenate(ys, axis=0)

    so = jax.nn.sigmoid(dense[OFF_O])
    sz = _silu(jnp.concatenate([dense[OFF_Z], dense[OFF_Z + D_MODEL]], axis=1))
    sg = jax.nn.sigmoid(jnp.concatenate([dense[OFF_GATE], dense[OFF_GATE + D_MODEL]], axis=1))
    x1_ref[...] = _post_math(h, y, xs, so, sz, sg, xn, mg_ref, sng_ref, dexp_ref,
                             wpa_ref, wpb_ref, wout_ref, g1_ref, b1_ref)


def _mixer(x_rows, cin_qk, cin_xbc, states, wts, *, nseq, rows, n_valid):
    total = x_rows.shape[0]
    nt = total // (nseq * rows)
    rspec = pl.BlockSpec((rows, D_MODEL), lambda b, c: (b * nt + c, 0))

    def per_seq(shape):
        nd = len(shape)
        return pl.BlockSpec((None,) + shape, lambda b, c: (b,) + (0,) * nd)

    def shared(shape):
        nd = len(shape)
        return pl.BlockSpec((None,) + shape, lambda b, c: (0,) * (nd + 1), pipeline_mode=pl.Buffered(1))

    tails = [(SUBLANES, 2 * M_QK), (SUBLANES, S_XBC)]
    in_specs = ([rspec, _const_spec((1, D_MODEL)), _const_spec((1, D_MODEL)),
                 _const_spec((D_MODEL // 2, W_BIG_COLS)), _const_spec((D_MODEL // 2, LANES)),
                 _const_spec((1, LANES)), _const_spec((1, LANES)),
                 _const_spec((M_CONV, 2 * M_QK)), _const_spec((1, 2 * M_QK)),
                 _const_spec((S_CONV, S_XBC)), _const_spec((1, S_XBC))]
                + [shared(s) for s in tails + STATE_SHAPES]
                + [_const_spec((1, M_V)), _const_spec((1, S_INNER)), _const_spec((1, S_INNER)),
                   _const_spec((M_V // 2, D_MODEL)), _const_spec((S_INNER // 2, D_MODEL)),
                   _const_spec((D_MODEL // 2, D_MODEL)), _const_spec((1, D_MODEL)), _const_spec((1, D_MODEL))])
    out_specs = [rspec] + [per_seq(s) for s in tails + STATE_SHAPES]
    out_shapes = ([jax.ShapeDtypeStruct((total, D_MODEL), F32)]
                  + [jax.ShapeDtypeStruct((nseq,) + s, F32) for s in tails + STATE_SHAPES])
    outs = pl.pallas_call(
        functools.partial(_mixer_body, rows, n_valid),
        grid=(nseq, nt), in_specs=in_specs, out_specs=out_specs, out_shape=out_shapes,
        name=f"mixer_{nseq}", compiler_params=_params(2),
    )(x_rows, wts["ln0_g"], wts["ln0_b"], wts["w_big"], wts["w_small"], wts["b_small"], wts["a_log"],
      wts["w_mconv"], wts["b_mconv"], wts["w_sconv"], wts["b_sconv"], cin_qk, cin_xbc, *states,
      wts["mnorm_g"], wts["snorm_g"], wts["d_exp"], wts["w_proj_a"], wts["w_proj_b"], wts["w_out"],
      wts["ln1_g"], wts["ln1_b"])
    return outs[0], outs[1:3], outs[3:]


def _in_proj_body(x_ref, g0_ref, b0_ref, wbig_ref, wsm_ref, bsm_ref, alog_ref,
                  wmc_ref, bmc_ref, wsc_ref, bsc_ref, st_qk_ref, st_xbc_ref,
                  q_ref, k_ref, v_ref, gate_ref, xs_ref, bm_ref, cm_ref,
                  xn_ref, xs_tm_ref, so_ref, sz_ref, sg_ref, new_qk_ref, new_xbc_ref):
    xn = _layer_norm(_time_major(x_ref, D_MODEL), g0_ref[...], b0_ref[...])
    xn_ref[...] = xn
    xb = xn.astype(BF16)
    lane = lax.broadcasted_iota(jnp.int32, (1, LANES), 1)
    _store_seq_major(gate_ref, _gate_tile(xb, None, wsm_ref, bsm_ref, alog_ref),
                     jnp.where(lane < LANE_LF, -jnp.inf, 0.0))

    u_qk = _wdot(xb, wbig_ref, OFF_QK, OFF_QK + 2 * M_QK)
    qk = _silu(_conv_tm(u_qk, st_qk_ref, new_qk_ref, wmc_ref, bmc_ref, M_CONV, 2 * M_QK))
    _store_seq_major(q_ref, qk[:, :M_QK] * (M_DK ** -0.5))
    _store_seq_major(k_ref, qk[:, M_QK:])
    _store_seq_major(v_ref, _wdot(xb, wbig_ref, OFF_V, OFF_V + M_V))
    so_ref[...] = jax.nn.sigmoid(_wdot(xb, wbig_ref, OFF_O, OFF_O + M_V))
    sz_ref[...] = _silu(_wdot(xb, wbig_ref, OFF_Z, OFF_Z + S_INNER))
    u_xbc = _wdot(xb, wbig_ref, OFF_XBC, OFF_XBC + S_XBC)
    xbc = _silu(_conv_tm(u_xbc, st_xbc_ref, new_xbc_ref, wsc_ref, bsc_ref, S_CONV, S_XBC))
    xs_tm_ref[...] = xbc[:, :S_INNER]
    _store_seq_major(xs_ref, xbc[:, :S_INNER])
    _store_seq_major(bm_ref, xbc[:, S_INNER:S_INNER + S_BC])
    _store_seq_major(cm_ref, xbc[:, S_INNER + S_BC:])
    sg_ref[...] = jax.nn.sigmoid(_wdot(xb, wbig_ref, OFF_GATE, OFF_GATE + 2 * D_MODEL))


def _in_proj(x_view, st_qk, st_xbc, wts):
    nseq = x_view.shape[0]
    nb = SAMPLE_BLOCK

    def vspec(cols):
        return pl.BlockSpec((nb, cols), lambda i: (i, 0))

    def sspec(cols):
        return pl.BlockSpec((nb, SEQ_PAD, cols), lambda i: (i, 0, 0))

    def tspec(cols):
        return pl.BlockSpec((nb * DEC_SEQ, cols), lambda i: (i, 0))

    in_specs = [vspec(DEC_SEQ * D_MODEL), _const_spec((1, D_MODEL)), _const_spec((1, D_MODEL)),
                _const_spec((D_MODEL // 2, W_BIG_COLS)), _const_spec((D_MODEL // 2, LANES)),
                _const_spec((1, LANES)), _const_spec((1, LANES)),
                _const_spec((M_CONV, 2 * M_QK)), _const_spec((1, 2 * M_QK)),
                _const_spec((S_CONV, S_XBC)), _const_spec((1, S_XBC)),
                vspec((M_CONV - 1) * 2 * M_QK), vspec((S_CONV - 1) * S_XBC)]
    seq_cols = [M_QK, M_QK, M_V, LANES, S_INNER, S_BC, S_BC]
    tm_cols = [D_MODEL, S_INNER, M_V, S_INNER, 2 * D_MODEL]
    new_cols = [(M_CONV - 1) * 2 * M_QK, (S_CONV - 1) * S_XBC]
    outs = pl.pallas_call(
        _in_proj_body, grid=(nseq // nb,), in_specs=in_specs,
        out_specs=[sspec(c) for c in seq_cols] + [tspec(c) for c in tm_cols] + [vspec(c) for c in new_cols],
        out_shape=([jax.ShapeDtypeStruct((nseq, SEQ_PAD, c), F32) for c in seq_cols]
                   + [jax.ShapeDtypeStruct((nseq * DEC_SEQ, c), F32) for c in tm_cols]
                   + [jax.ShapeDtypeStruct((nseq, c), F32) for c in new_cols]),
        name="sample_in_proj", compiler_params=_params(1),
    )(x_view, wts["ln0_g"], wts["ln0_b"], wts["w_big"], wts["w_small"], wts["b_small"], wts["a_log"],
      wts["w_mconv"], wts["b_mconv"], wts["w_sconv"], wts["b_sconv"], st_qk, st_xbc)
    return outs[:7], outs[7:12], outs[12:]


def _sample_recur_body(q_ref, k_ref, v_ref, gate_ref, xs_ref, bm_ref, cm_ref,
                       c0_ref, n0_ref, m0_ref, s0_ref,
                       h_ref, y_ref, c1_ref, n1_ref, m1_ref, s1_ref):
    lane = lax.broadcasted_iota(jnp.int32, (1, LANES), 1)
    gate_pad = jnp.broadcast_to(jnp.where(lane < LANE_LF, -jnp.inf, 0.0), (BF16_ROWS - SEQ_PAD, LANES))

    for i in range(SAMPLE_SEQS_PER_STEP):
        rows = slice(i * SEQ_PAD, (i + 1) * SEQ_PAD)

        def load(ref, dtype):
            return _pad_rows(ref[rows, :], BF16_ROWS).astype(dtype)

        gt = jnp.concatenate([gate_ref[rows, :], gate_pad], axis=0)
        h, y = _chunk_recur(load(q_ref, BF16), load(k_ref, BF16), load(v_ref, BF16), gt,
                            load(xs_ref, F32), load(bm_ref, BF16), load(cm_ref, BF16),
                            (c0_ref.at[i], n0_ref.at[i], m0_ref.at[i], s0_ref.at[i]),
                            (c1_ref.at[i], n1_ref.at[i], m1_ref.at[i], s1_ref.at[i]))
        h_ref[rows, :] = h[:SEQ_PAD]
        y_ref[rows, :] = y[:SEQ_PAD]


def _sample_recur(acts, states, *, nseq):
    q, k, v, gate, xs, bm, cm = acts
    total = q.shape[0]
    nb = SAMPLE_SEQS_PER_STEP

    def rspec(cols):
        return pl.BlockSpec((nb * SEQ_PAD, cols), lambda i: (i, 0))

    def sspec(shape):
        nd = len(shape)
        return pl.BlockSpec((nb,) + shape, lambda i: (i,) + (0,) * nd)

    in_specs = ([rspec(M_QK), rspec(M_QK), rspec(M_V), rspec(LANES), rspec(S_INNER), rspec(S_BC), rspec(S_BC)]
                + [sspec(s) for s in STATE_SHAPES])
    out_specs = [rspec(M_V), rspec(S_INNER)] + [sspec(s) for s in STATE_SHAPES]
    out_shapes = ([jax.ShapeDtypeStruct((total, M_V), F32), jax.ShapeDtypeStruct((total, S_INNER), F32)]
                  + [jax.ShapeDtypeStruct((nseq,) + s, F32) for s in STATE_SHAPES])
    outs = pl.pallas_call(
        _sample_recur_body, grid=(nseq // nb,), in_specs=in_specs, out_specs=out_specs,
        out_shape=out_shapes, name="sample_recur", compiler_params=_params(1),
    )(q, k, v, gate, xs, bm, cm, *states)
    return outs[0], outs[1], outs[2:]


def _post_body(h_ref, y_ref, xs_ref, so_ref, sz_ref, sg_ref, xn_ref,
               mg_ref, sng_ref, dexp_ref, wpa_ref, wpb_ref, wout_ref, g1_ref, b1_ref, x1_ref):
    h = jnp.concatenate([h_ref[:, t, :] for t in range(DEC_SEQ)], axis=0)
    y = jnp.concatenate([y_ref[:, t, :] for t in range(DEC_SEQ)], axis=0)
    x1_ref[...] = _post_math(h, y, xs_ref[...], so_ref[...], sz_ref[...], sg_ref[...],
                             xn_ref[...], mg_ref, sng_ref, dexp_ref, wpa_ref, wpb_ref, wout_ref,
                             g1_ref, b1_ref)


def _post(h, y, xs, so, sz, sg, xn, wts):
    nseq = h.shape[0]
    nb = SAMPLE_BLOCK

    def sspec(cols):
        return pl.BlockSpec((nb, SEQ_PAD, cols), lambda i: (i, 0, 0))

    def tspec(cols):
        return pl.BlockSpec((nb * DEC_SEQ, cols), lambda i: (i, 0))

    in_specs = [sspec(M_V), sspec(S_INNER), tspec(S_INNER), tspec(M_V), tspec(S_INNER),
                tspec(2 * D_MODEL), tspec(D_MODEL),
                _const_spec((1, M_V)), _const_spec((1, S_INNER)), _const_spec((1, S_INNER)),
                _const_spec((M_V // 2, D_MODEL)), _const_spec((S_INNER // 2, D_MODEL)),
                _const_spec((D_MODEL // 2, D_MODEL)), _const_spec((1, D_MODEL)), _const_spec((1, D_MODEL))]
    return pl.pallas_call(
        _post_body, grid=(nseq // nb,), in_specs=in_specs, out_specs=tspec(D_MODEL),
        out_shape=jax.ShapeDtypeStruct((nseq * DEC_SEQ, D_MODEL), F32), name="sample_post",
        compiler_params=_params(1),
    )(h, y, xs, so, sz, sg, xn, wts["mnorm_g"], wts["snorm_g"], wts["d_exp"],
      wts["w_proj_a"], wts["w_proj_b"], wts["w_out"], wts["ln1_g"], wts["ln1_b"])


def _ffn_body(mode, rows, n_valid,
              x1_ref, wup_ref, wfc_ref, bfc_ref, wdn_ref, g2_ref, b2_ref, cin_ref,
              y_ref, cout_ref):
    x1 = x1_ref[...]
    x1b = x1.astype(BF16)
    if mode == "seq":
        @pl.when(pl.program_id(1) == 0)
        def _():
            cout_ref[...] = cin_ref[...]

    def conv(u, cols):
        if mode == "tm":
            return _conv_tm(u, cin_ref, cout_ref, wfc_ref, bfc_ref, F_CONV, 2 * D_FF, cols)
        return _conv_carry(u, cout_ref, wfc_ref, bfc_ref, F_CONV, n_valid, cols)

    def up_proj(j):
        lo = j * FFN_BLOCK
        return (_wdot(x1b, wup_ref, lo, lo + FFN_BLOCK),
                _wdot(x1b, wup_ref, D_FF + lo, D_FF + lo + FFN_BLOCK))

    nblk = D_FF // FFN_BLOCK
    ff = None
    ups = [up_proj(j) for j in range(min(FFN_AHEAD, nblk))]
    for j in range(nblk):
        lo = j * FFN_BLOCK
        ua, ub = ups[j]
        if j + FFN_AHEAD < nblk:
            ups.append(up_proj(j + FFN_AHEAD))
        va = conv(ua, slice(lo, lo + FFN_BLOCK))
        vb = conv(ub, slice(D_FF + lo, D_FF + lo + FFN_BLOCK))
        act = (_silu(va) * vb).astype(BF16)
        part = _dot(act, pltpu.bitcast(wdn_ref[lo // 2:(lo + FFN_BLOCK) // 2, :], BF16))
        ff = part if ff is None else ff + part
    y = _layer_norm(ALPHA * x1 + ff, g2_ref[...], b2_ref[...])
    if mode == "tm":
        nseq = y_ref.shape[0]
        for t in range(DEC_SEQ):
            y_ref[:, t * D_MODEL:(t + 1) * D_MODEL] = y[t * nseq:(t + 1) * nseq]
    else:
        y_ref[...] = y


def _ffn(x1, cin, wts, *, mode, nseq, rows, n_valid):
    total = x1.shape[0]
    nt = total // (nseq * rows)

    def rspec(cols):
        return pl.BlockSpec((rows, cols), lambda b, c: (b * nt + c, 0))

    if mode == "seq":
        cin_spec = pl.BlockSpec((None, SUBLANES, 2 * D_FF), lambda b, c: (0, 0, 0))
        cout_spec = pl.BlockSpec((None, SUBLANES, 2 * D_FF), lambda b, c: (b, 0, 0))
        cout_shape = jax.ShapeDtypeStruct((nseq, SUBLANES, 2 * D_FF), F32)
        y_spec = rspec(D_MODEL)
        y_shape = jax.ShapeDtypeStruct((total, D_MODEL), F32)
    else:
        vrows = rows // DEC_SEQ
        vspec = lambda cols: pl.BlockSpec((vrows, cols), lambda b, c: (b * nt + c, 0))
        cin_spec = vspec((F_CONV - 1) * 2 * D_FF)
        cout_spec = vspec((F_CONV - 1) * 2 * D_FF)
        cout_shape = jax.ShapeDtypeStruct((total // DEC_SEQ, (F_CONV - 1) * 2 * D_FF), F32)
        y_spec = vspec(DEC_SEQ * D_MODEL)
        y_shape = jax.ShapeDtypeStruct((total // DEC_SEQ, DEC_SEQ * D_MODEL), F32)
    in_specs = [rspec(D_MODEL), _const_spec((D_MODEL // 2, 2 * D_FF)), _const_spec((F_CONV, 2 * D_FF)),
                _const_spec((1, 2 * D_FF)), _const_spec((D_FF // 2, D_MODEL)),
                _const_spec((1, D_MODEL)), _const_spec((1, D_MODEL)), cin_spec]
    return pl.pallas_call(
        functools.partial(_ffn_body, mode, rows, n_valid),
        grid=(nseq, nt), in_specs=in_specs, out_specs=[y_spec, cout_spec],
        out_shape=[y_shape, cout_shape],
        name=f"ffn_{mode}_{nseq}", compiler_params=_params(2),
    )(x1, wts["w_up"], wts["w_fconv"], wts["b_fconv"], wts["w_down"], wts["ln2_g"], wts["ln2_b"], cin)


def kernel(x_prompt, x_sample, state_mlstm_conv, state_mlstm_C, state_mlstm_n, state_mlstm_m, state_ssm_conv, state_ssm, state_ffn_conv, meta_tokens, ln0_g, ln0_b, w_in, b_mlstm_if, w_mlstm_conv, b_mlstm_conv, mlstm_norm_g, w_proj_a, w_ssm_conv, b_ssm_conv, ssm_dt_bias, ssm_A_log, ssm_D, ssm_norm_g, w_proj_b, w_out, ln1_g, ln1_b, w_up, w_ffn_conv, b_ffn_conv, w_down, ln2_g, ln2_b):
    batch, seq, _ = x_prompt.shape
    dec_batch, dec_seq, _ = x_sample.shape
    prompt_tile = 2 * CHUNK
    ffn_tile = 2 * CHUNK
    assert dec_seq == DEC_SEQ and seq % ffn_tile == 0 and meta_tokens.shape[0] == N_META
    assert dec_batch % SAMPLE_SEQS_PER_STEP == 0 and dec_batch % SAMPLE_BLOCK == 0

    w_big, w_small = _pack_in(w_in[0])
    lane_pad = jnp.zeros((LANES - LANE_END,), F32)
    b_small = jnp.concatenate([b_mlstm_if[0], ssm_dt_bias[0], ssm_dt_bias[0], lane_pad])[None]
    a_log = jnp.concatenate([jnp.zeros((LANE_DTA,), F32), ssm_A_log[0],
                             jnp.zeros((LANES - LANE_DT,), F32)])[None]
    wts = {
        "ln0_g": ln0_g[None], "ln0_b": ln0_b[None], "w_big": w_big, "w_small": w_small,
        "b_small": b_small, "a_log": a_log,
        "w_mconv": w_mlstm_conv[0], "b_mconv": b_mlstm_conv, "w_sconv": w_ssm_conv[0], "b_sconv": b_ssm_conv,
        "mnorm_g": mlstm_norm_g, "snorm_g": ssm_norm_g, "d_exp": jnp.repeat(ssm_D[0], S_HEADDIM)[None],
        "w_proj_a": _pack_weight(w_proj_a[0], "proj_a"), "w_proj_b": _pack_weight(w_proj_b[0], "proj_b"),
        "w_out": _pack_weight(w_out[0], "out"), "ln1_g": ln1_g, "ln1_b": ln1_b,
        "w_up": _pack_weight(w_up[0], "up"), "w_fconv": w_ffn_conv[0], "b_fconv": b_ffn_conv,
        "w_down": _pack_weight(w_down[0], "down"), "ln2_g": ln2_g, "ln2_b": ln2_b,
    }

    x_meta = jnp.pad(meta_tokens, ((0, CHUNK - N_META), (0, 0)))
    zero_states = [jnp.zeros((1,) + s, F32) for s in STATE_SHAPES]
    x1_m, tails_m, states_m = _mixer(
        x_meta, jnp.zeros((1, SUBLANES, 2 * M_QK), F32), jnp.zeros((1, SUBLANES, S_XBC), F32),
        zero_states, wts, nseq=1, rows=CHUNK, n_valid=N_META)
    _, tail_ffn_m = _ffn(x1_m, jnp.zeros((1, SUBLANES, 2 * D_FF), F32), wts, mode="seq", nseq=1,
                         rows=CHUNK, n_valid=N_META)

    x1_p, tails_p, states_p = _mixer(
        x_prompt.reshape(batch * seq, D_MODEL), tails_m[0], tails_m[1], states_m, wts,
        nseq=batch, rows=prompt_tile, n_valid=prompt_tile)
    y_p, tail_ffn_p = _ffn(x1_p, tail_ffn_m, wts, mode="seq", nseq=batch, rows=ffn_tile, n_valid=ffn_tile)

    def view(a):
        return a.reshape(dec_batch, -1)

    acts, (xn, xs_tm, so, sz, sg), (new_qk, new_xbc) = _in_proj(
        view(x_sample), view(state_mlstm_conv[0]), view(state_ssm_conv[0]), wts)
    m_in = jnp.pad(state_mlstm_m[0], ((0, 0), (0, LANES - M_HEADS)))[:, None, :]
    h_s, yss_s, states_s = _sample_recur(
        [a.reshape(dec_batch * SEQ_PAD, -1) for a in acts],
        (state_mlstm_C[0], state_mlstm_n[0], m_in, state_ssm[0].reshape(dec_batch, S_INNER, S_STATE)),
        nseq=dec_batch)
    x1_s = _post(h_s.reshape(dec_batch, SEQ_PAD, M_V), yss_s.reshape(dec_batch, SEQ_PAD, S_INNER),
                 xs_tm, so, sz, sg, xn, wts)
    y_s, new_ffn = _ffn(x1_s, view(state_ffn_conv[0]), wts, mode="tm", nseq=1,
                        rows=SAMPLE_BLOCK * DEC_SEQ, n_valid=SAMPLE_BLOCK * DEC_SEQ)

    def pack_states(convs, states, n):
        c1, n1, m1, s1 = states
        return (convs[0][None], c1[None], n1[None], m1[:, 0, :M_HEADS][None], convs[1][None],
                s1.reshape(n, S_HEADS, S_HEADDIM, S_STATE)[None], convs[2][None])

    def tail_rows(t, width):
        return t[:, SUBLANES - (width - 1):, :]

    p_out = pack_states((tail_rows(tails_p[0], M_CONV), tail_rows(tails_p[1], S_CONV),
                         tail_rows(tail_ffn_p, F_CONV)), states_p, batch)
    s_out = pack_states((new_qk.reshape(dec_batch, M_CONV - 1, 2 * M_QK),
                         new_xbc.reshape(dec_batch, S_CONV - 1, S_XBC),
                         new_ffn.reshape(dec_batch, F_CONV - 1, 2 * D_FF)), states_s, dec_batch)
    y_prompt = y_p.reshape(batch, seq, D_MODEL)
    y_sample = y_s.reshape(dec_batch, dec_seq, D_MODEL)
    return (y_prompt, y_sample) + p_out + s_out
```

```python
import functools

import jax
import jax.numpy as jnp
from jax import lax
from jax.experimental import pallas as pl
from jax.experimental.pallas import tpu as pltpu

F32 = jnp.float32
BF16 = jnp.bfloat16

D_MODEL = 1024
N_META = 16
M_HEADS = 4
M_DK = 128
M_DV = 256
M_QK = M_HEADS * M_DK
M_V = M_HEADS * M_DV
M_CONV = 4
S_INNER = 2048
S_HEADDIM = 64
S_HEADS = 32
S_GROUPS = 4
S_STATE = 128
S_CONV = 4
S_BC = S_GROUPS * S_STATE
S_XBC = S_INNER + 2 * S_BC
S_GROUP_W = S_INNER // S_GROUPS
S_HEADS_PER_GROUP = S_HEADS // S_GROUPS
D_FF = 2816
F_CONV = 3
ALPHA = 2.0 ** 0.25
LN_EPS = 1e-5
RMS_EPS = 1e-5

LANES = 128
SUBLANES = 8
BF16_ROWS = 16
CHUNK = 128
SEQ_PAD = SUBLANES
DEC_SEQ = 4
SAMPLE_SEQS_PER_STEP = 8
SAMPLE_BLOCK = 32
PACK_COLS = 512
PACK_ROWS = 128
FFN_BLOCK = 256
FFN_AHEAD = 3
FILLER_SLOTS = 2
VMEM_LIMIT_BYTES = 60 * 1024 * 1024

OFF_QK = 0
OFF_V = OFF_QK + 2 * M_QK
OFF_O = OFF_V + M_V
OFF_Z = OFF_O + M_V
OFF_XBC = OFF_Z + S_INNER
OFF_GATE = OFF_XBC + S_XBC
W_BIG_COLS = OFF_GATE + 2 * D_MODEL
LANE_IG = 0
LANE_LF = M_HEADS
LANE_DTA = 2 * M_HEADS
LANE_DT = LANE_DTA + S_HEADS
LANE_END = LANE_DT + S_HEADS

IN_I = 2 * M_QK + 2 * M_V
IN_Z = IN_I + 2 * M_HEADS
IN_DT = IN_Z + S_INNER + S_XBC
IN_GATE = IN_DT + S_HEADS
D_IN = IN_GATE + 2 * D_MODEL

STATE_SHAPES = [(M_HEADS, M_DK, M_DV), (M_HEADS, M_DK), (1, LANES), (S_INNER, S_STATE)]


def _softplus(x):
    return jnp.maximum(x, 0.0) + jnp.log1p(jnp.exp(-jnp.abs(x)))


def _silu(x):
    return x * jax.nn.sigmoid(x)


def _layer_norm(x, g, b):
    mu = jnp.mean(x, axis=-1, keepdims=True)
    xc = x - mu
    var = jnp.mean(xc * xc, axis=-1, keepdims=True)
    return xc * lax.rsqrt(var + LN_EPS) * g + b


def _dot(a, b):
    return jnp.dot(a, b, preferred_element_type=F32)


def _wdot(a, w_ref, lo=None, hi=None):
    w = w_ref[...] if lo is None else w_ref[:, lo:hi]
    return _dot(a, pltpu.bitcast(w, BF16))


def _dot_nt(a, b):
    return lax.dot_general(a, b, (((1,), (1,)), ((), ())), preferred_element_type=F32)


def _dot_tn(a, b):
    return lax.dot_general(a, b, (((0,), (0,)), ((), ())), preferred_element_type=F32)


def _pad_rows(x, rows):
    if x.shape[0] == rows:
        return x
    return jnp.concatenate([x, jnp.zeros((rows - x.shape[0], x.shape[1]), x.dtype)], axis=0)


def _conv_carry(u, carry_ref, w_ref, b_ref, width, n_valid, cols=slice(None)):
    full = jnp.concatenate([carry_ref[:, cols], u], axis=0)
    acc = b_ref[:, cols] + u * w_ref[width - 1:width, cols]
    for k in range(1, width):
        acc = acc + pltpu.roll(full, k, 0)[SUBLANES:] * w_ref[width - 1 - k:width - k, cols]
    carry_ref[:, cols] = u[n_valid - SUBLANES:n_valid]
    return acc


def _time_major(ref):
    return jnp.concatenate([ref[:, t, :] for t in range(DEC_SEQ)], axis=0)


def _store_seq_major(ref, x, fill=None):
    nseq, _, c = ref.shape
    pad = jnp.zeros((1, c), x.dtype) if fill is None else fill
    ref[:, DEC_SEQ:, :] = jnp.broadcast_to(pad[None], (nseq, SEQ_PAD - DEC_SEQ, c))
    for t in range(DEC_SEQ):
        ref[:, t, :] = x[t * nseq:(t + 1) * nseq]


def _conv_tm(u, st_ref, new_ref, w_ref, b_ref, width, row_axis, cols=slice(None)):
    def idx(j):
        return (j, slice(None), cols) if row_axis == 0 else (slice(None), j, cols)

    nseq = st_ref.shape[1 - row_axis]
    full = ([st_ref[idx(j)] for j in range(width - 1)]
            + [u[t * nseq:(t + 1) * nseq] for t in range(DEC_SEQ)])
    outs = []
    for t in range(DEC_SEQ):
        acc = b_ref[:, cols]
        for j in range(width):
            acc = acc + full[t + j] * w_ref[j:j + 1, cols]
        outs.append(acc)
    for j in range(width - 1):
        new_ref[idx(j)] = full[DEC_SEQ + j]
    return jnp.concatenate(outs, axis=0)


def _gate_tile(xb, valid, wsm_ref, bsm_ref, alog_ref):
    g = _wdot(xb, wsm_ref) + bsm_ref[...]
    lane = lax.broadcasted_iota(jnp.int32, (1, LANES), 1)
    sp = _softplus(g)
    lsg = -_softplus(-g)
    a_row = jnp.where((lane >= LANE_DTA) & (lane < LANE_DT), -jnp.exp(alog_ref[...]), 0.0)
    tile = jnp.where(lane < LANE_LF, g,
                     jnp.where(lane < LANE_DTA, lsg,
                               jnp.where(lane < LANE_DT, sp * a_row,
                                         jnp.where(lane < LANE_END, sp, 0.0))))
    if valid is None:
        return tile
    return jnp.where(valid, tile, jnp.where(lane < LANE_LF, -jnp.inf, 0.0))


def _cumsum_rows(x):
    row = lax.broadcasted_iota(jnp.int32, x.shape, 0)
    k = 1
    while k < x.shape[0]:
        x = x + jnp.where(row >= k, pltpu.roll(x, k, 0), 0.0)
        k *= 2
    return x


def _chunk_recur(q, k, v, gt, xs, bm, cm, st_in, st_out, fillers=None):
    staged = fillers is not None
    fillers = list(fillers or ())

    def fill():
        if fillers:
            fillers.pop(0)()

    def run(vector_stage, matmul_stage, items):
        if staged:
            ctxs = [vector_stage(i) for i in items]
            return [matmul_stage(c) for c in ctxs]
        return [matmul_stage(vector_stage(i)) for i in items]

    c_ref, n_ref, m_ref, s_ref = st_in
    c_out, n_out, m_out, s_out = st_out
    tr = q.shape[0]
    lane = lax.broadcasted_iota(jnp.int32, (1, LANES), 1)
    cum_lanes = (lane >= LANE_LF) & (lane < LANE_DT)
    gc = jnp.where(cum_lanes, _cumsum_rows(jnp.where(cum_lanes, gt, 0.0)), gt)
    if tr < CHUNK:
        tail = jnp.where(lane < LANE_LF, -jnp.inf, jnp.where(cum_lanes, gc[tr - 1:tr, :], 0.0))
        gc_full = jnp.concatenate([gc, jnp.broadcast_to(tail, (CHUNK - tr, LANES))], axis=0)
    else:
        gc_full = gc
    gct = gc_full.T
    k_s = _pad_rows(k, CHUNK)
    bm_s = _pad_rows(bm, CHUNK)

    def head_dots(h):
        qh = q[:, h * M_DK:(h + 1) * M_DK]
        c0 = c_ref[h]
        return _dot_nt(qh, k_s[:, h * M_DK:(h + 1) * M_DK]), _dot(qh, c0.astype(BF16)), c0

    def group_dots(g):
        cg = cm[:, g * S_STATE:(g + 1) * S_STATE]
        s_old = s_ref[g * S_GROUP_W:(g + 1) * S_GROUP_W, :]
        return (_dot_nt(cg, bm_s[:, g * S_STATE:(g + 1) * S_STATE]),
                _dot_nt(cg, s_old.astype(BF16)), s_old)

    if staged:
        head_pre = [head_dots(h) for h in range(M_HEADS)]
        group_pre = [group_dots(g) for g in range(S_GROUPS)]
        head_dots = head_pre.__getitem__
        group_dots = group_pre.__getitem__
    fill()

    last = gc[tr - 1:tr, :]
    ti = lax.broadcasted_iota(jnp.int32, (tr, CHUNK), 0)
    si = lax.broadcasted_iota(jnp.int32, (tr, CHUNK), 1)
    causal = si <= ti
    v_s = _pad_rows(v, CHUNK)
    m_row = m_ref[...]

    def head_vector(h):
        qh = q[:, h * M_DK:(h + 1) * M_DK]
        bt_c = gc[:, LANE_LF + h:LANE_LF + h + 1]
        ig_c = gc[:, LANE_IG + h:LANE_IG + h + 1]
        bt_r = gct[LANE_LF + h:LANE_LF + h + 1, :]
        ig_r = gct[LANE_IG + h:LANE_IG + h + 1, :]
        m0 = m_row[:, h:h + 1]
        sqk, qc, c0 = head_dots(h)
        n0 = n_ref[h:h + 1, :]

        dmat = jnp.where(causal, bt_c - bt_r + ig_r, -jnp.inf)
        inter = bt_c + m0
        m_t = jnp.maximum(inter, jnp.max(dmat, axis=1, keepdims=True))
        w_intra = jnp.exp(dmat - m_t)
        w_inter = jnp.exp(inter - m_t)
        s = sqk * w_intra
        den = (jnp.sum(s, axis=1, keepdims=True)
               + jnp.sum(qh.astype(F32) * n0, axis=1, keepdims=True) * w_inter)
        rdenom = 1.0 / jnp.maximum(jnp.abs(den), jnp.exp(-m_t))

        b_last = last[:, LANE_LF + h:LANE_LF + h + 1]
        m_new = jnp.maximum(b_last + m0, jnp.max(b_last - bt_r + ig_r, axis=1, keepdims=True))
        w_last = jnp.exp(b_last - bt_c + ig_c - m_new)
        decay = jnp.exp(b_last + m0 - m_new)
        kw = k[:, h * M_DK:(h + 1) * M_DK].astype(F32) * w_last
        n_out[h:h + 1, :] = decay * n0 + jnp.sum(kw, axis=0, keepdims=True)
        return h, s.astype(BF16), qc * w_inter, rdenom, kw.astype(BF16), decay * c0, m_new

    def head_matmul(ctx):
        h, s_b, inter_part, rdenom, kw_b, c_decayed, m_new = ctx
        num = _dot(s_b, v_s[:, h * M_DV:(h + 1) * M_DV]) + inter_part
        c_out[h] = c_decayed + _dot_tn(kw_b, v[:, h * M_DV:(h + 1) * M_DV])
        return num * rdenom, m_new

    heads = run(head_vector, head_matmul, range(M_HEADS))
    m_new_row = m_row
    for h, (_, m_new) in enumerate(heads):
        m_new_row = jnp.where(lane == h, m_new, m_new_row)
    m_out[...] = m_new_row
    fill()

    lo_t = lax.broadcasted_iota(jnp.int32, (tr, LANES), 1) < S_HEADDIM
    lo_s = lax.broadcasted_iota(jnp.int32, (CHUNK, LANES), 1) < S_HEADDIM
    xs_b = xs.astype(BF16)

    def group_vector(g):
        rows_g = slice(g * S_GROUP_W, (g + 1) * S_GROUP_W)
        cb, y_inter, s_old = group_dots(g)
        pairs, wd_parts, dec_parts = [], [], []
        for p in range(S_HEADS_PER_GROUP // 2):
            ms, es, wds = [], [], []
            for r in (2 * p, 2 * p + 1):
                hh = g * S_HEADS_PER_GROUP + r
                bt_c = gc[:, LANE_DTA + hh:LANE_DTA + hh + 1]
                dt_c = gc[:, LANE_DT + hh:LANE_DT + hh + 1]
                bt_r = gct[LANE_DTA + hh:LANE_DTA + hh + 1, :]
                dt_r = gct[LANE_DT + hh:LANE_DT + hh + 1, :]
                b_last = last[:, LANE_DTA + hh:LANE_DTA + hh + 1]
                dec = jnp.exp(jnp.where(causal, bt_c - bt_r, -jnp.inf))
                ms.append((cb * dec * dt_r).astype(BF16))
                es.append(jnp.exp(bt_c))
                wds.append(jnp.exp(b_last - bt_c) * dt_c)
                dec_parts.append(jnp.broadcast_to(jnp.exp(b_last), (S_HEADDIM, 1)))
            col = g * S_GROUP_W + p * LANES
            xp = _pad_rows(xs_b[:, col:col + LANES], CHUNK)
            zero = jnp.zeros_like(xp)
            rhs = jnp.concatenate([jnp.where(lo_s, xp, zero), jnp.where(lo_s, zero, xp)], axis=0)
            pairs.append((jnp.concatenate(ms, axis=1), rhs,
                          y_inter[:, p * LANES:(p + 1) * LANES] * jnp.where(lo_t, es[0], es[1])))
            wd_parts.append(jnp.where(lo_t, wds[0], wds[1]))
        a_mat = (xs[:, rows_g] * jnp.concatenate(wd_parts, axis=1)).astype(BF16)
        return g, pairs, a_mat, jnp.concatenate(dec_parts, axis=0) * s_old

    def group_matmul(ctx):
        g, pairs, a_mat, s_decayed = ctx
        ys = [_dot(lhs, rhs) + inter_part for lhs, rhs, inter_part in pairs]
        s_out[g * S_GROUP_W:(g + 1) * S_GROUP_W, :] = (
            s_decayed + _dot_tn(a_mat, bm[:, g * S_STATE:(g + 1) * S_STATE]))
        return jnp.concatenate(ys, axis=1)

    groups = run(group_vector, group_matmul, range(S_GROUPS))
    return jnp.concatenate([hp for hp, _ in heads], axis=1), jnp.concatenate(groups, axis=1)


def _post_math(h, y, xs, so, sz, sg, xn, mg_ref, sng_ref, dexp_ref, wpa_ref, wpb_ref, wout_ref,
               g1_ref, b1_ref):
    hs = []
    for i in range(M_HEADS):
        hh = h[:, i * M_DV:(i + 1) * M_DV]
        mu = jnp.mean(hh, axis=-1, keepdims=True)
        hc = hh - mu
        var = jnp.mean(hc * hc, axis=-1, keepdims=True)
        hs.append(hc * lax.rsqrt(var + LN_EPS))
    hn = jnp.concatenate(hs, axis=1) * mg_ref[...]
    ya = _wdot((so * hn).astype(BF16), wpa_ref)

    y = (y + dexp_ref[...] * xs) * sz
    ys = []
    for g in range(S_GROUPS):
        yg = y[:, g * S_GROUP_W:(g + 1) * S_GROUP_W]
        ys.append(yg * lax.rsqrt(jnp.mean(yg * yg, axis=-1, keepdims=True) + RMS_EPS))
    yn = jnp.concatenate(ys, axis=1) * sng_ref[...]
    yb = _wdot(yn.astype(BF16), wpb_ref)

    mixed = _wdot((sg[:, :D_MODEL] * ya + sg[:, D_MODEL:] * yb).astype(BF16), wout_ref)
    return _layer_norm(ALPHA * xn + mixed, g1_ref[...], b1_ref[...])


def _const_spec(shape):
    nd = len(shape)
    return pl.BlockSpec(shape, lambda *_: (0,) * nd, pipeline_mode=pl.Buffered(1))


def _pack_body(w_ref, o_ref):
    o_ref[...] = pltpu.bitcast(w_ref[...].astype(BF16), jnp.uint32)


def _pack_weight(w, name):
    k, n = w.shape
    bn = min(n, PACK_COLS)
    return pl.pallas_call(
        _pack_body, grid=(n // bn,), in_specs=[pl.BlockSpec((k, bn), lambda j: (0, j))],
        out_specs=pl.BlockSpec((k // 2, bn), lambda j: (0, j)),
        out_shape=jax.ShapeDtypeStruct((k // 2, n), jnp.uint32), name=f"pack_{name}",
        compiler_params=_params(1),
    )(w)


def _pack_t_body(wt_ref, o_ref):
    o_ref[...] = pltpu.bitcast(wt_ref[...].T.astype(BF16), jnp.uint32)


def _pack_gates_body(wif_ref, wdt_ref, o_ref):
    wdt = wdt_ref[...]
    rows = jnp.concatenate([wif_ref[...], wdt, wdt, jnp.zeros((LANES - LANE_END, D_MODEL), F32)], axis=0)
    o_ref[...] = pltpu.bitcast(rows.T.astype(BF16), jnp.uint32)


def _pack_in(wt):
    k = wt.shape[1]
    assert wt.shape[0] == D_IN
    starts = (0, IN_Z - IN_I, IN_GATE - IN_DT + IN_Z - IN_I)
    lens = (IN_I, IN_DT - IN_Z, D_IN - IN_GATE)
    assert all(s % SUBLANES == 0 for s in starts) and all(n % PACK_COLS == 0 for n in lens)
    b1 = lens[0] // PACK_COLS
    b2 = b1 + lens[1] // PACK_COLS

    def src_row(j):
        s0, s1, s2 = (s // SUBLANES for s in starts)
        return (j * (PACK_COLS // SUBLANES) + jnp.where(j < b1, s0, jnp.where(j < b2, s1, s2))) * SUBLANES

    big = pl.pallas_call(
        _pack_t_body, grid=(W_BIG_COLS // PACK_COLS,),
        in_specs=[pl.BlockSpec((pl.Element(PACK_COLS), pl.Element(k)), lambda j: (src_row(j), 0))],
        out_specs=pl.BlockSpec((k // 2, PACK_COLS), lambda j: (0, j)),
        out_shape=jax.ShapeDtypeStruct((k // 2, W_BIG_COLS), jnp.uint32),
        name="pack_in", compiler_params=_params(1),
    )(wt)
    small = pl.pallas_call(
        _pack_gates_body, out_shape=jax.ShapeDtypeStruct((k // 2, LANES), jnp.uint32), name="pack_gates",
    )(wt[IN_I:IN_Z], wt[IN_DT:IN_GATE])
    return big, small


def _params(dims):
    return pltpu.CompilerParams(dimension_semantics=("arbitrary",) * dims,
                                vmem_limit_bytes=VMEM_LIMIT_BYTES)


def _mixer_body(rows, n_valid,
                x_ref, g0_ref, b0_ref, wbig_ref, wsm_ref, bsm_ref, alog_ref,
                wmc_ref, bmc_ref, wsc_ref, bsc_ref, cin_qk_ref, cin_xbc_ref,
                c0_ref, n0_ref, m0_ref, s0_ref,
                mg_ref, sng_ref, dexp_ref, wpa_ref, wpb_ref, wout_ref, g1_ref, b1_ref,
                x1_ref, cout_qk_ref, cout_xbc_ref, c1_ref, n1_ref, m1_ref, s1_ref):
    @pl.when(pl.program_id(1) == 0)
    def _():
        cout_qk_ref[...] = cin_qk_ref[...]
        cout_xbc_ref[...] = cin_xbc_ref[...]
        c1_ref[...] = c0_ref[...]
        n1_ref[...] = n0_ref[...]
        m1_ref[...] = m0_ref[...]
        s1_ref[...] = s0_ref[...]

    xn = _layer_norm(x_ref[...], g0_ref[...], b0_ref[...])
    xb = xn.astype(BF16)
    valid = None
    if n_valid < rows:
        valid = lax.broadcasted_iota(jnp.int32, (rows, 1), 0) < n_valid
    gt = _gate_tile(xb, valid, wsm_ref, bsm_ref, alog_ref)

    u_qk = _wdot(xb, wbig_ref, OFF_QK, OFF_QK + 2 * M_QK)
    qk = _silu(_conv_carry(u_qk, cout_qk_ref, wmc_ref, bmc_ref, M_CONV, n_valid))
    q = (qk[:, :M_QK] * (M_DK ** -0.5)).astype(BF16)
    k = qk[:, M_QK:].astype(BF16)
    v = _wdot(xb, wbig_ref, OFF_V, OFF_V + M_V).astype(BF16)
    u_xbc = _wdot(xb, wbig_ref, OFF_XBC, OFF_XBC + S_XBC)
    xbc = _silu(_conv_carry(u_xbc, cout_xbc_ref, wsc_ref, bsc_ref, S_CONV, n_valid))
    xs = xbc[:, :S_INNER]
    bm = xbc[:, S_INNER:S_INNER + S_BC].astype(BF16)
    cm = xbc[:, S_INNER + S_BC:].astype(BF16)

    offs = [OFF_O, OFF_Z, OFF_Z + D_MODEL, OFF_GATE, OFF_GATE + D_MODEL]
    dense = {}
    nchunk = rows // CHUNK
    nslot = nchunk * FILLER_SLOTS
    hs, ys = [], []
    for ci in range(nchunk):
        sl = slice(ci * CHUNK, (ci + 1) * CHUNK)
        state = (c1_ref, n1_ref, m1_ref, s1_ref)

        def filler(slot):
            def issue():
                for off in offs[slot::nslot]:
                    dense[off] = _wdot(xb, wbig_ref, off, off + D_MODEL)
            return issue

        h, y = _chunk_recur(q[sl], k[sl], v[sl], gt[sl], xs[sl], bm[sl], cm[sl], state, state,
                            [filler(ci * FILLER_SLOTS + i) for i in range(FILLER_SLOTS)])
        hs.append(h)
        ys.append(y)
    h = jnp.concatenate(hs, axis=0)
    y = jnp.concatenate(ys, axis=0)

    so = jax.nn.sigmoid(dense[OFF_O])
    sz = _silu(jnp.concatenate([dense[OFF_Z], dense[OFF_Z + D_MODEL]], axis=1))
    sg = jax.nn.sigmoid(jnp.concatenate([dense[OFF_GATE], dense[OFF_GATE + D_MODEL]], axis=1))
    x1_ref[...] = _post_math(h, y, xs, so, sz, sg, xn, mg_ref, sng_ref, dexp_ref,
                             wpa_ref, wpb_ref, wout_ref, g1_ref, b1_ref)


def _mixer(x_rows, cin_qk, cin_xbc, states, wts, *, nseq, rows, n_valid):
    total = x_rows.shape[0]
    nt = total // (nseq * rows)
    rspec = pl.BlockSpec((rows, D_MODEL), lambda b, c: (b * nt + c, 0))

    def per_seq(shape):
        nd = len(shape)
        return pl.BlockSpec((None,) + shape, lambda b, c: (b,) + (0,) * nd)

    def shared(shape):
        nd = len(shape)
        return pl.BlockSpec((None,) + shape, lambda b, c: (0,) * (nd + 1), pipeline_mode=pl.Buffered(1))

    tails = [(SUBLANES, 2 * M_QK), (SUBLANES, S_XBC)]
    in_specs = ([rspec, _const_spec((1, D_MODEL)), _const_spec((1, D_MODEL)),
                 _const_spec((D_MODEL // 2, W_BIG_COLS)), _const_spec((D_MODEL // 2, LANES)),
                 _const_spec((1, LANES)), _const_spec((1, LANES)),
                 _const_spec((M_CONV, 2 * M_QK)), _const_spec((1, 2 * M_QK)),
                 _const_spec((S_CONV, S_XBC)), _const_spec((1, S_XBC))]
                + [shared(s) for s in tails + STATE_SHAPES]
                + [_const_spec((1, M_V)), _const_spec((1, S_INNER)), _const_spec((1, S_INNER)),
                   _const_spec((M_V // 2, D_MODEL)), _const_spec((S_INNER // 2, D_MODEL)),
                   _const_spec((D_MODEL // 2, D_MODEL)), _const_spec((1, D_MODEL)), _const_spec((1, D_MODEL))])
    out_specs = [rspec] + [per_seq(s) for s in tails + STATE_SHAPES]
    out_shapes = ([jax.ShapeDtypeStruct((total, D_MODEL), F32)]
                  + [jax.ShapeDtypeStruct((nseq,) + s, F32) for s in tails + STATE_SHAPES])
    outs = pl.pallas_call(
        functools.partial(_mixer_body, rows, n_valid),
        grid=(nseq, nt), in_specs=in_specs, out_specs=out_specs, out_shape=out_shapes,
        name=f"mixer_{nseq}", compiler_params=_params(2),
    )(x_rows, wts["ln0_g"], wts["ln0_b"], wts["w_big"], wts["w_small"], wts["b_small"], wts["a_log"],
      wts["w_mconv"], wts["b_mconv"], wts["w_sconv"], wts["b_sconv"], cin_qk, cin_xbc, *states,
      wts["mnorm_g"], wts["snorm_g"], wts["d_exp"], wts["w_proj_a"], wts["w_proj_b"], wts["w_out"],
      wts["ln1_g"], wts["ln1_b"])
    return outs[0], outs[1:3], outs[3:]


def _in_proj_body(x_ref, g0_ref, b0_ref, wbig_ref, wsm_ref, bsm_ref, alog_ref,
                  wmc_ref, bmc_ref, wsc_ref, bsc_ref, st_qk_ref, st_xbc_ref,
                  q_ref, k_ref, v_ref, gate_ref, xs_ref, bm_ref, cm_ref,
                  xn_ref, xs_tm_ref, so_ref, sz_ref, sg_ref, new_qk_ref, new_xbc_ref):
    xn = _layer_norm(_time_major(x_ref), g0_ref[...], b0_ref[...])
    xn_ref[...] = xn
    xb = xn.astype(BF16)
    lane = lax.broadcasted_iota(jnp.int32, (1, LANES), 1)
    _store_seq_major(gate_ref, _gate_tile(xb, None, wsm_ref, bsm_ref, alog_ref),
                     jnp.where(lane < LANE_LF, -jnp.inf, 0.0))

    u_qk = _wdot(xb, wbig_ref, OFF_QK, OFF_QK + 2 * M_QK)
    qk = _silu(_conv_tm(u_qk, st_qk_ref, new_qk_ref, wmc_ref, bmc_ref, M_CONV, 0))
    _store_seq_major(q_ref, qk[:, :M_QK] * (M_DK ** -0.5))
    _store_seq_major(k_ref, qk[:, M_QK:])
    _store_seq_major(v_ref, _wdot(xb, wbig_ref, OFF_V, OFF_V + M_V))
    so_ref[...] = jax.nn.sigmoid(_wdot(xb, wbig_ref, OFF_O, OFF_O + M_V))
    sz_ref[...] = _silu(_wdot(xb, wbig_ref, OFF_Z, OFF_Z + S_INNER))
    u_xbc = _wdot(xb, wbig_ref, OFF_XBC, OFF_XBC + S_XBC)
    xbc = _silu(_conv_tm(u_xbc, st_xbc_ref, new_xbc_ref, wsc_ref, bsc_ref, S_CONV, 0))
    xs_tm_ref[...] = xbc[:, :S_INNER]
    _store_seq_major(xs_ref, xbc[:, :S_INNER])
    _store_seq_major(bm_ref, xbc[:, S_INNER:S_INNER + S_BC])
    _store_seq_major(cm_ref, xbc[:, S_INNER + S_BC:])
    sg_ref[...] = jax.nn.sigmoid(_wdot(xb, wbig_ref, OFF_GATE, OFF_GATE + 2 * D_MODEL))


def _in_proj(x, st_qk, st_xbc, wts):
    nseq = x.shape[0]
    nb = SAMPLE_BLOCK

    def rows_first(width, cols):
        return pl.BlockSpec((width - 1, nb, cols), lambda i: (0, i, 0))

    def sspec(rows, cols):
        return pl.BlockSpec((nb, rows, cols), lambda i: (i, 0, 0))

    def tspec(cols):
        return pl.BlockSpec((nb * DEC_SEQ, cols), lambda i: (i, 0))

    in_specs = [sspec(DEC_SEQ, D_MODEL), _const_spec((1, D_MODEL)), _const_spec((1, D_MODEL)),
                _const_spec((D_MODEL // 2, W_BIG_COLS)), _const_spec((D_MODEL // 2, LANES)),
                _const_spec((1, LANES)), _const_spec((1, LANES)),
                _const_spec((M_CONV, 2 * M_QK)), _const_spec((1, 2 * M_QK)),
                _const_spec((S_CONV, S_XBC)), _const_spec((1, S_XBC)),
                rows_first(M_CONV, 2 * M_QK), rows_first(S_CONV, S_XBC)]
    seq_cols = [M_QK, M_QK, M_V, LANES, S_INNER, S_BC, S_BC]
    tm_cols = [D_MODEL, S_INNER, M_V, S_INNER, 2 * D_MODEL]
    outs = pl.pallas_call(
        _in_proj_body, grid=(nseq // nb,), in_specs=in_specs,
        out_specs=([sspec(SEQ_PAD, c) for c in seq_cols] + [tspec(c) for c in tm_cols]
                   + [rows_first(M_CONV, 2 * M_QK), rows_first(S_CONV, S_XBC)]),
        out_shape=([jax.ShapeDtypeStruct((nseq, SEQ_PAD, c), F32) for c in seq_cols]
                   + [jax.ShapeDtypeStruct((nseq * DEC_SEQ, c), F32) for c in tm_cols]
                   + [jax.ShapeDtypeStruct((M_CONV - 1, nseq, 2 * M_QK), F32),
                      jax.ShapeDtypeStruct((S_CONV - 1, nseq, S_XBC), F32)]),
        name="sample_in_proj", compiler_params=_params(1),
    )(x, wts["ln0_g"], wts["ln0_b"], wts["w_big"], wts["w_small"], wts["b_small"], wts["a_log"],
      wts["w_mconv"], wts["b_mconv"], wts["w_sconv"], wts["b_sconv"], st_qk, st_xbc)
    return outs[:7], outs[7:12], outs[12:]


def _sample_recur_body(q_ref, k_ref, v_ref, gate_ref, xs_ref, bm_ref, cm_ref,
                       c0_ref, n0_ref, m0_ref, s0_ref,
                       h_ref, y_ref, c1_ref, n1_ref, m1_ref, s1_ref):
    lane = lax.broadcasted_iota(jnp.int32, (1, LANES), 1)
    gate_pad = jnp.broadcast_to(jnp.where(lane < LANE_LF, -jnp.inf, 0.0), (BF16_ROWS - SEQ_PAD, LANES))

    for i in range(SAMPLE_SEQS_PER_STEP):
        rows = slice(i * SEQ_PAD, (i + 1) * SEQ_PAD)

        def load(ref, dtype):
            return _pad_rows(ref[rows, :], BF16_ROWS).astype(dtype)

        gt = jnp.concatenate([gate_ref[rows, :], gate_pad], axis=0)
        h, y = _chunk_recur(load(q_ref, BF16), load(k_ref, BF16), load(v_ref, BF16), gt,
                            load(xs_ref, F32), load(bm_ref, BF16), load(cm_ref, BF16),
                            (c0_ref.at[i], n0_ref.at[i], m0_ref.at[i], s0_ref.at[i]),
                            (c1_ref.at[i], n1_ref.at[i], m1_ref.at[i], s1_ref.at[i]))
        h_ref[rows, :] = h[:SEQ_PAD]
        y_ref[rows, :] = y[:SEQ_PAD]


def _sample_recur(acts, states, *, nseq):
    q, k, v, gate, xs, bm, cm = acts
    total = q.shape[0]
    nb = SAMPLE_SEQS_PER_STEP

    def rspec(cols):
        return pl.BlockSpec((nb * SEQ_PAD, cols), lambda i: (i, 0))

    def sspec(shape):
        nd = len(shape)
        return pl.BlockSpec((nb,) + shape, lambda i: (i,) + (0,) * nd)

    in_specs = ([rspec(M_QK), rspec(M_QK), rspec(M_V), rspec(LANES), rspec(S_INNER), rspec(S_BC), rspec(S_BC)]
                + [sspec(s) for s in STATE_SHAPES])
    out_specs = [rspec(M_V), rspec(S_INNER)] + [sspec(s) for s in STATE_SHAPES]
    out_shapes = ([jax.ShapeDtypeStruct((total, M_V), F32), jax.ShapeDtypeStruct((total, S_INNER), F32)]
                  + [jax.ShapeDtypeStruct((nseq,) + s, F32) for s in STATE_SHAPES])
    outs = pl.pallas_call(
        _sample_recur_body, grid=(nseq // nb,), in_specs=in_specs, out_specs=out_specs,
        out_shape=out_shapes, name="sample_recur", compiler_params=_params(1),
    )(q, k, v, gate, xs, bm, cm, *states)
    return outs[0], outs[1], outs[2:]


def _post_body(h_ref, y_ref, xs_ref, so_ref, sz_ref, sg_ref, xn_ref,
               mg_ref, sng_ref, dexp_ref, wpa_ref, wpb_ref, wout_ref, g1_ref, b1_ref, x1_ref):
    h = jnp.concatenate([h_ref[:, t, :] for t in range(DEC_SEQ)], axis=0)
    y = jnp.concatenate([y_ref[:, t, :] for t in range(DEC_SEQ)], axis=0)
    x1_ref[...] = _post_math(h, y, xs_ref[...], so_ref[...], sz_ref[...], sg_ref[...],
                             xn_ref[...], mg_ref, sng_ref, dexp_ref, wpa_ref, wpb_ref, wout_ref,
                             g1_ref, b1_ref)


def _post(h, y, xs, so, sz, sg, xn, wts):
    nseq = h.shape[0]
    nb = SAMPLE_BLOCK

    def sspec(cols):
        return pl.BlockSpec((nb, SEQ_PAD, cols), lambda i: (i, 0, 0))

    def tspec(cols):
        return pl.BlockSpec((nb * DEC_SEQ, cols), lambda i: (i, 0))

    in_specs = [sspec(M_V), sspec(S_INNER), tspec(S_INNER), tspec(M_V), tspec(S_INNER),
                tspec(2 * D_MODEL), tspec(D_MODEL),
                _const_spec((1, M_V)), _const_spec((1, S_INNER)), _const_spec((1, S_INNER)),
                _const_spec((M_V // 2, D_MODEL)), _const_spec((S_INNER // 2, D_MODEL)),
                _const_spec((D_MODEL // 2, D_MODEL)), _const_spec((1, D_MODEL)), _const_spec((1, D_MODEL))]
    return pl.pallas_call(
        _post_body, grid=(nseq // nb,), in_specs=in_specs, out_specs=tspec(D_MODEL),
        out_shape=jax.ShapeDtypeStruct((nseq * DEC_SEQ, D_MODEL), F32), name="sample_post",
        compiler_params=_params(1),
    )(h, y, xs, so, sz, sg, xn, wts["mnorm_g"], wts["snorm_g"], wts["d_exp"],
      wts["w_proj_a"], wts["w_proj_b"], wts["w_out"], wts["ln1_g"], wts["ln1_b"])


def _ffn_body(mode, rows, n_valid,
              x1_ref, wup_ref, wfc_ref, bfc_ref, wdn_ref, g2_ref, b2_ref, cin_ref,
              y_ref, cout_ref):
    x1 = x1_ref[...]
    x1b = x1.astype(BF16)
    if mode == "seq":
        @pl.when(pl.program_id(1) == 0)
        def _():
            cout_ref[...] = cin_ref[...]

    def conv(u, cols):
        if mode == "tm":
            return _conv_tm(u, cin_ref, cout_ref, wfc_ref, bfc_ref, F_CONV, 1, cols)
        return _conv_carry(u, cout_ref, wfc_ref, bfc_ref, F_CONV, n_valid, cols)

    def up_proj(j):
        lo = j * FFN_BLOCK
        return (_wdot(x1b, wup_ref, lo, lo + FFN_BLOCK),
                _wdot(x1b, wup_ref, D_FF + lo, D_FF + lo + FFN_BLOCK))

    nblk = D_FF // FFN_BLOCK
    ff = None
    ups = [up_proj(j) for j in range(min(FFN_AHEAD, nblk))]
    for j in range(nblk):
        lo = j * FFN_BLOCK
        ua, ub = ups[j]
        if j + FFN_AHEAD < nblk:
            ups.append(up_proj(j + FFN_AHEAD))
        va = conv(ua, slice(lo, lo + FFN_BLOCK))
        vb = conv(ub, slice(D_FF + lo, D_FF + lo + FFN_BLOCK))
        act = (_silu(va) * vb).astype(BF16)
        part = _dot(act, pltpu.bitcast(wdn_ref[lo // 2:(lo + FFN_BLOCK) // 2, :], BF16))
        ff = part if ff is None else ff + part
    y = _layer_norm(ALPHA * x1 + ff, g2_ref[...], b2_ref[...])
    if mode == "tm":
        nseq = y_ref.shape[0]
        for t in range(DEC_SEQ):
            y_ref[:, t, :] = y[t * nseq:(t + 1) * nseq]
    else:
        y_ref[...] = y


def _ffn(x1, cin, wts, *, mode, nseq, rows, n_valid):
    total = x1.shape[0]
    nt = total // (nseq * rows)

    def rspec(cols):
        return pl.BlockSpec((rows, cols), lambda b, c: (b * nt + c, 0))

    if mode == "seq":
        cin_spec = pl.BlockSpec((None, SUBLANES, 2 * D_FF), lambda b, c: (0, 0, 0))
        cout_spec = pl.BlockSpec((None, SUBLANES, 2 * D_FF), lambda b, c: (b, 0, 0))
        cout_shape = jax.ShapeDtypeStruct((nseq, SUBLANES, 2 * D_FF), F32)
        y_spec = rspec(D_MODEL)
        y_shape = jax.ShapeDtypeStruct((total, D_MODEL), F32)
    else:
        nb = rows // DEC_SEQ
        sspec = lambda r, cols: pl.BlockSpec((nb, r, cols), lambda b, c: (b * nt + c, 0, 0))
        cin_spec = sspec(F_CONV - 1, 2 * D_FF)
        cout_spec = sspec(F_CONV - 1, 2 * D_FF)
        cout_shape = jax.ShapeDtypeStruct((total // DEC_SEQ, F_CONV - 1, 2 * D_FF), F32)
        y_spec = sspec(DEC_SEQ, D_MODEL)
        y_shape = jax.ShapeDtypeStruct((total // DEC_SEQ, DEC_SEQ, D_MODEL), F32)
    in_specs = [rspec(D_MODEL), _const_spec((D_MODEL // 2, 2 * D_FF)), _const_spec((F_CONV, 2 * D_FF)),
                _const_spec((1, 2 * D_FF)), _const_spec((D_FF // 2, D_MODEL)),
                _const_spec((1, D_MODEL)), _const_spec((1, D_MODEL)), cin_spec]
    return pl.pallas_call(
        functools.partial(_ffn_body, mode, rows, n_valid),
        grid=(nseq, nt), in_specs=in_specs, out_specs=[y_spec, cout_spec],
        out_shape=[y_shape, cout_shape],
        name=f"ffn_{mode}_{nseq}", compiler_params=_params(2),
    )(x1, wts["w_up"], wts["w_fconv"], wts["b_fconv"], wts["w_down"], wts["ln2_g"], wts["ln2_b"], cin)


def kernel(x_prompt, x_sample, state_mlstm_conv, state_mlstm_C, state_mlstm_n, state_mlstm_m, state_ssm_conv, state_ssm, state_ffn_conv, meta_tokens, ln0_g, ln0_b, w_in, b_mlstm_if, w_mlstm_conv, b_mlstm_conv, mlstm_norm_g, w_proj_a, w_ssm_conv, b_ssm_conv, ssm_dt_bias, ssm_A_log, ssm_D, ssm_norm_g, w_proj_b, w_out, ln1_g, ln1_b, w_up, w_ffn_conv, b_ffn_conv, w_down, ln2_g, ln2_b):
    batch, seq, _ = x_prompt.shape
    dec_batch, dec_seq, _ = x_sample.shape
    prompt_tile = 2 * CHUNK
    ffn_tile = 2 * CHUNK
    assert dec_seq == DEC_SEQ and seq % ffn_tile == 0 and meta_tokens.shape[0] == N_META
    assert dec_batch % SAMPLE_SEQS_PER_STEP == 0 and dec_batch % SAMPLE_BLOCK == 0

    w_big, w_small = _pack_in(w_in[0].T)
    lane_pad = jnp.zeros((LANES - LANE_END,), F32)
    b_small = jnp.concatenate([b_mlstm_if[0], ssm_dt_bias[0], ssm_dt_bias[0], lane_pad])[None]
    a_log = jnp.concatenate([jnp.zeros((LANE_DTA,), F32), ssm_A_log[0],
                             jnp.zeros((LANES - LANE_DT,), F32)])[None]
    wts = {
        "ln0_g": ln0_g[None], "ln0_b": ln0_b[None], "w_big": w_big, "w_small": w_small,
        "b_small": b_small, "a_log": a_log,
        "w_mconv": w_mlstm_conv[0], "b_mconv": b_mlstm_conv, "w_sconv": w_ssm_conv[0], "b_sconv": b_ssm_conv,
        "mnorm_g": mlstm_norm_g, "snorm_g": ssm_norm_g, "d_exp": jnp.repeat(ssm_D[0], S_HEADDIM)[None],
        "w_proj_a": _pack_weight(w_proj_a[0], "proj_a"), "w_proj_b": _pack_weight(w_proj_b[0], "proj_b"),
        "w_out": _pack_weight(w_out[0], "out"), "ln1_g": ln1_g, "ln1_b": ln1_b,
        "w_up": _pack_weight(w_up[0], "up"), "w_fconv": w_ffn_conv[0], "b_fconv": b_ffn_conv,
        "w_down": _pack_weight(w_down[0], "down"), "ln2_g": ln2_g, "ln2_b": ln2_b,
    }

    x_meta = jnp.pad(meta_tokens, ((0, CHUNK - N_META), (0, 0)))
    zero_states = [jnp.zeros((1,) + s, F32) for s in STATE_SHAPES]
    x1_m, tails_m, states_m = _mixer(
        x_meta, jnp.zeros((1, SUBLANES, 2 * M_QK), F32), jnp.zeros((1, SUBLANES, S_XBC), F32),
        zero_states, wts, nseq=1, rows=CHUNK, n_valid=N_META)
    _, tail_ffn_m = _ffn(x1_m, jnp.zeros((1, SUBLANES, 2 * D_FF), F32), wts, mode="seq", nseq=1,
                         rows=CHUNK, n_valid=N_META)

    x1_p, tails_p, states_p = _mixer(
        x_prompt.reshape(batch * seq, D_MODEL), tails_m[0], tails_m[1], states_m, wts,
        nseq=batch, rows=prompt_tile, n_valid=prompt_tile)
    y_p, tail_ffn_p = _ffn(x1_p, tail_ffn_m, wts, mode="seq", nseq=batch, rows=ffn_tile, n_valid=ffn_tile)

    def rows_first(a):
        return jnp.swapaxes(a, 0, 1)

    acts, (xn, xs_tm, so, sz, sg), (new_qk, new_xbc) = _in_proj(
        x_sample, rows_first(state_mlstm_conv[0]), rows_first(state_ssm_conv[0]), wts)
    m_in = jnp.pad(state_mlstm_m[0], ((0, 0), (0, LANES - M_HEADS)))[:, None, :]
    h_s, yss_s, states_s = _sample_recur(
        [a.reshape(dec_batch * SEQ_PAD, -1) for a in acts],
        (state_mlstm_C[0], state_mlstm_n[0], m_in, state_ssm[0].reshape(dec_batch, S_INNER, S_STATE)),
        nseq=dec_batch)
    x1_s = _post(h_s.reshape(dec_batch, SEQ_PAD, M_V), yss_s.reshape(dec_batch, SEQ_PAD, S_INNER),
                 xs_tm, so, sz, sg, xn, wts)
    y_s, new_ffn = _ffn(x1_s, state_ffn_conv[0], wts, mode="tm", nseq=1,
                        rows=SAMPLE_BLOCK * DEC_SEQ, n_valid=SAMPLE_BLOCK * DEC_SEQ)

    def pack_states(convs, states, n):
        c1, n1, m1, s1 = states
        return (convs[0][None], c1[None], n1[None], m1[:, 0, :M_HEADS][None], convs[1][None],
                s1.reshape(n, S_HEADS, S_HEADDIM, S_STATE)[None], convs[2][None])

    def tail_rows(t, width):
        return t[:, SUBLANES - (width - 1):, :]

    p_out = pack_states((tail_rows(tails_p[0], M_CONV), tail_rows(tails_p[1], S_CONV),
                         tail_rows(tail_ffn_p, F_CONV)), states_p, batch)
    s_out = pack_states((rows_first(new_qk), rows_first(new_xbc), new_ffn), states_s, dec_batch)
    y_prompt = y_p.reshape(batch, seq, D_MODEL)
    return (y_prompt, y_s) + p_out + s_out
```

```python
import functools

import jax
import jax.numpy as jnp
from jax import lax
from jax.experimental import pallas as pl
from jax.experimental.pallas import tpu as pltpu

F32 = jnp.float32
BF16 = jnp.bfloat16

D_MODEL = 1024
N_META = 16
M_HEADS = 4
M_DK = 128
M_DV = 256
M_QK = M_HEADS * M_DK
M_V = M_HEADS * M_DV
M_CONV = 4
S_INNER = 2048
S_HEADDIM = 64
S_HEADS = 32
S_GROUPS = 4
S_STATE = 128
S_CONV = 4
S_BC = S_GROUPS * S_STATE
S_XBC = S_INNER + 2 * S_BC
S_GROUP_W = S_INNER // S_GROUPS
S_HEADS_PER_GROUP = S_HEADS // S_GROUPS
D_FF = 2816
F_CONV = 3
ALPHA = 2.0 ** 0.25
LN_EPS = 1e-5
RMS_EPS = 1e-5

LANES = 128
SUBLANES = 8
BF16_ROWS = 16
CHUNK = 128
SEQ_PAD = SUBLANES
DEC_SEQ = 4
SAMPLE_SEQS_PER_STEP = 8
SAMPLE_SUB = 16
SAMPLE_BLOCK = 32
PACK_COLS = 512
PACK_ROWS = 128
FFN_BLOCK = 256
FFN_AHEAD = 3
FILLER_SLOTS = 2
VMEM_LIMIT_BYTES = 60 * 1024 * 1024

OFF_QK = 0
OFF_V = OFF_QK + 2 * M_QK
OFF_O = OFF_V + M_V
OFF_Z = OFF_O + M_V
OFF_XBC = OFF_Z + S_INNER
OFF_GATE = OFF_XBC + S_XBC
W_BIG_COLS = OFF_GATE + 2 * D_MODEL
LANE_IG = 0
LANE_LF = M_HEADS
LANE_DTA = 2 * M_HEADS
LANE_DT = LANE_DTA + S_HEADS
LANE_END = LANE_DT + S_HEADS

IN_I = 2 * M_QK + 2 * M_V
IN_Z = IN_I + 2 * M_HEADS
IN_DT = IN_Z + S_INNER + S_XBC
IN_GATE = IN_DT + S_HEADS
D_IN = IN_GATE + 2 * D_MODEL

STATE_SHAPES = [(M_HEADS, M_DK, M_DV), (M_HEADS, M_DK), (1, LANES), (S_INNER, S_STATE)]


def _softplus(x):
    return jnp.maximum(x, 0.0) + jnp.log1p(jnp.exp(-jnp.abs(x)))


def _silu(x):
    return x * jax.nn.sigmoid(x)


def _layer_norm(x, g, b):
    mu = jnp.mean(x, axis=-1, keepdims=True)
    xc = x - mu
    var = jnp.mean(xc * xc, axis=-1, keepdims=True)
    return xc * lax.rsqrt(var + LN_EPS) * g + b


def _dot(a, b):
    return jnp.dot(a, b, preferred_element_type=F32)


def _wdot(a, w_ref, lo=None, hi=None):
    w = w_ref[...] if lo is None else w_ref[:, lo:hi]
    return _dot(a, pltpu.bitcast(w, BF16))


def _dot_nt(a, b):
    return lax.dot_general(a, b, (((1,), (1,)), ((), ())), preferred_element_type=F32)


def _dot_tn(a, b):
    return lax.dot_general(a, b, (((0,), (0,)), ((), ())), preferred_element_type=F32)


def _pad_rows(x, rows):
    if x.shape[0] == rows:
        return x
    return jnp.concatenate([x, jnp.zeros((rows - x.shape[0], x.shape[1]), x.dtype)], axis=0)


def _conv_carry(u, carry_ref, w_ref, b_ref, width, n_valid, cols=slice(None)):
    full = jnp.concatenate([carry_ref[:, cols], u], axis=0)
    acc = b_ref[:, cols] + u * w_ref[width - 1:width, cols]
    for k in range(1, width):
        acc = acc + pltpu.roll(full, k, 0)[SUBLANES:] * w_ref[width - 1 - k:width - k, cols]
    carry_ref[:, cols] = u[n_valid - SUBLANES:n_valid]
    return acc


def _time_major(ref):
    return jnp.concatenate([ref[:, t, :] for t in range(DEC_SEQ)], axis=0)


def _store_seq_major(ref, x, fill=None):
    nseq, _, c = ref.shape
    pad = jnp.zeros((1, c), x.dtype) if fill is None else fill
    ref[:, DEC_SEQ:, :] = jnp.broadcast_to(pad[None], (nseq, SEQ_PAD - DEC_SEQ, c))
    for t in range(DEC_SEQ):
        ref[:, t, :] = x[t * nseq:(t + 1) * nseq]


def _conv_tm(u, st_ref, new_ref, w_ref, b_ref, width, row_axis, cols=slice(None)):
    def idx(j):
        return (j, slice(None), cols) if row_axis == 0 else (slice(None), j, cols)

    nseq = st_ref.shape[1 - row_axis]
    full = ([st_ref[idx(j)] for j in range(width - 1)]
            + [u[t * nseq:(t + 1) * nseq] for t in range(DEC_SEQ)])
    outs = []
    for t in range(DEC_SEQ):
        acc = b_ref[:, cols]
        for j in range(width):
            acc = acc + full[t + j] * w_ref[j:j + 1, cols]
        outs.append(acc)
    for j in range(width - 1):
        new_ref[idx(j)] = full[DEC_SEQ + j]
    return jnp.concatenate(outs, axis=0)


def _gate_tile(xb, valid, wsm_ref, bsm_ref, alog_ref):
    g = _wdot(xb, wsm_ref) + bsm_ref[...]
    lane = lax.broadcasted_iota(jnp.int32, (1, LANES), 1)
    sp = _softplus(g)
    lsg = -_softplus(-g)
    a_row = jnp.where((lane >= LANE_DTA) & (lane < LANE_DT), -jnp.exp(alog_ref[...]), 0.0)
    tile = jnp.where(lane < LANE_LF, g,
                     jnp.where(lane < LANE_DTA, lsg,
                               jnp.where(lane < LANE_DT, sp * a_row,
                                         jnp.where(lane < LANE_END, sp, 0.0))))
    if valid is None:
        return tile
    return jnp.where(valid, tile, jnp.where(lane < LANE_LF, -jnp.inf, 0.0))


def _cumsum_rows(x, period):
    pos = lax.broadcasted_iota(jnp.int32, x.shape, 0) & (period - 1)
    k = 1
    while k < period:
        x = x + jnp.where(pos >= k, pltpu.roll(x, k, 0), 0.0)
        k *= 2
    return x


def _per_row(values, sub):
    if len(values) == 1:
        return values[0]
    return jnp.concatenate([jnp.broadcast_to(x, (sub, x.shape[1])) for x in values], axis=0)


def _chunk_recur(q, k, v, gt, xs, bm, cm, states, sub, fillers=None):
    staged = fillers is not None
    fillers = list(fillers or ())

    def fill():
        if fillers:
            fillers.pop(0)()

    def run(vector_stage, matmul_stage, items):
        if staged:
            ctxs = [vector_stage(i) for i in items]
            return [matmul_stage(c) for c in ctxs]
        return [matmul_stage(vector_stage(i)) for i in items]

    nsub = len(states)
    assert nsub * sub == CHUNK == q.shape[0]
    sub_rows = [slice(i * sub, (i + 1) * sub) for i in range(nsub)]
    lane = lax.broadcasted_iota(jnp.int32, (1, LANES), 1)
    cum_lanes = (lane >= LANE_LF) & (lane < LANE_DT)
    gc = jnp.where(cum_lanes, _cumsum_rows(jnp.where(cum_lanes, gt, 0.0), sub), gt)
    gct = gc.T
    lasts = [gc[r.stop - 1:r.stop, :] for r in sub_rows]
    last_rows = _per_row(lasts, sub)

    def head_dots(h):
        qh = q[:, h * M_DK:(h + 1) * M_DK]
        c0s = [st_in[0][h] for st_in, _ in states]
        qc = [_dot(qh[r], c0.astype(BF16)) for r, c0 in zip(sub_rows, c0s)]
        return _dot_nt(qh, k[:, h * M_DK:(h + 1) * M_DK]), jnp.concatenate(qc, axis=0), c0s

    def group_dots(g):
        cg = cm[:, g * S_STATE:(g + 1) * S_STATE]
        s_olds = [st_in[3][g * S_GROUP_W:(g + 1) * S_GROUP_W, :] for st_in, _ in states]
        yi = [_dot_nt(cg[r], s_old.astype(BF16)) for r, s_old in zip(sub_rows, s_olds)]
        return (_dot_nt(cg, bm[:, g * S_STATE:(g + 1) * S_STATE]),
                jnp.concatenate(yi, axis=0), s_olds)

    if staged:
        head_pre = [head_dots(h) for h in range(M_HEADS)]
        group_pre = [group_dots(g) for g in range(S_GROUPS)]
        head_dots = head_pre.__getitem__
        group_dots = group_pre.__getitem__
    fill()

    ti = lax.broadcasted_iota(jnp.int32, (CHUNK, CHUNK), 0)
    si = lax.broadcasted_iota(jnp.int32, (CHUNK, CHUNK), 1)
    causal = si <= ti
    if nsub > 1:
        causal = causal & ((si & -sub) == (ti & -sub))
    sub_lanes = [(lane >= r.start) & (lane < r.stop) for r in sub_rows]
    m_rows = [st_in[2][...] for st_in, _ in states]

    def head_vector(h):
        qh = q[:, h * M_DK:(h + 1) * M_DK]
        bt_c = gc[:, LANE_LF + h:LANE_LF + h + 1]
        ig_c = gc[:, LANE_IG + h:LANE_IG + h + 1]
        bt_r = gct[LANE_LF + h:LANE_LF + h + 1, :]
        ig_r = gct[LANE_IG + h:LANE_IG + h + 1, :]
        m0s = [m_row[:, h:h + 1] for m_row in m_rows]
        n0s = [st_in[1][h:h + 1, :] for st_in, _ in states]
        m0 = _per_row(m0s, sub)
        sqk, qc, c0s = head_dots(h)

        dmat = jnp.where(causal, bt_c - bt_r + ig_r, -jnp.inf)
        inter = bt_c + m0
        m_t = jnp.maximum(inter, jnp.max(dmat, axis=1, keepdims=True))
        w_intra = jnp.exp(dmat - m_t)
        w_inter = jnp.exp(inter - m_t)
        s = sqk * w_intra
        den = (jnp.sum(s, axis=1, keepdims=True)
               + jnp.sum(qh.astype(F32) * _per_row(n0s, sub), axis=1, keepdims=True) * w_inter)
        rdenom = 1.0 / jnp.maximum(jnp.abs(den), jnp.exp(-m_t))

        updates = []
        for i, r in enumerate(sub_rows):
            b_last = lasts[i][:, LANE_LF + h:LANE_LF + h + 1]
            d_last = b_last - bt_r + ig_r
            if nsub > 1:
                d_last = jnp.where(sub_lanes[i], d_last, -jnp.inf)
            m_new = jnp.maximum(b_last + m0s[i], jnp.max(d_last, axis=1, keepdims=True))
            w_last = jnp.exp(b_last - bt_c[r] + ig_c[r] - m_new)
            decay = jnp.exp(b_last + m0s[i] - m_new)
            kw = k[r, h * M_DK:(h + 1) * M_DK].astype(F32) * w_last
            states[i][1][1][h:h + 1, :] = decay * n0s[i] + jnp.sum(kw, axis=0, keepdims=True)
            updates.append((kw.astype(BF16), decay * c0s[i], m_new))
        return h, s.astype(BF16), qc * w_inter, rdenom, updates

    def head_matmul(ctx):
        h, s_b, inter_part, rdenom, updates = ctx
        num = _dot(s_b, v[:, h * M_DV:(h + 1) * M_DV]) + inter_part
        for i, (kw_b, c_decayed, _) in enumerate(updates):
            states[i][1][0][h] = c_decayed + _dot_tn(kw_b, v[sub_rows[i], h * M_DV:(h + 1) * M_DV])
        return num * rdenom, [m_new for _, _, m_new in updates]

    heads = run(head_vector, head_matmul, range(M_HEADS))
    for i in range(nsub):
        m_new_row = m_rows[i]
        for h, (_, m_news) in enumerate(heads):
            m_new_row = jnp.where(lane == h, m_news[i], m_new_row)
        states[i][1][2][...] = m_new_row
    fill()

    lo_half = lax.broadcasted_iota(jnp.int32, (CHUNK, LANES), 1) < S_HEADDIM
    xs_b = xs.astype(BF16)

    def group_vector(g):
        rows_g = slice(g * S_GROUP_W, (g + 1) * S_GROUP_W)
        cb, y_inter, s_olds = group_dots(g)
        pairs, wd_parts = [], []
        dec_parts = [[] for _ in range(nsub)]
        for p in range(S_HEADS_PER_GROUP // 2):
            ms, es, wds = [], [], []
            for r in (2 * p, 2 * p + 1):
                hh = g * S_HEADS_PER_GROUP + r
                bt_c = gc[:, LANE_DTA + hh:LANE_DTA + hh + 1]
                dt_c = gc[:, LANE_DT + hh:LANE_DT + hh + 1]
                bt_r = gct[LANE_DTA + hh:LANE_DTA + hh + 1, :]
                dt_r = gct[LANE_DT + hh:LANE_DT + hh + 1, :]
                b_last = last_rows[:, LANE_DTA + hh:LANE_DTA + hh + 1]
                dec = jnp.exp(jnp.where(causal, bt_c - bt_r, -jnp.inf))
                ms.append((cb * dec * dt_r).astype(BF16))
                es.append(jnp.exp(bt_c))
                wds.append(jnp.exp(b_last - bt_c) * dt_c)
                for i in range(nsub):
                    dec_parts[i].append(jnp.broadcast_to(
                        jnp.exp(lasts[i][:, LANE_DTA + hh:LANE_DTA + hh + 1]), (S_HEADDIM, 1)))
            col = g * S_GROUP_W + p * LANES
            xp = xs_b[:, col:col + LANES]
            zero = jnp.zeros_like(xp)
            rhs = jnp.concatenate([jnp.where(lo_half, xp, zero), jnp.where(lo_half, zero, xp)], axis=0)
            pairs.append((jnp.concatenate(ms, axis=1), rhs,
                          y_inter[:, p * LANES:(p + 1) * LANES] * jnp.where(lo_half, es[0], es[1])))
            wd_parts.append(jnp.where(lo_half, wds[0], wds[1]))
        a_mat = (xs[:, rows_g] * jnp.concatenate(wd_parts, axis=1)).astype(BF16)
        return g, pairs, a_mat, [jnp.concatenate(d, axis=0) * s_old for d, s_old in zip(dec_parts, s_olds)]

    def group_matmul(ctx):
        g, pairs, a_mat, s_decayed = ctx
        ys = [_dot(lhs, rhs) + inter_part for lhs, rhs, inter_part in pairs]
        for i, r in enumerate(sub_rows):
            states[i][1][3][g * S_GROUP_W:(g + 1) * S_GROUP_W, :] = (
                s_decayed[i] + _dot_tn(a_mat[r], bm[r, g * S_STATE:(g + 1) * S_STATE]))
        return jnp.concatenate(ys, axis=1)

    groups = run(group_vector, group_matmul, range(S_GROUPS))
    return jnp.concatenate([hp for hp, _ in heads], axis=1), jnp.concatenate(groups, axis=1)


def _post_math(h, y, xs, so, sz, sg, xn, mg_ref, sng_ref, dexp_ref, wpa_ref, wpb_ref, wout_ref,
               g1_ref, b1_ref):
    hs = []
    for i in range(M_HEADS):
        hh = h[:, i * M_DV:(i + 1) * M_DV]
        mu = jnp.mean(hh, axis=-1, keepdims=True)
        hc = hh - mu
        var = jnp.mean(hc * hc, axis=-1, keepdims=True)
        hs.append(hc * lax.rsqrt(var + LN_EPS))
    hn = jnp.concatenate(hs, axis=1) * mg_ref[...]
    ya = _wdot((so * hn).astype(BF16), wpa_ref)

    y = (y + dexp_ref[...] * xs) * sz
    ys = []
    for g in range(S_GROUPS):
        yg = y[:, g * S_GROUP_W:(g + 1) * S_GROUP_W]
        ys.append(yg * lax.rsqrt(jnp.mean(yg * yg, axis=-1, keepdims=True) + RMS_EPS))
    yn = jnp.concatenate(ys, axis=1) * sng_ref[...]
    yb = _wdot(yn.astype(BF16), wpb_ref)

    mixed = _wdot((sg[:, :D_MODEL] * ya + sg[:, D_MODEL:] * yb).astype(BF16), wout_ref)
    return _layer_norm(ALPHA * xn + mixed, g1_ref[...], b1_ref[...])


def _const_spec(shape):
    nd = len(shape)
    return pl.BlockSpec(shape, lambda *_: (0,) * nd, pipeline_mode=pl.Buffered(1))


def _pack_body(w_ref, o_ref):
    o_ref[...] = pltpu.bitcast(w_ref[...].astype(BF16), jnp.uint32)


def _pack_weight(w, name):
    k, n = w.shape
    bn = min(n, PACK_COLS)
    return pl.pallas_call(
        _pack_body, grid=(n // bn,), in_specs=[pl.BlockSpec((k, bn), lambda j: (0, j))],
        out_specs=pl.BlockSpec((k // 2, bn), lambda j: (0, j)),
        out_shape=jax.ShapeDtypeStruct((k // 2, n), jnp.uint32), name=f"pack_{name}",
        compiler_params=_params(1),
    )(w)


def _pack_t_body(wt_ref, o_ref):
    o_ref[...] = pltpu.bitcast(wt_ref[...].T.astype(BF16), jnp.uint32)


def _pack_gates_body(wif_ref, wdt_ref, o_ref):
    wdt = wdt_ref[...]
    rows = jnp.concatenate([wif_ref[...], wdt, wdt, jnp.zeros((LANES - LANE_END, D_MODEL), F32)], axis=0)
    o_ref[...] = pltpu.bitcast(rows.T.astype(BF16), jnp.uint32)


def _pack_in(wt):
    k = wt.shape[1]
    assert wt.shape[0] == D_IN
    starts = (0, IN_Z - IN_I, IN_GATE - IN_DT + IN_Z - IN_I)
    lens = (IN_I, IN_DT - IN_Z, D_IN - IN_GATE)
    assert all(s % SUBLANES == 0 for s in starts) and all(n % PACK_COLS == 0 for n in lens)
    b1 = lens[0] // PACK_COLS
    b2 = b1 + lens[1] // PACK_COLS

    def src_row(j):
        s0, s1, s2 = (s // SUBLANES for s in starts)
        return (j * (PACK_COLS // SUBLANES) + jnp.where(j < b1, s0, jnp.where(j < b2, s1, s2))) * SUBLANES

    big = pl.pallas_call(
        _pack_t_body, grid=(W_BIG_COLS // PACK_COLS,),
        in_specs=[pl.BlockSpec((pl.Element(PACK_COLS), pl.Element(k)), lambda j: (src_row(j), 0))],
        out_specs=pl.BlockSpec((k // 2, PACK_COLS), lambda j: (0, j)),
        out_shape=jax.ShapeDtypeStruct((k // 2, W_BIG_COLS), jnp.uint32),
        name="pack_in", compiler_params=_params(1),
    )(wt)
    small = pl.pallas_call(
        _pack_gates_body, out_shape=jax.ShapeDtypeStruct((k // 2, LANES), jnp.uint32), name="pack_gates",
    )(wt[IN_I:IN_Z], wt[IN_DT:IN_GATE])
    return big, small


def _params(dims):
    return pltpu.CompilerParams(dimension_semantics=("arbitrary",) * dims,
                                vmem_limit_bytes=VMEM_LIMIT_BYTES)


def _mixer_body(rows, n_valid,
                x_ref, g0_ref, b0_ref, wbig_ref, wsm_ref, bsm_ref, alog_ref,
                wmc_ref, bmc_ref, wsc_ref, bsc_ref, cin_qk_ref, cin_xbc_ref,
                c0_ref, n0_ref, m0_ref, s0_ref,
                mg_ref, sng_ref, dexp_ref, wpa_ref, wpb_ref, wout_ref, g1_ref, b1_ref,
                x1_ref, cout_qk_ref, cout_xbc_ref, c1_ref, n1_ref, m1_ref, s1_ref):
    @pl.when(pl.program_id(1) == 0)
    def _():
        cout_qk_ref[...] = cin_qk_ref[...]
        cout_xbc_ref[...] = cin_xbc_ref[...]
        c1_ref[...] = c0_ref[...]
        n1_ref[...] = n0_ref[...]
        m1_ref[...] = m0_ref[...]
        s1_ref[...] = s0_ref[...]

    xn = _layer_norm(x_ref[...], g0_ref[...], b0_ref[...])
    xb = xn.astype(BF16)
    valid = None
    if n_valid < rows:
        valid = lax.broadcasted_iota(jnp.int32, (rows, 1), 0) < n_valid
    gt = _gate_tile(xb, valid, wsm_ref, bsm_ref, alog_ref)

    u_qk = _wdot(xb, wbig_ref, OFF_QK, OFF_QK + 2 * M_QK)
    qk = _silu(_conv_carry(u_qk, cout_qk_ref, wmc_ref, bmc_ref, M_CONV, n_valid))
    q = (qk[:, :M_QK] * (M_DK ** -0.5)).astype(BF16)
    k = qk[:, M_QK:].astype(BF16)
    v = _wdot(xb, wbig_ref, OFF_V, OFF_V + M_V).astype(BF16)
    u_xbc = _wdot(xb, wbig_ref, OFF_XBC, OFF_XBC + S_XBC)
    xbc = _silu(_conv_carry(u_xbc, cout_xbc_ref, wsc_ref, bsc_ref, S_CONV, n_valid))
    xs = xbc[:, :S_INNER]
    bm = xbc[:, S_INNER:S_INNER + S_BC].astype(BF16)
    cm = xbc[:, S_INNER + S_BC:].astype(BF16)

    offs = [OFF_O, OFF_Z, OFF_Z + D_MODEL, OFF_GATE, OFF_GATE + D_MODEL]
    dense = {}
    nchunk = rows // CHUNK
    nslot = nchunk * FILLER_SLOTS
    hs, ys = [], []
    for ci in range(nchunk):
        sl = slice(ci * CHUNK, (ci + 1) * CHUNK)
        state = (c1_ref, n1_ref, m1_ref, s1_ref)

        def filler(slot):
            def issue():
                for off in offs[slot::nslot]:
                    dense[off] = _wdot(xb, wbig_ref, off, off + D_MODEL)
            return issue

        h, y = _chunk_recur(q[sl], k[sl], v[sl], gt[sl], xs[sl], bm[sl], cm[sl], [(state, state)], CHUNK,
                            [filler(ci * FILLER_SLOTS + i) for i in range(FILLER_SLOTS)])
        hs.append(h)
        ys.append(y)
    h = jnp.concatenate(hs, axis=0)
    y = jnp.concatenate(ys, axis=0)

    so = jax.nn.sigmoid(dense[OFF_O])
    sz = _silu(jnp.concatenate([dense[OFF_Z], dense[OFF_Z + D_MODEL]], axis=1))
    sg = jax.nn.sigmoid(jnp.concatenate([dense[OFF_GATE], dense[OFF_GATE + D_MODEL]], axis=1))
    x1_ref[...] = _post_math(h, y, xs, so, sz, sg, xn, mg_ref, sng_ref, dexp_ref,
                             wpa_ref, wpb_ref, wout_ref, g1_ref, b1_ref)


def _mixer(x_rows, cin_qk, cin_xbc, states, wts, *, nseq, rows, n_valid):
    total = x_rows.shape[0]
    nt = total // (nseq * rows)
    rspec = pl.BlockSpec((rows, D_MODEL), lambda b, c: (b * nt + c, 0))

    def per_seq(shape):
        nd = len(shape)
        return pl.BlockSpec((None,) + shape, lambda b, c: (b,) + (0,) * nd)

    def shared(shape):
        nd = len(shape)
        return pl.BlockSpec((None,) + shape, lambda b, c: (0,) * (nd + 1), pipeline_mode=pl.Buffered(1))

    tails = [(SUBLANES, 2 * M_QK), (SUBLANES, S_XBC)]
    in_specs = ([rspec, _const_spec((1, D_MODEL)), _const_spec((1, D_MODEL)),
                 _const_spec((D_MODEL // 2, W_BIG_COLS)), _const_spec((D_MODEL // 2, LANES)),
                 _const_spec((1, LANES)), _const_spec((1, LANES)),
                 _const_spec((M_CONV, 2 * M_QK)), _const_spec((1, 2 * M_QK)),
                 _const_spec((S_CONV, S_XBC)), _const_spec((1, S_XBC))]
                + [shared(s) for s in tails + STATE_SHAPES]
                + [_const_spec((1, M_V)), _const_spec((1, S_INNER)), _const_spec((1, S_INNER)),
                   _const_spec((M_V // 2, D_MODEL)), _const_spec((S_INNER // 2, D_MODEL)),
                   _const_spec((D_MODEL // 2, D_MODEL)), _const_spec((1, D_MODEL)), _const_spec((1, D_MODEL))])
    out_specs = [rspec] + [per_seq(s) for s in tails + STATE_SHAPES]
    out_shapes = ([jax.ShapeDtypeStruct((total, D_MODEL), F32)]
                  + [jax.ShapeDtypeStruct((nseq,) + s, F32) for s in tails + STATE_SHAPES])
    outs = pl.pallas_call(
        functools.partial(_mixer_body, rows, n_valid),
        grid=(nseq, nt), in_specs=in_specs, out_specs=out_specs, out_shape=out_shapes,
        name=f"mixer_{nseq}", compiler_params=_params(2),
    )(x_rows, wts["ln0_g"], wts["ln0_b"], wts["w_big"], wts["w_small"], wts["b_small"], wts["a_log"],
      wts["w_mconv"], wts["b_mconv"], wts["w_sconv"], wts["b_sconv"], cin_qk, cin_xbc, *states,
      wts["mnorm_g"], wts["snorm_g"], wts["d_exp"], wts["w_proj_a"], wts["w_proj_b"], wts["w_out"],
      wts["ln1_g"], wts["ln1_b"])
    return outs[0], outs[1:3], outs[3:]


def _in_proj_body(x_ref, g0_ref, b0_ref, wbig_ref, wsm_ref, bsm_ref, alog_ref,
                  wmc_ref, bmc_ref, wsc_ref, bsc_ref, st_qk_ref, st_xbc_ref,
                  q_ref, k_ref, v_ref, gate_ref, xs_ref, bm_ref, cm_ref,
                  xn_ref, xs_tm_ref, so_ref, sz_ref, sg_ref, new_qk_ref, new_xbc_ref):
    xn = _layer_norm(_time_major(x_ref), g0_ref[...], b0_ref[...])
    xn_ref[...] = xn
    xb = xn.astype(BF16)
    lane = lax.broadcasted_iota(jnp.int32, (1, LANES), 1)
    _store_seq_major(gate_ref, _gate_tile(xb, None, wsm_ref, bsm_ref, alog_ref),
                     jnp.where(lane < LANE_LF, -jnp.inf, 0.0))

    u_qk = _wdot(xb, wbig_ref, OFF_QK, OFF_QK + 2 * M_QK)
    qk = _silu(_conv_tm(u_qk, st_qk_ref, new_qk_ref, wmc_ref, bmc_ref, M_CONV, 0))
    _store_seq_major(q_ref, qk[:, :M_QK] * (M_DK ** -0.5))
    _store_seq_major(k_ref, qk[:, M_QK:])
    _store_seq_major(v_ref, _wdot(xb, wbig_ref, OFF_V, OFF_V + M_V))
    so_ref[...] = jax.nn.sigmoid(_wdot(xb, wbig_ref, OFF_O, OFF_O + M_V))
    sz_ref[...] = _silu(_wdot(xb, wbig_ref, OFF_Z, OFF_Z + S_INNER))
    u_xbc = _wdot(xb, wbig_ref, OFF_XBC, OFF_XBC + S_XBC)
    xbc = _silu(_conv_tm(u_xbc, st_xbc_ref, new_xbc_ref, wsc_ref, bsc_ref, S_CONV, 0))
    xs_tm_ref[...] = xbc[:, :S_INNER]
    _store_seq_major(xs_ref, xbc[:, :S_INNER])
    _store_seq_major(bm_ref, xbc[:, S_INNER:S_INNER + S_BC])
    _store_seq_major(cm_ref, xbc[:, S_INNER + S_BC:])
    sg_ref[...] = jax.nn.sigmoid(_wdot(xb, wbig_ref, OFF_GATE, OFF_GATE + 2 * D_MODEL))


def _in_proj(x, st_qk, st_xbc, wts):
    nseq = x.shape[0]
    nb = SAMPLE_BLOCK

    def rows_first(width, cols):
        return pl.BlockSpec((width - 1, nb, cols), lambda i: (0, i, 0))

    def sspec(rows, cols):
        return pl.BlockSpec((nb, rows, cols), lambda i: (i, 0, 0))

    def tspec(cols):
        return pl.BlockSpec((nb * DEC_SEQ, cols), lambda i: (i, 0))

    in_specs = [sspec(DEC_SEQ, D_MODEL), _const_spec((1, D_MODEL)), _const_spec((1, D_MODEL)),
                _const_spec((D_MODEL // 2, W_BIG_COLS)), _const_spec((D_MODEL // 2, LANES)),
                _const_spec((1, LANES)), _const_spec((1, LANES)),
                _const_spec((M_CONV, 2 * M_QK)), _const_spec((1, 2 * M_QK)),
                _const_spec((S_CONV, S_XBC)), _const_spec((1, S_XBC)),
                rows_first(M_CONV, 2 * M_QK), rows_first(S_CONV, S_XBC)]
    seq_cols = [M_QK, M_QK, M_V, LANES, S_INNER, S_BC, S_BC]
    tm_cols = [D_MODEL, S_INNER, M_V, S_INNER, 2 * D_MODEL]
    outs = pl.pallas_call(
        _in_proj_body, grid=(nseq // nb,), in_specs=in_specs,
        out_specs=([sspec(SEQ_PAD, c) for c in seq_cols] + [tspec(c) for c in tm_cols]
                   + [rows_first(M_CONV, 2 * M_QK), rows_first(S_CONV, S_XBC)]),
        out_shape=([jax.ShapeDtypeStruct((nseq, SEQ_PAD, c), F32) for c in seq_cols]
                   + [jax.ShapeDtypeStruct((nseq * DEC_SEQ, c), F32) for c in tm_cols]
                   + [jax.ShapeDtypeStruct((M_CONV - 1, nseq, 2 * M_QK), F32),
                      jax.ShapeDtypeStruct((S_CONV - 1, nseq, S_XBC), F32)]),
        name="sample_in_proj", compiler_params=_params(1),
    )(x, wts["ln0_g"], wts["ln0_b"], wts["w_big"], wts["w_small"], wts["b_small"], wts["a_log"],
      wts["w_mconv"], wts["b_mconv"], wts["w_sconv"], wts["b_sconv"], st_qk, st_xbc)
    return outs[:7], outs[7:12], outs[12:]


def _sample_recur_body(q_ref, k_ref, v_ref, gate_ref, xs_ref, bm_ref, cm_ref,
                       c0_ref, n0_ref, m0_ref, s0_ref,
                       h_ref, y_ref, c1_ref, n1_ref, m1_ref, s1_ref):
    nb = SAMPLE_SEQS_PER_STEP
    lane = lax.broadcasted_iota(jnp.int32, (1, LANES), 1)

    def stack(ref, dtype, fill=None):
        x = ref[...]
        c = x.shape[2]
        pad = jnp.zeros((1, 1, c), F32) if fill is None else fill[None]
        x = jnp.concatenate([x, jnp.broadcast_to(pad, (nb, SAMPLE_SUB - SEQ_PAD, c))], axis=1)
        return x.reshape(nb * SAMPLE_SUB, c).astype(dtype)

    def unstack(x):
        return x.reshape(nb, SAMPLE_SUB, x.shape[1])[:, :SEQ_PAD, :]

    states = [((c0_ref.at[i], n0_ref.at[i], m0_ref.at[i], s0_ref.at[i]),
               (c1_ref.at[i], n1_ref.at[i], m1_ref.at[i], s1_ref.at[i])) for i in range(nb)]
    h, y = _chunk_recur(stack(q_ref, BF16), stack(k_ref, BF16), stack(v_ref, BF16),
                        stack(gate_ref, F32, jnp.where(lane < LANE_LF, -jnp.inf, 0.0)),
                        stack(xs_ref, F32), stack(bm_ref, BF16), stack(cm_ref, BF16), states, SAMPLE_SUB)
    h_ref[...] = unstack(h)
    y_ref[...] = unstack(y)


def _sample_recur(acts, states, *, nseq):
    q, k, v, gate, xs, bm, cm = acts
    nb = SAMPLE_SEQS_PER_STEP
    assert nb * SAMPLE_SUB == CHUNK

    def rspec(cols):
        return pl.BlockSpec((nb, SEQ_PAD, cols), lambda i: (i, 0, 0))

    def sspec(shape):
        nd = len(shape)
        return pl.BlockSpec((nb,) + shape, lambda i: (i,) + (0,) * nd)

    in_specs = ([rspec(M_QK), rspec(M_QK), rspec(M_V), rspec(LANES), rspec(S_INNER), rspec(S_BC), rspec(S_BC)]
                + [sspec(s) for s in STATE_SHAPES])
    out_specs = [rspec(M_V), rspec(S_INNER)] + [sspec(s) for s in STATE_SHAPES]
    out_shapes = ([jax.ShapeDtypeStruct((nseq, SEQ_PAD, M_V), F32),
                   jax.ShapeDtypeStruct((nseq, SEQ_PAD, S_INNER), F32)]
                  + [jax.ShapeDtypeStruct((nseq,) + s, F32) for s in STATE_SHAPES])
    outs = pl.pallas_call(
        _sample_recur_body, grid=(nseq // nb,), in_specs=in_specs, out_specs=out_specs,
        out_shape=out_shapes, name="sample_recur", compiler_params=_params(1),
    )(q, k, v, gate, xs, bm, cm, *states)
    return outs[0], outs[1], outs[2:]


def _post_body(h_ref, y_ref, xs_ref, so_ref, sz_ref, sg_ref, xn_ref,
               mg_ref, sng_ref, dexp_ref, wpa_ref, wpb_ref, wout_ref, g1_ref, b1_ref, x1_ref):
    h = jnp.concatenate([h_ref[:, t, :] for t in range(DEC_SEQ)], axis=0)
    y = jnp.concatenate([y_ref[:, t, :] for t in range(DEC_SEQ)], axis=0)
    x1_ref[...] = _post_math(h, y, xs_ref[...], so_ref[...], sz_ref[...], sg_ref[...],
                             xn_ref[...], mg_ref, sng_ref, dexp_ref, wpa_ref, wpb_ref, wout_ref,
                             g1_ref, b1_ref)


def _post(h, y, xs, so, sz, sg, xn, wts):
    nseq = h.shape[0]
    nb = SAMPLE_BLOCK

    def sspec(cols):
        return pl.BlockSpec((nb, SEQ_PAD, cols), lambda i: (i, 0, 0))

    def tspec(cols):
        return pl.BlockSpec((nb * DEC_SEQ, cols), lambda i: (i, 0))

    in_specs = [sspec(M_V), sspec(S_INNER), tspec(S_INNER), tspec(M_V), tspec(S_INNER),
                tspec(2 * D_MODEL), tspec(D_MODEL),
                _const_spec((1, M_V)), _const_spec((1, S_INNER)), _const_spec((1, S_INNER)),
                _const_spec((M_V // 2, D_MODEL)), _const_spec((S_INNER // 2, D_MODEL)),
                _const_spec((D_MODEL // 2, D_MODEL)), _const_spec((1, D_MODEL)), _const_spec((1, D_MODEL))]
    return pl.pallas_call(
        _post_body, grid=(nseq // nb,), in_specs=in_specs, out_specs=tspec(D_MODEL),
        out_shape=jax.ShapeDtypeStruct((nseq * DEC_SEQ, D_MODEL), F32), name="sample_post",
        compiler_params=_params(1),
    )(h, y, xs, so, sz, sg, xn, wts["mnorm_g"], wts["snorm_g"], wts["d_exp"],
      wts["w_proj_a"], wts["w_proj_b"], wts["w_out"], wts["ln1_g"], wts["ln1_b"])


def _ffn_body(mode, rows, n_valid,
              x1_ref, wup_ref, wfc_ref, bfc_ref, wdn_ref, g2_ref, b2_ref, cin_ref,
              y_ref, cout_ref):
    x1 = x1_ref[...]
    x1b = x1.astype(BF16)
    if mode == "seq":
        @pl.when(pl.program_id(1) == 0)
        def _():
            cout_ref[...] = cin_ref[...]

    def conv(u, cols):
        if mode == "tm":
            return _conv_tm(u, cin_ref, cout_ref, wfc_ref, bfc_ref, F_CONV, 1, cols)
        return _conv_carry(u, cout_ref, wfc_ref, bfc_ref, F_CONV, n_valid, cols)

    def up_proj(j):
        lo = j * FFN_BLOCK
        return (_wdot(x1b, wup_ref, lo, lo + FFN_BLOCK),
                _wdot(x1b, wup_ref, D_FF + lo, D_FF + lo + FFN_BLOCK))

    nblk = D_FF // FFN_BLOCK
    ff = None
    ups = [up_proj(j) for j in range(min(FFN_AHEAD, nblk))]
    for j in range(nblk):
        lo = j * FFN_BLOCK
        ua, ub = ups[j]
        if j + FFN_AHEAD < nblk:
            ups.append(up_proj(j + FFN_AHEAD))
        va = conv(ua, slice(lo, lo + FFN_BLOCK))
        vb = conv(ub, slice(D_FF + lo, D_FF + lo + FFN_BLOCK))
        act = (_silu(va) * vb).astype(BF16)
        part = _dot(act, pltpu.bitcast(wdn_ref[lo // 2:(lo + FFN_BLOCK) // 2, :], BF16))
        ff = part if ff is None else ff + part
    y = _layer_norm(ALPHA * x1 + ff, g2_ref[...], b2_ref[...])
    if mode == "tm":
        nseq = y_ref.shape[0]
        for t in range(DEC_SEQ):
            y_ref[:, t, :] = y[t * nseq:(t + 1) * nseq]
    else:
        y_ref[...] = y


def _ffn(x1, cin, wts, *, mode, nseq, rows, n_valid):
    total = x1.shape[0]
    nt = total // (nseq * rows)

    def rspec(cols):
        return pl.BlockSpec((rows, cols), lambda b, c: (b * nt + c, 0))

    if mode == "seq":
        cin_spec = pl.BlockSpec((None, SUBLANES, 2 * D_FF), lambda b, c: (0, 0, 0))
        cout_spec = pl.BlockSpec((None, SUBLANES, 2 * D_FF), lambda b, c: (b, 0, 0))
        cout_shape = jax.ShapeDtypeStruct((nseq, SUBLANES, 2 * D_FF), F32)
        y_spec = rspec(D_MODEL)
        y_shape = jax.ShapeDtypeStruct((total, D_MODEL), F32)
    else:
        nb = rows // DEC_SEQ
        sspec = lambda r, cols: pl.BlockSpec((nb, r, cols), lambda b, c: (b * nt + c, 0, 0))
        cin_spec = sspec(F_CONV - 1, 2 * D_FF)
        cout_spec = sspec(F_CONV - 1, 2 * D_FF)
        cout_shape = jax.ShapeDtypeStruct((total // DEC_SEQ, F_CONV - 1, 2 * D_FF), F32)
        y_spec = sspec(DEC_SEQ, D_MODEL)
        y_shape = jax.ShapeDtypeStruct((total // DEC_SEQ, DEC_SEQ, D_MODEL), F32)
    in_specs = [rspec(D_MODEL), _const_spec((D_MODEL // 2, 2 * D_FF)), _const_spec((F_CONV, 2 * D_FF)),
                _const_spec((1, 2 * D_FF)), _const_spec((D_FF // 2, D_MODEL)),
                _const_spec((1, D_MODEL)), _const_spec((1, D_MODEL)), cin_spec]
    return pl.pallas_call(
        functools.partial(_ffn_body, mode, rows, n_valid),
        grid=(nseq, nt), in_specs=in_specs, out_specs=[y_spec, cout_spec],
        out_shape=[y_shape, cout_shape],
        name=f"ffn_{mode}_{nseq}", compiler_params=_params(2),
    )(x1, wts["w_up"], wts["w_fconv"], wts["b_fconv"], wts["w_down"], wts["ln2_g"], wts["ln2_b"], cin)


def kernel(x_prompt, x_sample, state_mlstm_conv, state_mlstm_C, state_mlstm_n, state_mlstm_m, state_ssm_conv, state_ssm, state_ffn_conv, meta_tokens, ln0_g, ln0_b, w_in, b_mlstm_if, w_mlstm_conv, b_mlstm_conv, mlstm_norm_g, w_proj_a, w_ssm_conv, b_ssm_conv, ssm_dt_bias, ssm_A_log, ssm_D, ssm_norm_g, w_proj_b, w_out, ln1_g, ln1_b, w_up, w_ffn_conv, b_ffn_conv, w_down, ln2_g, ln2_b):
    batch, seq, _ = x_prompt.shape
    dec_batch, dec_seq, _ = x_sample.shape
    prompt_tile = 2 * CHUNK
    ffn_tile = 2 * CHUNK
    assert dec_seq == DEC_SEQ and seq % ffn_tile == 0 and meta_tokens.shape[0] == N_META
    assert dec_batch % SAMPLE_SEQS_PER_STEP == 0 and dec_batch % SAMPLE_BLOCK == 0

    w_big, w_small = _pack_in(w_in[0].T)
    lane_pad = jnp.zeros((LANES - LANE_END,), F32)
    b_small = jnp.concatenate([b_mlstm_if[0], ssm_dt_bias[0], ssm_dt_bias[0], lane_pad])[None]
    a_log = jnp.concatenate([jnp.zeros((LANE_DTA,), F32), ssm_A_log[0],
                             jnp.zeros((LANES - LANE_DT,), F32)])[None]
    wts = {
        "ln0_g": ln0_g[None], "ln0_b": ln0_b[None], "w_big": w_big, "w_small": w_small,
        "b_small": b_small, "a_log": a_log,
        "w_mconv": w_mlstm_conv[0], "b_mconv": b_mlstm_conv, "w_sconv": w_ssm_conv[0], "b_sconv": b_ssm_conv,
        "mnorm_g": mlstm_norm_g, "snorm_g": ssm_norm_g, "d_exp": jnp.repeat(ssm_D[0], S_HEADDIM)[None],
        "w_proj_a": _pack_weight(w_proj_a[0], "proj_a"), "w_proj_b": _pack_weight(w_proj_b[0], "proj_b"),
        "w_out": _pack_weight(w_out[0], "out"), "ln1_g": ln1_g, "ln1_b": ln1_b,
        "w_up": _pack_weight(w_up[0], "up"), "w_fconv": w_ffn_conv[0], "b_fconv": b_ffn_conv,
        "w_down": _pack_weight(w_down[0], "down"), "ln2_g": ln2_g, "ln2_b": ln2_b,
    }

    x_meta = jnp.pad(meta_tokens, ((0, CHUNK - N_META), (0, 0)))
    zero_states = [jnp.zeros((1,) + s, F32) for s in STATE_SHAPES]
    x1_m, tails_m, states_m = _mixer(
        x_meta, jnp.zeros((1, SUBLANES, 2 * M_QK), F32), jnp.zeros((1, SUBLANES, S_XBC), F32),
        zero_states, wts, nseq=1, rows=CHUNK, n_valid=N_META)
    _, tail_ffn_m = _ffn(x1_m, jnp.zeros((1, SUBLANES, 2 * D_FF), F32), wts, mode="seq", nseq=1,
                         rows=CHUNK, n_valid=N_META)

    x1_p, tails_p, states_p = _mixer(
        x_prompt.reshape(batch * seq, D_MODEL), tails_m[0], tails_m[1], states_m, wts,
        nseq=batch, rows=prompt_tile, n_valid=prompt_tile)
    y_p, tail_ffn_p = _ffn(x1_p, tail_ffn_m, wts, mode="seq", nseq=batch, rows=ffn_tile, n_valid=ffn_tile)

    def rows_first(a):
        return jnp.swapaxes(a, 0, 1)

    acts, (xn, xs_tm, so, sz, sg), (new_qk, new_xbc) = _in_proj(
        x_sample, rows_first(state_mlstm_conv[0]), rows_first(state_ssm_conv[0]), wts)
    m_in = jnp.pad(state_mlstm_m[0], ((0, 0), (0, LANES - M_HEADS)))[:, None, :]
    h_s, yss_s, states_s = _sample_recur(
        acts, (state_mlstm_C[0], state_mlstm_n[0], m_in, state_ssm[0].reshape(dec_batch, S_INNER, S_STATE)),
        nseq=dec_batch)
    x1_s = _post(h_s, yss_s, xs_tm, so, sz, sg, xn, wts)
    y_s, new_ffn = _ffn(x1_s, state_ffn_conv[0], wts, mode="tm", nseq=1,
                        rows=SAMPLE_BLOCK * DEC_SEQ, n_valid=SAMPLE_BLOCK * DEC_SEQ)

    def pack_states(convs, states, n):
        c1, n1, m1, s1 = states
        return (convs[0][None], c1[None], n1[None], m1[:, 0, :M_HEADS][None], convs[1][None],
                s1.reshape(n, S_HEADS, S_HEADDIM, S_STATE)[None], convs[2][None])

    def tail_rows(t, width):
        return t[:, SUBLANES - (width - 1):, :]

    p_out = pack_states((tail_rows(tails_p[0], M_CONV), tail_rows(tails_p[1], S_CONV),
                         tail_rows(tail_ffn_p, F_CONV)), states_p, batch)
    s_out = pack_states((rows_first(new_qk), rows_first(new_xbc), new_ffn), states_s, dec_batch)
    y_prompt = y_p.reshape(batch, seq, D_MODEL)
    return (y_prompt, y_s) + p_out + s_out
```

```python
import functools

import jax
import jax.numpy as jnp
from jax import lax
from jax.experimental import pallas as pl
from jax.experimental.pallas import tpu as pltpu

F32 = jnp.float32
BF16 = jnp.bfloat16

D_MODEL = 1024
N_META = 16
M_HEADS = 4
M_DK = 128
M_DV = 256
M_QK = M_HEADS * M_DK
M_V = M_HEADS * M_DV
M_CONV = 4
S_INNER = 2048
S_HEADDIM = 64
S_HEADS = 32
S_GROUPS = 4
S_STATE = 128
S_CONV = 4
S_BC = S_GROUPS * S_STATE
S_XBC = S_INNER + 2 * S_BC
S_GROUP_W = S_INNER // S_GROUPS
S_HEADS_PER_GROUP = S_HEADS // S_GROUPS
D_FF = 2816
F_CONV = 3
ALPHA = 2.0 ** 0.25
LN_EPS = 1e-5
RMS_EPS = 1e-5

LANES = 128
SUBLANES = 8
CHUNK = 128
SEQ_PAD = SUBLANES
DEC_SEQ = 4
SAMPLE_SEQS_PER_STEP = 8
SAMPLE_SUB = 16
SAMPLE_BLOCK = 32
PACK_COLS = 512
FFN_BLOCK = 256
FFN_AHEAD = 3
FILLER_SLOTS = 2
VMEM_LIMIT_BYTES = 60 * 1024 * 1024

OFF_QK = 0
OFF_V = OFF_QK + 2 * M_QK
OFF_O = OFF_V + M_V
OFF_Z = OFF_O + M_V
OFF_XBC = OFF_Z + S_INNER
OFF_GATE = OFF_XBC + S_XBC
W_BIG_COLS = OFF_GATE + 2 * D_MODEL
LANE_IG = 0
LANE_LF = M_HEADS
LANE_DTA = 2 * M_HEADS
LANE_DT = LANE_DTA + S_HEADS
LANE_END = LANE_DT + S_HEADS

IN_I = 2 * M_QK + 2 * M_V
IN_Z = IN_I + 2 * M_HEADS
IN_DT = IN_Z + S_INNER + S_XBC
IN_GATE = IN_DT + S_HEADS
D_IN = IN_GATE + 2 * D_MODEL

STATE_SHAPES = [(M_HEADS, M_DK, M_DV), (M_HEADS, M_DK), (1, LANES), (S_INNER, S_STATE)]


def _softplus(x):
    return jnp.maximum(x, 0.0) + jnp.log1p(jnp.exp(-jnp.abs(x)))


def _silu(x):
    return x * jax.nn.sigmoid(x)


def _layer_norm(x, g, b):
    mu = jnp.mean(x, axis=-1, keepdims=True)
    xc = x - mu
    var = jnp.mean(xc * xc, axis=-1, keepdims=True)
    return xc * lax.rsqrt(var + LN_EPS) * g + b


def _dot(a, b):
    return jnp.dot(a, b, preferred_element_type=F32)


def _wdot(a, w_ref, lo=None, hi=None):
    w = w_ref[...] if lo is None else w_ref[:, lo:hi]
    return _dot(a, pltpu.bitcast(w, BF16))


def _dot_nt(a, b):
    return lax.dot_general(a, b, (((1,), (1,)), ((), ())), preferred_element_type=F32)


def _dot_tn(a, b):
    return lax.dot_general(a, b, (((0,), (0,)), ((), ())), preferred_element_type=F32)


def _conv_carry(u, carry_ref, w_ref, b_ref, width, n_valid, cols=slice(None)):
    full = jnp.concatenate([carry_ref[:, cols], u], axis=0)
    acc = b_ref[:, cols] + u * w_ref[width - 1:width, cols]
    for k in range(1, width):
        acc = acc + pltpu.roll(full, k, 0)[SUBLANES:] * w_ref[width - 1 - k:width - k, cols]
    carry_ref[:, cols] = u[n_valid - SUBLANES:n_valid]
    return acc


def _time_major(ref):
    return jnp.concatenate([ref[:, t, :] for t in range(DEC_SEQ)], axis=0)


def _store_seq_major(ref, x, fill=None):
    nseq, _, c = ref.shape
    pad = jnp.zeros((1, c), x.dtype) if fill is None else fill
    ref[:, DEC_SEQ:, :] = jnp.broadcast_to(pad[None], (nseq, SEQ_PAD - DEC_SEQ, c))
    for t in range(DEC_SEQ):
        ref[:, t, :] = x[t * nseq:(t + 1) * nseq]


def _conv_tm(u, st_ref, new_ref, w_ref, b_ref, width, row_axis, cols=slice(None)):
    def idx(j):
        return (j, slice(None), cols) if row_axis == 0 else (slice(None), j, cols)

    nseq = st_ref.shape[1 - row_axis]
    full = ([st_ref[idx(j)] for j in range(width - 1)]
            + [u[t * nseq:(t + 1) * nseq] for t in range(DEC_SEQ)])
    outs = []
    for t in range(DEC_SEQ):
        acc = b_ref[:, cols]
        for j in range(width):
            acc = acc + full[t + j] * w_ref[j:j + 1, cols]
        outs.append(acc)
    for j in range(width - 1):
        new_ref[idx(j)] = full[DEC_SEQ + j]
    return jnp.concatenate(outs, axis=0)


def _gate_tile(xb, valid, wsm_ref, bsm_ref, alog_ref):
    g = _wdot(xb, wsm_ref) + bsm_ref[...]
    lane = lax.broadcasted_iota(jnp.int32, (1, LANES), 1)
    sp = _softplus(g)
    lsg = -_softplus(-g)
    a_row = jnp.where((lane >= LANE_DTA) & (lane < LANE_DT), -jnp.exp(alog_ref[...]), 0.0)
    tile = jnp.where(lane < LANE_LF, g,
                     jnp.where(lane < LANE_DTA, lsg,
                               jnp.where(lane < LANE_DT, sp * a_row,
                                         jnp.where(lane < LANE_END, sp, 0.0))))
    if valid is None:
        return tile
    return jnp.where(valid, tile, jnp.where(lane < LANE_LF, -jnp.inf, 0.0))


def _cumsum_rows(x, period):
    pos = lax.broadcasted_iota(jnp.int32, x.shape, 0) & (period - 1)
    k = 1
    while k < period:
        x = x + jnp.where(pos >= k, pltpu.roll(x, k, 0), 0.0)
        k *= 2
    return x


def _per_row(values, sub):
    if len(values) == 1:
        return values[0]
    return jnp.concatenate([jnp.broadcast_to(x, (sub, x.shape[1])) for x in values], axis=0)


def _chunk_recur(q, k, v, gt, xs, bm, cm, states, sub, fillers=None):
    staged = fillers is not None
    fillers = list(fillers or ())

    def fill():
        if fillers:
            fillers.pop(0)()

    def run(vector_stage, matmul_stage, items):
        if staged:
            ctxs = [vector_stage(i) for i in items]
            return [matmul_stage(c) for c in ctxs]
        return [matmul_stage(vector_stage(i)) for i in items]

    nsub = len(states)
    assert nsub * sub == CHUNK == q.shape[0]
    sub_rows = [slice(i * sub, (i + 1) * sub) for i in range(nsub)]
    lane = lax.broadcasted_iota(jnp.int32, (1, LANES), 1)
    cum_lanes = (lane >= LANE_LF) & (lane < LANE_DT)
    gc = jnp.where(cum_lanes, _cumsum_rows(jnp.where(cum_lanes, gt, 0.0), sub), gt)
    gct = gc.T
    lasts = [gc[r.stop - 1:r.stop, :] for r in sub_rows]
    last_rows = _per_row(lasts, sub)

    def head_dots(h):
        qh = q[:, h * M_DK:(h + 1) * M_DK]
        c0s = [st_in[0][h] for st_in, _ in states]
        qc = [_dot(qh[r], c0.astype(BF16)) for r, c0 in zip(sub_rows, c0s)]
        return _dot_nt(qh, k[:, h * M_DK:(h + 1) * M_DK]), jnp.concatenate(qc, axis=0), c0s

    def group_dots(g):
        cg = cm[:, g * S_STATE:(g + 1) * S_STATE]
        s_olds = [st_in[3][g * S_GROUP_W:(g + 1) * S_GROUP_W, :] for st_in, _ in states]
        yi = [_dot_nt(cg[r], s_old.astype(BF16)) for r, s_old in zip(sub_rows, s_olds)]
        return (_dot_nt(cg, bm[:, g * S_STATE:(g + 1) * S_STATE]),
                jnp.concatenate(yi, axis=0), s_olds)

    if staged:
        head_pre = [head_dots(h) for h in range(M_HEADS)]
        group_pre = [group_dots(g) for g in range(S_GROUPS)]
        head_dots = head_pre.__getitem__
        group_dots = group_pre.__getitem__
    fill()

    ti = lax.broadcasted_iota(jnp.int32, (CHUNK, CHUNK), 0)
    si = lax.broadcasted_iota(jnp.int32, (CHUNK, CHUNK), 1)
    causal = si <= ti
    if nsub > 1:
        causal = causal & ((si & -sub) == (ti & -sub))
    sub_lanes = [(lane >= r.start) & (lane < r.stop) for r in sub_rows]
    m_rows = [st_in[2][...] for st_in, _ in states]

    def head_vector(h):
        qh = q[:, h * M_DK:(h + 1) * M_DK]
        bt_c = gc[:, LANE_LF + h:LANE_LF + h + 1]
        ig_c = gc[:, LANE_IG + h:LANE_IG + h + 1]
        bt_r = gct[LANE_LF + h:LANE_LF + h + 1, :]
        ig_r = gct[LANE_IG + h:LANE_IG + h + 1, :]
        m0s = [m_row[:, h:h + 1] for m_row in m_rows]
        n0s = [st_in[1][h:h + 1, :] for st_in, _ in states]
        m0 = _per_row(m0s, sub)
        sqk, qc, c0s = head_dots(h)

        dmat = jnp.where(causal, bt_c - bt_r + ig_r, -jnp.inf)
        inter = bt_c + m0
        m_t = jnp.maximum(inter, jnp.max(dmat, axis=1, keepdims=True))
        w_intra = jnp.exp(dmat - m_t)
        w_inter = jnp.exp(inter - m_t)
        s = sqk * w_intra
        den = (jnp.sum(s, axis=1, keepdims=True)
               + jnp.sum(qh.astype(F32) * _per_row(n0s, sub), axis=1, keepdims=True) * w_inter)
        rdenom = 1.0 / jnp.maximum(jnp.abs(den), jnp.exp(-m_t))

        updates = []
        for i, r in enumerate(sub_rows):
            b_last = lasts[i][:, LANE_LF + h:LANE_LF + h + 1]
            d_last = b_last - bt_r + ig_r
            if nsub > 1:
                d_last = jnp.where(sub_lanes[i], d_last, -jnp.inf)
            m_new = jnp.maximum(b_last + m0s[i], jnp.max(d_last, axis=1, keepdims=True))
            w_last = jnp.exp(b_last - bt_c[r] + ig_c[r] - m_new)
            decay = jnp.exp(b_last + m0s[i] - m_new)
            kw = k[r, h * M_DK:(h + 1) * M_DK].astype(F32) * w_last
            states[i][1][1][h:h + 1, :] = decay * n0s[i] + jnp.sum(kw, axis=0, keepdims=True)
            updates.append((kw.astype(BF16), decay * c0s[i], m_new))
        return h, s.astype(BF16), qc * w_inter, rdenom, updates

    def head_matmul(ctx):
        h, s_b, inter_part, rdenom, updates = ctx
        num = _dot(s_b, v[:, h * M_DV:(h + 1) * M_DV]) + inter_part
        for i, (kw_b, c_decayed, _) in enumerate(updates):
            states[i][1][0][h] = c_decayed + _dot_tn(kw_b, v[sub_rows[i], h * M_DV:(h + 1) * M_DV])
        return num * rdenom, [m_new for _, _, m_new in updates]

    heads = run(head_vector, head_matmul, range(M_HEADS))
    for i in range(nsub):
        m_new_row = m_rows[i]
        for h, (_, m_news) in enumerate(heads):
            m_new_row = jnp.where(lane == h, m_news[i], m_new_row)
        states[i][1][2][...] = m_new_row
    fill()

    lo_half = lax.broadcasted_iota(jnp.int32, (CHUNK, LANES), 1) < S_HEADDIM
    xs_b = xs.astype(BF16)

    def group_vector(g):
        rows_g = slice(g * S_GROUP_W, (g + 1) * S_GROUP_W)
        cb, y_inter, s_olds = group_dots(g)
        pairs, wd_parts = [], []
        dec_parts = [[] for _ in range(nsub)]
        for p in range(S_HEADS_PER_GROUP // 2):
            ms, es, wds = [], [], []
            for r in (2 * p, 2 * p + 1):
                hh = g * S_HEADS_PER_GROUP + r
                bt_c = gc[:, LANE_DTA + hh:LANE_DTA + hh + 1]
                dt_c = gc[:, LANE_DT + hh:LANE_DT + hh + 1]
                bt_r = gct[LANE_DTA + hh:LANE_DTA + hh + 1, :]
                dt_r = gct[LANE_DT + hh:LANE_DT + hh + 1, :]
                b_last = last_rows[:, LANE_DTA + hh:LANE_DTA + hh + 1]
                dec = jnp.exp(jnp.where(causal, bt_c - bt_r, -jnp.inf))
                ms.append((cb * dec * dt_r).astype(BF16))
                es.append(jnp.exp(bt_c))
                wds.append(jnp.exp(b_last - bt_c) * dt_c)
                for i in range(nsub):
                    dec_parts[i].append(jnp.broadcast_to(
                        jnp.exp(lasts[i][:, LANE_DTA + hh:LANE_DTA + hh + 1]), (S_HEADDIM, 1)))
            col = g * S_GROUP_W + p * LANES
            xp = xs_b[:, col:col + LANES]
            zero = jnp.zeros_like(xp)
            rhs = jnp.concatenate([jnp.where(lo_half, xp, zero), jnp.where(lo_half, zero, xp)], axis=0)
            pairs.append((jnp.concatenate(ms, axis=1), rhs,
                          y_inter[:, p * LANES:(p + 1) * LANES] * jnp.where(lo_half, es[0], es[1])))
            wd_parts.append(jnp.where(lo_half, wds[0], wds[1]))
        a_mat = (xs[:, rows_g] * jnp.concatenate(wd_parts, axis=1)).astype(BF16)
        return g, pairs, a_mat, [jnp.concatenate(d, axis=0) * s_old for d, s_old in zip(dec_parts, s_olds)]

    def group_matmul(ctx):
        g, pairs, a_mat, s_decayed = ctx
        ys = [_dot(lhs, rhs) + inter_part for lhs, rhs, inter_part in pairs]
        for i, r in enumerate(sub_rows):
            states[i][1][3][g * S_GROUP_W:(g + 1) * S_GROUP_W, :] = (
                s_decayed[i] + _dot_tn(a_mat[r], bm[r, g * S_STATE:(g + 1) * S_STATE]))
        return jnp.concatenate(ys, axis=1)

    groups = run(group_vector, group_matmul, range(S_GROUPS))
    return jnp.concatenate([hp for hp, _ in heads], axis=1), jnp.concatenate(groups, axis=1)


def _post_math(h, y, xs, so, sz, sg, xn, mg_ref, sng_ref, dexp_ref, wpa_ref, wpb_ref, wout_ref,
               g1_ref, b1_ref):
    hs = []
    for i in range(M_HEADS):
        hh = h[:, i * M_DV:(i + 1) * M_DV]
        mu = jnp.mean(hh, axis=-1, keepdims=True)
        hc = hh - mu
        var = jnp.mean(hc * hc, axis=-1, keepdims=True)
        hs.append(hc * lax.rsqrt(var + LN_EPS))
    hn = jnp.concatenate(hs, axis=1) * mg_ref[...]
    ya = _wdot((so * hn).astype(BF16), wpa_ref)

    y = (y + dexp_ref[...] * xs) * sz
    ys = []
    for g in range(S_GROUPS):
        yg = y[:, g * S_GROUP_W:(g + 1) * S_GROUP_W]
        ys.append(yg * lax.rsqrt(jnp.mean(yg * yg, axis=-1, keepdims=True) + RMS_EPS))
    yn = jnp.concatenate(ys, axis=1) * sng_ref[...]
    yb = _wdot(yn.astype(BF16), wpb_ref)

    mixed = _wdot((sg[:, :D_MODEL] * ya + sg[:, D_MODEL:] * yb).astype(BF16), wout_ref)
    return _layer_norm(ALPHA * xn + mixed, g1_ref[...], b1_ref[...])


def _const_spec(shape):
    nd = len(shape)
    return pl.BlockSpec(shape, lambda *_: (0,) * nd, pipeline_mode=pl.Buffered(1))


def _pack_body(w_ref, o_ref):
    o_ref[...] = pltpu.bitcast(w_ref[...].astype(BF16), jnp.uint32)


def _pack_weight(w, name):
    k, n = w.shape
    bn = min(n, PACK_COLS)
    return pl.pallas_call(
        _pack_body, grid=(n // bn,), in_specs=[pl.BlockSpec((k, bn), lambda j: (0, j))],
        out_specs=pl.BlockSpec((k // 2, bn), lambda j: (0, j)),
        out_shape=jax.ShapeDtypeStruct((k // 2, n), jnp.uint32), name=f"pack_{name}",
        compiler_params=_params(1),
    )(w)


def _pack_t_body(wt_ref, o_ref):
    o_ref[...] = pltpu.bitcast(wt_ref[...].T.astype(BF16), jnp.uint32)


def _pack_gates_body(wif_ref, wdt_ref, o_ref):
    wdt = wdt_ref[...]
    rows = jnp.concatenate([wif_ref[...], wdt, wdt, jnp.zeros((LANES - LANE_END, D_MODEL), F32)], axis=0)
    o_ref[...] = pltpu.bitcast(rows.T.astype(BF16), jnp.uint32)


def _pack_in(wt):
    k = wt.shape[1]
    assert wt.shape[0] == D_IN
    starts = (0, IN_Z - IN_I, IN_GATE - IN_DT + IN_Z - IN_I)
    lens = (IN_I, IN_DT - IN_Z, D_IN - IN_GATE)
    assert all(s % SUBLANES == 0 for s in starts) and all(n % PACK_COLS == 0 for n in lens)
    b1 = lens[0] // PACK_COLS
    b2 = b1 + lens[1] // PACK_COLS

    def src_row(j):
        s0, s1, s2 = (s // SUBLANES for s in starts)
        return (j * (PACK_COLS // SUBLANES) + jnp.where(j < b1, s0, jnp.where(j < b2, s1, s2))) * SUBLANES

    big = pl.pallas_call(
        _pack_t_body, grid=(W_BIG_COLS // PACK_COLS,),
        in_specs=[pl.BlockSpec((pl.Element(PACK_COLS), pl.Element(k)), lambda j: (src_row(j), 0))],
        out_specs=pl.BlockSpec((k // 2, PACK_COLS), lambda j: (0, j)),
        out_shape=jax.ShapeDtypeStruct((k // 2, W_BIG_COLS), jnp.uint32),
        name="pack_in", compiler_params=_params(1),
    )(wt)
    small = pl.pallas_call(
        _pack_gates_body, out_shape=jax.ShapeDtypeStruct((k // 2, LANES), jnp.uint32), name="pack_gates",
    )(wt[IN_I:IN_Z], wt[IN_DT:IN_GATE])
    return big, small


def _params(dims):
    return pltpu.CompilerParams(dimension_semantics=("arbitrary",) * dims,
                                vmem_limit_bytes=VMEM_LIMIT_BYTES)


def _mixer_body(rows, n_valid,
                x_ref, g0_ref, b0_ref, wbig_ref, wsm_ref, bsm_ref, alog_ref,
                wmc_ref, bmc_ref, wsc_ref, bsc_ref, cin_qk_ref, cin_xbc_ref,
                c0_ref, n0_ref, m0_ref, s0_ref,
                mg_ref, sng_ref, dexp_ref, wpa_ref, wpb_ref, wout_ref, g1_ref, b1_ref,
                x1_ref, cout_qk_ref, cout_xbc_ref, c1_ref, n1_ref, m1_ref, s1_ref):
    @pl.when(pl.program_id(1) == 0)
    def _():
        cout_qk_ref[...] = cin_qk_ref[...]
        cout_xbc_ref[...] = cin_xbc_ref[...]
        c1_ref[...] = c0_ref[...]
        n1_ref[...] = n0_ref[...]
        m1_ref[...] = m0_ref[...]
        s1_ref[...] = s0_ref[...]

    xn = _layer_norm(x_ref[...], g0_ref[...], b0_ref[...])
    xb = xn.astype(BF16)
    valid = None
    if n_valid < rows:
        valid = lax.broadcasted_iota(jnp.int32, (rows, 1), 0) < n_valid
    gt = _gate_tile(xb, valid, wsm_ref, bsm_ref, alog_ref)

    u_qk = _wdot(xb, wbig_ref, OFF_QK, OFF_QK + 2 * M_QK)
    qk = _silu(_conv_carry(u_qk, cout_qk_ref, wmc_ref, bmc_ref, M_CONV, n_valid))
    q = (qk[:, :M_QK] * (M_DK ** -0.5)).astype(BF16)
    k = qk[:, M_QK:].astype(BF16)
    v = _wdot(xb, wbig_ref, OFF_V, OFF_V + M_V).astype(BF16)
    u_xbc = _wdot(xb, wbig_ref, OFF_XBC, OFF_XBC + S_XBC)
    xbc = _silu(_conv_carry(u_xbc, cout_xbc_ref, wsc_ref, bsc_ref, S_CONV, n_valid))
    xs = xbc[:, :S_INNER]
    bm = xbc[:, S_INNER:S_INNER + S_BC].astype(BF16)
    cm = xbc[:, S_INNER + S_BC:].astype(BF16)

    offs = [OFF_O, OFF_Z, OFF_Z + D_MODEL, OFF_GATE, OFF_GATE + D_MODEL]
    dense = {}
    nchunk = rows // CHUNK
    nslot = nchunk * FILLER_SLOTS
    hs, ys = [], []
    for ci in range(nchunk):
        sl = slice(ci * CHUNK, (ci + 1) * CHUNK)
        state = (c1_ref, n1_ref, m1_ref, s1_ref)

        def filler(slot):
            def issue():
                for off in offs[slot::nslot]:
                    dense[off] = _wdot(xb, wbig_ref, off, off + D_MODEL)
            return issue

        h, y = _chunk_recur(q[sl], k[sl], v[sl], gt[sl], xs[sl], bm[sl], cm[sl], [(state, state)], CHUNK,
                            [filler(ci * FILLER_SLOTS + i) for i in range(FILLER_SLOTS)])
        hs.append(h)
        ys.append(y)
    h = jnp.concatenate(hs, axis=0)
    y = jnp.concatenate(ys, axis=0)

    so = jax.nn.sigmoid(dense[OFF_O])
    sz = _silu(jnp.concatenate([dense[OFF_Z], dense[OFF_Z + D_MODEL]], axis=1))
    sg = jax.nn.sigmoid(jnp.concatenate([dense[OFF_GATE], dense[OFF_GATE + D_MODEL]], axis=1))
    x1_ref[...] = _post_math(h, y, xs, so, sz, sg, xn, mg_ref, sng_ref, dexp_ref,
                             wpa_ref, wpb_ref, wout_ref, g1_ref, b1_ref)


def _mixer(x_rows, cin_qk, cin_xbc, states, wts, *, nseq, rows, n_valid):
    total = x_rows.shape[0]
    nt = total // (nseq * rows)
    rspec = pl.BlockSpec((rows, D_MODEL), lambda b, c: (b * nt + c, 0))

    def per_seq(shape):
        nd = len(shape)
        return pl.BlockSpec((None,) + shape, lambda b, c: (b,) + (0,) * nd)

    def shared(shape):
        nd = len(shape)
        return pl.BlockSpec((None,) + shape, lambda b, c: (0,) * (nd + 1), pipeline_mode=pl.Buffered(1))

    tails = [(SUBLANES, 2 * M_QK), (SUBLANES, S_XBC)]
    in_specs = ([rspec, _const_spec((1, D_MODEL)), _const_spec((1, D_MODEL)),
                 _const_spec((D_MODEL // 2, W_BIG_COLS)), _const_spec((D_MODEL // 2, LANES)),
                 _const_spec((1, LANES)), _const_spec((1, LANES)),
                 _const_spec((M_CONV, 2 * M_QK)), _const_spec((1, 2 * M_QK)),
                 _const_spec((S_CONV, S_XBC)), _const_spec((1, S_XBC))]
                + [shared(s) for s in tails + STATE_SHAPES]
                + [_const_spec((1, M_V)), _const_spec((1, S_INNER)), _const_spec((1, S_INNER)),
                   _const_spec((M_V // 2, D_MODEL)), _const_spec((S_INNER // 2, D_MODEL)),
                   _const_spec((D_MODEL // 2, D_MODEL)), _const_spec((1, D_MODEL)), _const_spec((1, D_MODEL))])
    out_specs = [rspec] + [per_seq(s) for s in tails + STATE_SHAPES]
    out_shapes = ([jax.ShapeDtypeStruct((total, D_MODEL), F32)]
                  + [jax.ShapeDtypeStruct((nseq,) + s, F32) for s in tails + STATE_SHAPES])
    outs = pl.pallas_call(
        functools.partial(_mixer_body, rows, n_valid),
        grid=(nseq, nt), in_specs=in_specs, out_specs=out_specs, out_shape=out_shapes,
        name=f"mixer_{nseq}", compiler_params=_params(2),
    )(x_rows, wts["ln0_g"], wts["ln0_b"], wts["w_big"], wts["w_small"], wts["b_small"], wts["a_log"],
      wts["w_mconv"], wts["b_mconv"], wts["w_sconv"], wts["b_sconv"], cin_qk, cin_xbc, *states,
      wts["mnorm_g"], wts["snorm_g"], wts["d_exp"], wts["w_proj_a"], wts["w_proj_b"], wts["w_out"],
      wts["ln1_g"], wts["ln1_b"])
    return outs[0], outs[1:3], outs[3:]


def _in_proj_body(x_ref, g0_ref, b0_ref, wbig_ref, wsm_ref, bsm_ref, alog_ref,
                  wmc_ref, bmc_ref, wsc_ref, bsc_ref, st_qk_ref, st_xbc_ref,
                  q_ref, k_ref, v_ref, gate_ref, xs_ref, bm_ref, cm_ref,
                  xn_ref, xs_tm_ref, so_ref, sz_ref, sg_ref, new_qk_ref, new_xbc_ref):
    xn = _layer_norm(_time_major(x_ref), g0_ref[...], b0_ref[...])
    xn_ref[...] = xn
    xb = xn.astype(BF16)
    lane = lax.broadcasted_iota(jnp.int32, (1, LANES), 1)
    _store_seq_major(gate_ref, _gate_tile(xb, None, wsm_ref, bsm_ref, alog_ref),
                     jnp.where(lane < LANE_LF, -jnp.inf, 0.0))

    u_qk = _wdot(xb, wbig_ref, OFF_QK, OFF_QK + 2 * M_QK)
    qk = _silu(_conv_tm(u_qk, st_qk_ref, new_qk_ref, wmc_ref, bmc_ref, M_CONV, 0))
    _store_seq_major(q_ref, qk[:, :M_QK] * (M_DK ** -0.5))
    _store_seq_major(k_ref, qk[:, M_QK:])
    _store_seq_major(v_ref, _wdot(xb, wbig_ref, OFF_V, OFF_V + M_V))
    so_ref[...] = jax.nn.sigmoid(_wdot(xb, wbig_ref, OFF_O, OFF_O + M_V))
    sz_ref[...] = _silu(_wdot(xb, wbig_ref, OFF_Z, OFF_Z + S_INNER))
    u_xbc = _wdot(xb, wbig_ref, OFF_XBC, OFF_XBC + S_XBC)
    xbc = _silu(_conv_tm(u_xbc, st_xbc_ref, new_xbc_ref, wsc_ref, bsc_ref, S_CONV, 0))
    xs_tm_ref[...] = xbc[:, :S_INNER]
    _store_seq_major(xs_ref, xbc[:, :S_INNER])
    _store_seq_major(bm_ref, xbc[:, S_INNER:S_INNER + S_BC])
    _store_seq_major(cm_ref, xbc[:, S_INNER + S_BC:])
    sg_ref[...] = jax.nn.sigmoid(_wdot(xb, wbig_ref, OFF_GATE, OFF_GATE + 2 * D_MODEL))


def _in_proj(x, st_qk, st_xbc, wts):
    nseq = x.shape[0]
    nb = SAMPLE_BLOCK

    def rows_first(width, cols):
        return pl.BlockSpec((width - 1, nb, cols), lambda i: (0, i, 0))

    def sspec(rows, cols):
        return pl.BlockSpec((nb, rows, cols), lambda i: (i, 0, 0))

    def tspec(cols):
        return pl.BlockSpec((nb * DEC_SEQ, cols), lambda i: (i, 0))

    in_specs = [sspec(DEC_SEQ, D_MODEL), _const_spec((1, D_MODEL)), _const_spec((1, D_MODEL)),
                _const_spec((D_MODEL // 2, W_BIG_COLS)), _const_spec((D_MODEL // 2, LANES)),
                _const_spec((1, LANES)), _const_spec((1, LANES)),
                _const_spec((M_CONV, 2 * M_QK)), _const_spec((1, 2 * M_QK)),
                _const_spec((S_CONV, S_XBC)), _const_spec((1, S_XBC)),
                rows_first(M_CONV, 2 * M_QK), rows_first(S_CONV, S_XBC)]
    seq_cols = [M_QK, M_QK, M_V, LANES, S_INNER, S_BC, S_BC]
    tm_cols = [D_MODEL, S_INNER, M_V, S_INNER, 2 * D_MODEL]
    outs = pl.pallas_call(
        _in_proj_body, grid=(nseq // nb,), in_specs=in_specs,
        out_specs=([sspec(SEQ_PAD, c) for c in seq_cols] + [tspec(c) for c in tm_cols]
                   + [rows_first(M_CONV, 2 * M_QK), rows_first(S_CONV, S_XBC)]),
        out_shape=([jax.ShapeDtypeStruct((nseq, SEQ_PAD, c), F32) for c in seq_cols]
                   + [jax.ShapeDtypeStruct((nseq * DEC_SEQ, c), F32) for c in tm_cols]
                   + [jax.ShapeDtypeStruct((M_CONV - 1, nseq, 2 * M_QK), F32),
                      jax.ShapeDtypeStruct((S_CONV - 1, nseq, S_XBC), F32)]),
        name="sample_in_proj", compiler_params=_params(1),
    )(x, wts["ln0_g"], wts["ln0_b"], wts["w_big"], wts["w_small"], wts["b_small"], wts["a_log"],
      wts["w_mconv"], wts["b_mconv"], wts["w_sconv"], wts["b_sconv"], st_qk, st_xbc)
    return outs[:7], outs[7:12], outs[12:]


def _sample_recur_body(q_ref, k_ref, v_ref, gate_ref, xs_ref, bm_ref, cm_ref,
                       c0_ref, n0_ref, m0_ref, s0_ref,
                       h_ref, y_ref, c1_ref, n1_ref, m1_ref, s1_ref):
    nb = SAMPLE_SEQS_PER_STEP
    lane = lax.broadcasted_iota(jnp.int32, (1, LANES), 1)

    def stack(ref, dtype, fill=None):
        x = ref[...]
        c = x.shape[2]
        pad = jnp.zeros((1, 1, c), F32) if fill is None else fill[None]
        x = jnp.concatenate([x, jnp.broadcast_to(pad, (nb, SAMPLE_SUB - SEQ_PAD, c))], axis=1)
        return x.reshape(nb * SAMPLE_SUB, c).astype(dtype)

    def unstack(x):
        return x.reshape(nb, SAMPLE_SUB, x.shape[1])[:, :SEQ_PAD, :]

    states = [((c0_ref.at[i], n0_ref.at[i], m0_ref.at[i], s0_ref.at[i]),
               (c1_ref.at[i], n1_ref.at[i], m1_ref.at[i], s1_ref.at[i])) for i in range(nb)]
    h, y = _chunk_recur(stack(q_ref, BF16), stack(k_ref, BF16), stack(v_ref, BF16),
                        stack(gate_ref, F32, jnp.where(lane < LANE_LF, -jnp.inf, 0.0)),
                        stack(xs_ref, F32), stack(bm_ref, BF16), stack(cm_ref, BF16), states, SAMPLE_SUB)
    h_ref[...] = unstack(h)
    y_ref[...] = unstack(y)


def _sample_recur(acts, states, *, nseq):
    q, k, v, gate, xs, bm, cm = acts
    nb = SAMPLE_SEQS_PER_STEP
    assert nb * SAMPLE_SUB == CHUNK

    def rspec(cols):
        return pl.BlockSpec((nb, SEQ_PAD, cols), lambda i: (i, 0, 0))

    def sspec(shape):
        nd = len(shape)
        return pl.BlockSpec((nb,) + shape, lambda i: (i,) + (0,) * nd)

    in_specs = ([rspec(M_QK), rspec(M_QK), rspec(M_V), rspec(LANES), rspec(S_INNER), rspec(S_BC), rspec(S_BC)]
                + [sspec(s) for s in STATE_SHAPES])
    out_specs = [rspec(M_V), rspec(S_INNER)] + [sspec(s) for s in STATE_SHAPES]
    out_shapes = ([jax.ShapeDtypeStruct((nseq, SEQ_PAD, M_V), F32),
                   jax.ShapeDtypeStruct((nseq, SEQ_PAD, S_INNER), F32)]
                  + [jax.ShapeDtypeStruct((nseq,) + s, F32) for s in STATE_SHAPES])
    outs = pl.pallas_call(
        _sample_recur_body, grid=(nseq // nb,), in_specs=in_specs, out_specs=out_specs,
        out_shape=out_shapes, name="sample_recur", compiler_params=_params(1),
    )(q, k, v, gate, xs, bm, cm, *states)
    return outs[0], outs[1], outs[2:]


def _post_body(h_ref, y_ref, xs_ref, so_ref, sz_ref, sg_ref, xn_ref,
               mg_ref, sng_ref, dexp_ref, wpa_ref, wpb_ref, wout_ref, g1_ref, b1_ref, x1_ref):
    h = jnp.concatenate([h_ref[:, t, :] for t in range(DEC_SEQ)], axis=0)
    y = jnp.concatenate([y_ref[:, t, :] for t in range(DEC_SEQ)], axis=0)
    x1_ref[...] = _post_math(h, y, xs_ref[...], so_ref[...], sz_ref[...], sg_ref[...],
                             xn_ref[...], mg_ref, sng_ref, dexp_ref, wpa_ref, wpb_ref, wout_ref,
                             g1_ref, b1_ref)


def _post(h, y, xs, so, sz, sg, xn, wts):
    nseq = h.shape[0]
    nb = SAMPLE_BLOCK

    def sspec(cols):
        return pl.BlockSpec((nb, SEQ_PAD, cols), lambda i: (i, 0, 0))

    def tspec(cols):
        return pl.BlockSpec((nb * DEC_SEQ, cols), lambda i: (i, 0))

    in_specs = [sspec(M_V), sspec(S_INNER), tspec(S_INNER), tspec(M_V), tspec(S_INNER),
                tspec(2 * D_MODEL), tspec(D_MODEL),
                _const_spec((1, M_V)), _const_spec((1, S_INNER)), _const_spec((1, S_INNER)),
                _const_spec((M_V // 2, D_MODEL)), _const_spec((S_INNER // 2, D_MODEL)),
                _const_spec((D_MODEL // 2, D_MODEL)), _const_spec((1, D_MODEL)), _const_spec((1, D_MODEL))]
    return pl.pallas_call(
        _post_body, grid=(nseq // nb,), in_specs=in_specs, out_specs=tspec(D_MODEL),
        out_shape=jax.ShapeDtypeStruct((nseq * DEC_SEQ, D_MODEL), F32), name="sample_post",
        compiler_params=_params(1),
    )(h, y, xs, so, sz, sg, xn, wts["mnorm_g"], wts["snorm_g"], wts["d_exp"],
      wts["w_proj_a"], wts["w_proj_b"], wts["w_out"], wts["ln1_g"], wts["ln1_b"])


def _ffn_body(mode, rows, n_valid,
              x1_ref, wup_ref, wfc_ref, bfc_ref, wdn_ref, g2_ref, b2_ref, cin_ref,
              y_ref, cout_ref):
    x1 = x1_ref[...]
    x1b = x1.astype(BF16)
    if mode == "seq":
        @pl.when(pl.program_id(1) == 0)
        def _():
            cout_ref[...] = cin_ref[...]

    def conv(u, cols):
        if mode == "tm":
            return _conv_tm(u, cin_ref, cout_ref, wfc_ref, bfc_ref, F_CONV, 1, cols)
        return _conv_carry(u, cout_ref, wfc_ref, bfc_ref, F_CONV, n_valid, cols)

    def up_proj(j):
        lo = j * FFN_BLOCK
        return (_wdot(x1b, wup_ref, lo, lo + FFN_BLOCK),
                _wdot(x1b, wup_ref, D_FF + lo, D_FF + lo + FFN_BLOCK))

    nblk = D_FF // FFN_BLOCK
    ff = None
    ups = [up_proj(j) for j in range(min(FFN_AHEAD, nblk))]
    for j in range(nblk):
        lo = j * FFN_BLOCK
        ua, ub = ups[j]
        if j + FFN_AHEAD < nblk:
            ups.append(up_proj(j + FFN_AHEAD))
        va = conv(ua, slice(lo, lo + FFN_BLOCK))
        vb = conv(ub, slice(D_FF + lo, D_FF + lo + FFN_BLOCK))
        act = (_silu(va) * vb).astype(BF16)
        part = _dot(act, pltpu.bitcast(wdn_ref[lo // 2:(lo + FFN_BLOCK) // 2, :], BF16))
        ff = part if ff is None else ff + part
    y = _layer_norm(ALPHA * x1 + ff, g2_ref[...], b2_ref[...])
    if mode == "tm":
        nseq = y_ref.shape[0]
        for t in range(DEC_SEQ):
            y_ref[:, t, :] = y[t * nseq:(t + 1) * nseq]
    else:
        y_ref[...] = y


def _ffn(x1, cin, wts, *, mode, nseq, rows, n_valid):
    total = x1.shape[0]
    nt = total // (nseq * rows)

    def rspec(cols):
        return pl.BlockSpec((rows, cols), lambda b, c: (b * nt + c, 0))

    if mode == "seq":
        cin_spec = pl.BlockSpec((None, SUBLANES, 2 * D_FF), lambda b, c: (0, 0, 0))
        cout_spec = pl.BlockSpec((None, SUBLANES, 2 * D_FF), lambda b, c: (b, 0, 0))
        cout_shape = jax.ShapeDtypeStruct((nseq, SUBLANES, 2 * D_FF), F32)
        y_spec = rspec(D_MODEL)
        y_shape = jax.ShapeDtypeStruct((total, D_MODEL), F32)
    else:
        nb = rows // DEC_SEQ
        sspec = lambda r, cols: pl.BlockSpec((nb, r, cols), lambda b, c: (b * nt + c, 0, 0))
        cin_spec = sspec(F_CONV - 1, 2 * D_FF)
        cout_spec = sspec(F_CONV - 1, 2 * D_FF)
        cout_shape = jax.ShapeDtypeStruct((total // DEC_SEQ, F_CONV - 1, 2 * D_FF), F32)
        y_spec = sspec(DEC_SEQ, D_MODEL)
        y_shape = jax.ShapeDtypeStruct((total // DEC_SEQ, DEC_SEQ, D_MODEL), F32)
    in_specs = [rspec(D_MODEL), _const_spec((D_MODEL // 2, 2 * D_FF)), _const_spec((F_CONV, 2 * D_FF)),
                _const_spec((1, 2 * D_FF)), _const_spec((D_FF // 2, D_MODEL)),
                _const_spec((1, D_MODEL)), _const_spec((1, D_MODEL)), cin_spec]
    return pl.pallas_call(
        functools.partial(_ffn_body, mode, rows, n_valid),
        grid=(nseq, nt), in_specs=in_specs, out_specs=[y_spec, cout_spec],
        out_shape=[y_shape, cout_shape],
        name=f"ffn_{mode}_{nseq}", compiler_params=_params(2),
    )(x1, wts["w_up"], wts["w_fconv"], wts["b_fconv"], wts["w_down"], wts["ln2_g"], wts["ln2_b"], cin)


def kernel(x_prompt, x_sample, state_mlstm_conv, state_mlstm_C, state_mlstm_n, state_mlstm_m, state_ssm_conv, state_ssm, state_ffn_conv, meta_tokens, ln0_g, ln0_b, w_in, b_mlstm_if, w_mlstm_conv, b_mlstm_conv, mlstm_norm_g, w_proj_a, w_ssm_conv, b_ssm_conv, ssm_dt_bias, ssm_A_log, ssm_D, ssm_norm_g, w_proj_b, w_out, ln1_g, ln1_b, w_up, w_ffn_conv, b_ffn_conv, w_down, ln2_g, ln2_b):
    batch, seq, _ = x_prompt.shape
    dec_batch, dec_seq, _ = x_sample.shape
    prompt_tile = 2 * CHUNK
    ffn_tile = 2 * CHUNK
    assert dec_seq == DEC_SEQ and seq % ffn_tile == 0 and meta_tokens.shape[0] == N_META
    assert dec_batch % SAMPLE_SEQS_PER_STEP == 0 and dec_batch % SAMPLE_BLOCK == 0

    w_big, w_small = _pack_in(w_in[0].T)
    lane_pad = jnp.zeros((LANES - LANE_END,), F32)
    b_small = jnp.concatenate([b_mlstm_if[0], ssm_dt_bias[0], ssm_dt_bias[0], lane_pad])[None]
    a_log = jnp.concatenate([jnp.zeros((LANE_DTA,), F32), ssm_A_log[0],
                             jnp.zeros((LANES - LANE_DT,), F32)])[None]
    wts = {
        "ln0_g": ln0_g[None], "ln0_b": ln0_b[None], "w_big": w_big, "w_small": w_small,
        "b_small": b_small, "a_log": a_log,
        "w_mconv": w_mlstm_conv[0], "b_mconv": b_mlstm_conv, "w_sconv": w_ssm_conv[0], "b_sconv": b_ssm_conv,
        "mnorm_g": mlstm_norm_g, "snorm_g": ssm_norm_g, "d_exp": jnp.repeat(ssm_D[0], S_HEADDIM)[None],
        "w_proj_a": _pack_weight(w_proj_a[0], "proj_a"), "w_proj_b": _pack_weight(w_proj_b[0], "proj_b"),
        "w_out": _pack_weight(w_out[0], "out"), "ln1_g": ln1_g, "ln1_b": ln1_b,
        "w_up": _pack_weight(w_up[0], "up"), "w_fconv": w_ffn_conv[0], "b_fconv": b_ffn_conv,
        "w_down": _pack_weight(w_down[0], "down"), "ln2_g": ln2_g, "ln2_b": ln2_b,
    }

    x_meta = jnp.pad(meta_tokens, ((0, CHUNK - N_META), (0, 0)))
    zero_states = [jnp.zeros((1,) + s, F32) for s in STATE_SHAPES]
    x1_m, tails_m, states_m = _mixer(
        x_meta, jnp.zeros((1, SUBLANES, 2 * M_QK), F32), jnp.zeros((1, SUBLANES, S_XBC), F32),
        zero_states, wts, nseq=1, rows=CHUNK, n_valid=N_META)
    _, tail_ffn_m = _ffn(x1_m, jnp.zeros((1, SUBLANES, 2 * D_FF), F32), wts, mode="seq", nseq=1,
                         rows=CHUNK, n_valid=N_META)

    x1_p, tails_p, states_p = _mixer(
        x_prompt.reshape(batch * seq, D_MODEL), tails_m[0], tails_m[1], states_m, wts,
        nseq=batch, rows=prompt_tile, n_valid=prompt_tile)
    y_p, tail_ffn_p = _ffn(x1_p, tail_ffn_m, wts, mode="seq", nseq=batch, rows=ffn_tile, n_valid=ffn_tile)

    def rows_first(a):
        return jnp.swapaxes(a, 0, 1)

    acts, (xn, xs_tm, so, sz, sg), (new_qk, new_xbc) = _in_proj(
        x_sample, rows_first(state_mlstm_conv[0]), rows_first(state_ssm_conv[0]), wts)
    m_in = jnp.pad(state_mlstm_m[0], ((0, 0), (0, LANES - M_HEADS)))[:, None, :]
    h_s, yss_s, states_s = _sample_recur(
        acts, (state_mlstm_C[0], state_mlstm_n[0], m_in, state_ssm[0].reshape(dec_batch, S_INNER, S_STATE)),
        nseq=dec_batch)
    x1_s = _post(h_s, yss_s, xs_tm, so, sz, sg, xn, wts)
    y_s, new_ffn = _ffn(x1_s, state_ffn_conv[0], wts, mode="tm", nseq=1,
                        rows=SAMPLE_BLOCK * DEC_SEQ, n_valid=SAMPLE_BLOCK * DEC_SEQ)

    def pack_states(convs, states, n):
        c1, n1, m1, s1 = states
        return (convs[0][None], c1[None], n1[None], m1[:, 0, :M_HEADS][None], convs[1][None],
                s1.reshape(n, S_HEADS, S_HEADDIM, S_STATE)[None], convs[2][None])

    def tail_rows(t, width):
        return t[:, SUBLANES - (width - 1):, :]

    p_out = pack_states((tail_rows(tails_p[0], M_CONV), tail_rows(tails_p[1], S_CONV),
                         tail_rows(tail_ffn_p, F_CONV)), states_p, batch)
    s_out = pack_states((rows_first(new_qk), rows_first(new_xbc), new_ffn), states_s, dec_batch)
    y_prompt = y_p.reshape(batch, seq, D_MODEL)
    return (y_prompt, y_s) + p_out + s_out
```

```python
import functools

import jax
import jax.numpy as jnp
from jax import lax
from jax.experimental import pallas as pl
from jax.experimental.pallas import tpu as pltpu

F32 = jnp.float32
BF16 = jnp.bfloat16

D_MODEL = 1024
N_META = 16
M_HEADS = 4
M_DK = 128
M_DV = 256
M_QK = M_HEADS * M_DK
M_V = M_HEADS * M_DV
M_CONV = 4
S_INNER = 2048
S_HEADDIM = 64
S_HEADS = 32
S_GROUPS = 4
S_STATE = 128
S_CONV = 4
S_BC = S_GROUPS * S_STATE
S_XBC = S_INNER + 2 * S_BC
S_GROUP_W = S_INNER // S_GROUPS
S_HEADS_PER_GROUP = S_HEADS // S_GROUPS
D_FF = 2816
F_CONV = 3
ALPHA = 2.0 ** 0.25
LN_EPS = 1e-5
RMS_EPS = 1e-5

LANES = 128
SUBLANES = 8
CHUNK = 128
SEQ_PAD = SUBLANES
DEC_SEQ = 4
SAMPLE_SEQS_PER_STEP = 8
SAMPLE_SUB = 16
SAMPLE_BLOCK = 32
PACK_COLS = 512
FFN_BLOCK = 256
FFN_AHEAD = 3
FFN_SUB_TILES = 4
FILLER_SLOTS = 2
VMEM_LIMIT_BYTES = 60 * 1024 * 1024

OFF_QK = 0
OFF_V = OFF_QK + 2 * M_QK
OFF_O = OFF_V + M_V
OFF_Z = OFF_O + M_V
OFF_XBC = OFF_Z + S_INNER
OFF_GATE = OFF_XBC + S_XBC
W_BIG_COLS = OFF_GATE + 2 * D_MODEL
LANE_IG = 0
LANE_LF = M_HEADS
LANE_DTA = 2 * M_HEADS
LANE_DT = LANE_DTA + S_HEADS
LANE_END = LANE_DT + S_HEADS

IN_I = 2 * M_QK + 2 * M_V
IN_Z = IN_I + 2 * M_HEADS
IN_DT = IN_Z + S_INNER + S_XBC
IN_GATE = IN_DT + S_HEADS
D_IN = IN_GATE + 2 * D_MODEL

STATE_SHAPES = [(M_HEADS, M_DK, M_DV), (M_HEADS, M_DK), (1, LANES), (S_INNER, S_STATE)]


def _softplus(x):
    return jnp.maximum(x, 0.0) + jnp.log1p(jnp.exp(-jnp.abs(x)))


def _silu(x):
    return x * jax.nn.sigmoid(x)


def _layer_norm(x, g, b):
    mu = jnp.mean(x, axis=-1, keepdims=True)
    xc = x - mu
    var = jnp.mean(xc * xc, axis=-1, keepdims=True)
    return xc * lax.rsqrt(var + LN_EPS) * g + b


def _dot(a, b):
    return jnp.dot(a, b, preferred_element_type=F32)


def _wdot(a, w_ref, lo=None, hi=None):
    w = w_ref[...] if lo is None else w_ref[:, lo:hi]
    return _dot(a, pltpu.bitcast(w, BF16))


def _dot_nt(a, b):
    return lax.dot_general(a, b, (((1,), (1,)), ((), ())), preferred_element_type=F32)


def _dot_tn(a, b):
    return lax.dot_general(a, b, (((0,), (0,)), ((), ())), preferred_element_type=F32)


def _conv_carry(u, carry_ref, w_ref, b_ref, width, n_valid, cols=slice(None)):
    full = jnp.concatenate([carry_ref[:, cols], u], axis=0)
    acc = b_ref[:, cols] + u * w_ref[width - 1:width, cols]
    for k in range(1, width):
        acc = acc + pltpu.roll(full, k, 0)[SUBLANES:] * w_ref[width - 1 - k:width - k, cols]
    carry_ref[:, cols] = u[n_valid - SUBLANES:n_valid]
    return acc


def _time_major(ref):
    return jnp.concatenate([ref[:, t, :] for t in range(DEC_SEQ)], axis=0)


def _store_seq_major(ref, x, fill=None):
    nseq, _, c = ref.shape
    pad = jnp.zeros((1, c), x.dtype) if fill is None else fill
    ref[:, DEC_SEQ:, :] = jnp.broadcast_to(pad[None], (nseq, SEQ_PAD - DEC_SEQ, c))
    for t in range(DEC_SEQ):
        ref[:, t, :] = x[t * nseq:(t + 1) * nseq]


def _conv_tm(u, st_ref, new_ref, w_ref, b_ref, width, row_axis, cols=slice(None)):
    def idx(j):
        return (j, slice(None), cols) if row_axis == 0 else (slice(None), j, cols)

    nseq = st_ref.shape[1 - row_axis]
    full = ([st_ref[idx(j)] for j in range(width - 1)]
            + [u[t * nseq:(t + 1) * nseq] for t in range(DEC_SEQ)])
    outs = []
    for t in range(DEC_SEQ):
        acc = b_ref[:, cols]
        for j in range(width):
            acc = acc + full[t + j] * w_ref[j:j + 1, cols]
        outs.append(acc)
    for j in range(width - 1):
        new_ref[idx(j)] = full[DEC_SEQ + j]
    return jnp.concatenate(outs, axis=0)


def _gate_tile(xb, valid, wsm_ref, bsm_ref, alog_ref):
    g = _wdot(xb, wsm_ref) + bsm_ref[...]
    lane = lax.broadcasted_iota(jnp.int32, (1, LANES), 1)
    sp = _softplus(g)
    lsg = -_softplus(-g)
    a_row = jnp.where((lane >= LANE_DTA) & (lane < LANE_DT), -jnp.exp(alog_ref[...]), 0.0)
    tile = jnp.where(lane < LANE_LF, g,
                     jnp.where(lane < LANE_DTA, lsg,
                               jnp.where(lane < LANE_DT, sp * a_row,
                                         jnp.where(lane < LANE_END, sp, 0.0))))
    if valid is None:
        return tile
    return jnp.where(valid, tile, jnp.where(lane < LANE_LF, -jnp.inf, 0.0))


def _cumsum_rows(x, period):
    pos = lax.broadcasted_iota(jnp.int32, x.shape, 0) & (period - 1)
    k = 1
    while k < period:
        x = x + jnp.where(pos >= k, pltpu.roll(x, k, 0), 0.0)
        k *= 2
    return x


def _per_row(values, sub):
    if len(values) == 1:
        return values[0]
    return jnp.concatenate([jnp.broadcast_to(x, (sub, x.shape[1])) for x in values], axis=0)


def _chunk_recur(q, k, v, gt, xs, bm, cm, states, sub, fillers=None):
    staged = fillers is not None
    fillers = list(fillers or ())

    def fill():
        if fillers:
            fillers.pop(0)()

    def run(vector_stage, matmul_stage, items):
        if staged:
            ctxs = [vector_stage(i) for i in items]
            return [matmul_stage(c) for c in ctxs]
        return [matmul_stage(vector_stage(i)) for i in items]

    nsub = len(states)
    assert nsub * sub == CHUNK == q.shape[0]
    sub_rows = [slice(i * sub, (i + 1) * sub) for i in range(nsub)]
    lane = lax.broadcasted_iota(jnp.int32, (1, LANES), 1)
    cum_lanes = (lane >= LANE_LF) & (lane < LANE_DT)
    gc = jnp.where(cum_lanes, _cumsum_rows(jnp.where(cum_lanes, gt, 0.0), sub), gt)
    gct = gc.T
    lasts = [gc[r.stop - 1:r.stop, :] for r in sub_rows]
    last_rows = _per_row(lasts, sub)

    def head_dots(h):
        qh = q[:, h * M_DK:(h + 1) * M_DK]
        c0s = [st_in[0][h] for st_in, _ in states]
        qc = [_dot(qh[r], c0.astype(BF16)) for r, c0 in zip(sub_rows, c0s)]
        return _dot_nt(qh, k[:, h * M_DK:(h + 1) * M_DK]), jnp.concatenate(qc, axis=0), c0s

    def group_dots(g):
        cg = cm[:, g * S_STATE:(g + 1) * S_STATE]
        s_olds = [st_in[3][g * S_GROUP_W:(g + 1) * S_GROUP_W, :] for st_in, _ in states]
        yi = [_dot_nt(cg[r], s_old.astype(BF16)) for r, s_old in zip(sub_rows, s_olds)]
        return (_dot_nt(cg, bm[:, g * S_STATE:(g + 1) * S_STATE]),
                jnp.concatenate(yi, axis=0), s_olds)

    if staged:
        head_pre = [head_dots(h) for h in range(M_HEADS)]
        group_pre = [group_dots(g) for g in range(S_GROUPS)]
        head_dots = head_pre.__getitem__
        group_dots = group_pre.__getitem__
    fill()

    ti = lax.broadcasted_iota(jnp.int32, (CHUNK, CHUNK), 0)
    si = lax.broadcasted_iota(jnp.int32, (CHUNK, CHUNK), 1)
    causal = si <= ti
    if nsub > 1:
        causal = causal & ((si & -sub) == (ti & -sub))
    sub_lanes = [(lane >= r.start) & (lane < r.stop) for r in sub_rows]
    m_rows = [st_in[2][...] for st_in, _ in states]

    def head_vector(h):
        qh = q[:, h * M_DK:(h + 1) * M_DK]
        bt_c = gc[:, LANE_LF + h:LANE_LF + h + 1]
        ig_c = gc[:, LANE_IG + h:LANE_IG + h + 1]
        bt_r = gct[LANE_LF + h:LANE_LF + h + 1, :]
        ig_r = gct[LANE_IG + h:LANE_IG + h + 1, :]
        m0s = [m_row[:, h:h + 1] for m_row in m_rows]
        n0s = [st_in[1][h:h + 1, :] for st_in, _ in states]
        m0 = _per_row(m0s, sub)
        sqk, qc, c0s = head_dots(h)

        dmat = jnp.where(causal, bt_c - bt_r + ig_r, -jnp.inf)
        inter = bt_c + m0
        m_t = jnp.maximum(inter, jnp.max(dmat, axis=1, keepdims=True))
        w_intra = jnp.exp(dmat - m_t)
        w_inter = jnp.exp(inter - m_t)
        s = sqk * w_intra
        den = (jnp.sum(s, axis=1, keepdims=True)
               + jnp.sum(qh.astype(F32) * _per_row(n0s, sub), axis=1, keepdims=True) * w_inter)
        rdenom = 1.0 / jnp.maximum(jnp.abs(den), jnp.exp(-m_t))

        updates = []
        for i, r in enumerate(sub_rows):
            b_last = lasts[i][:, LANE_LF + h:LANE_LF + h + 1]
            d_last = b_last - bt_r + ig_r
            if nsub > 1:
                d_last = jnp.where(sub_lanes[i], d_last, -jnp.inf)
            m_new = jnp.maximum(b_last + m0s[i], jnp.max(d_last, axis=1, keepdims=True))
            w_last = jnp.exp(b_last - bt_c[r] + ig_c[r] - m_new)
            decay = jnp.exp(b_last + m0s[i] - m_new)
            kw = k[r, h * M_DK:(h + 1) * M_DK].astype(F32) * w_last
            states[i][1][1][h:h + 1, :] = decay * n0s[i] + jnp.sum(kw, axis=0, keepdims=True)
            updates.append((kw.astype(BF16), decay * c0s[i], m_new))
        return h, s.astype(BF16), qc * w_inter, rdenom, updates

    def head_matmul(ctx):
        h, s_b, inter_part, rdenom, updates = ctx
        num = _dot(s_b, v[:, h * M_DV:(h + 1) * M_DV]) + inter_part
        for i, (kw_b, c_decayed, _) in enumerate(updates):
            states[i][1][0][h] = c_decayed + _dot_tn(kw_b, v[sub_rows[i], h * M_DV:(h + 1) * M_DV])
        return num * rdenom, [m_new for _, _, m_new in updates]

    heads = run(head_vector, head_matmul, range(M_HEADS))
    for i in range(nsub):
        m_new_row = m_rows[i]
        for h, (_, m_news) in enumerate(heads):
            m_new_row = jnp.where(lane == h, m_news[i], m_new_row)
        states[i][1][2][...] = m_new_row
    fill()

    lo_half = lax.broadcasted_iota(jnp.int32, (CHUNK, LANES), 1) < S_HEADDIM
    xs_b = xs.astype(BF16)

    def group_vector(g):
        rows_g = slice(g * S_GROUP_W, (g + 1) * S_GROUP_W)
        cb, y_inter, s_olds = group_dots(g)
        pairs, wd_parts = [], []
        dec_parts = [[] for _ in range(nsub)]
        for p in range(S_HEADS_PER_GROUP // 2):
            ms, es, wds = [], [], []
            for r in (2 * p, 2 * p + 1):
                hh = g * S_HEADS_PER_GROUP + r
                bt_c = gc[:, LANE_DTA + hh:LANE_DTA + hh + 1]
                dt_c = gc[:, LANE_DT + hh:LANE_DT + hh + 1]
                bt_r = gct[LANE_DTA + hh:LANE_DTA + hh + 1, :]
                dt_r = gct[LANE_DT + hh:LANE_DT + hh + 1, :]
                b_last = last_rows[:, LANE_DTA + hh:LANE_DTA + hh + 1]
                dec = jnp.exp(jnp.where(causal, bt_c - bt_r, -jnp.inf))
                ms.append((cb * dec * dt_r).astype(BF16))
                es.append(jnp.exp(bt_c))
                wds.append(jnp.exp(b_last - bt_c) * dt_c)
                for i in range(nsub):
                    dec_parts[i].append(jnp.broadcast_to(
                        jnp.exp(lasts[i][:, LANE_DTA + hh:LANE_DTA + hh + 1]), (S_HEADDIM, 1)))
            col = g * S_GROUP_W + p * LANES
            xp = xs_b[:, col:col + LANES]
            zero = jnp.zeros_like(xp)
            rhs = jnp.concatenate([jnp.where(lo_half, xp, zero), jnp.where(lo_half, zero, xp)], axis=0)
            pairs.append((jnp.concatenate(ms, axis=1), rhs,
                          y_inter[:, p * LANES:(p + 1) * LANES] * jnp.where(lo_half, es[0], es[1])))
            wd_parts.append(jnp.where(lo_half, wds[0], wds[1]))
        a_mat = (xs[:, rows_g] * jnp.concatenate(wd_parts, axis=1)).astype(BF16)
        return g, pairs, a_mat, [jnp.concatenate(d, axis=0) * s_old for d, s_old in zip(dec_parts, s_olds)]

    def group_matmul(ctx):
        g, pairs, a_mat, s_decayed = ctx
        ys = [_dot(lhs, rhs) + inter_part for lhs, rhs, inter_part in pairs]
        for i, r in enumerate(sub_rows):
            states[i][1][3][g * S_GROUP_W:(g + 1) * S_GROUP_W, :] = (
                s_decayed[i] + _dot_tn(a_mat[r], bm[r, g * S_STATE:(g + 1) * S_STATE]))
        return jnp.concatenate(ys, axis=1)

    groups = run(group_vector, group_matmul, range(S_GROUPS))
    return jnp.concatenate([hp for hp, _ in heads], axis=1), jnp.concatenate(groups, axis=1)


def _post_math(h, y, xs, so, sz, sg, xn, mg_ref, sng_ref, dexp_ref, wpa_ref, wpb_ref, wout_ref,
               g1_ref, b1_ref):
    hs = []
    for i in range(M_HEADS):
        hh = h[:, i * M_DV:(i + 1) * M_DV]
        mu = jnp.mean(hh, axis=-1, keepdims=True)
        hc = hh - mu
        var = jnp.mean(hc * hc, axis=-1, keepdims=True)
        hs.append(hc * lax.rsqrt(var + LN_EPS))
    hn = jnp.concatenate(hs, axis=1) * mg_ref[...]
    ya = _wdot((so * hn).astype(BF16), wpa_ref)

    y = (y + dexp_ref[...] * xs) * sz
    ys = []
    for g in range(S_GROUPS):
        yg = y[:, g * S_GROUP_W:(g + 1) * S_GROUP_W]
        ys.append(yg * lax.rsqrt(jnp.mean(yg * yg, axis=-1, keepdims=True) + RMS_EPS))
    yn = jnp.concatenate(ys, axis=1) * sng_ref[...]
    yb = _wdot(yn.astype(BF16), wpb_ref)

    mixed = _wdot((sg[:, :D_MODEL] * ya + sg[:, D_MODEL:] * yb).astype(BF16), wout_ref)
    return _layer_norm(ALPHA * xn + mixed, g1_ref[...], b1_ref[...])


def _const_spec(shape):
    nd = len(shape)
    return pl.BlockSpec(shape, lambda *_: (0,) * nd, pipeline_mode=pl.Buffered(1))


def _pack_body(w_ref, o_ref):
    o_ref[...] = pltpu.bitcast(w_ref[...].astype(BF16), jnp.uint32)


def _pack_weight(w, name):
    k, n = w.shape
    bn = min(n, PACK_COLS)
    return pl.pallas_call(
        _pack_body, grid=(n // bn,), in_specs=[pl.BlockSpec((k, bn), lambda j: (0, j))],
        out_specs=pl.BlockSpec((k // 2, bn), lambda j: (0, j)),
        out_shape=jax.ShapeDtypeStruct((k // 2, n), jnp.uint32), name=f"pack_{name}",
        compiler_params=_params(1),
    )(w)


def _pack_t_body(wt_ref, o_ref):
    o_ref[...] = pltpu.bitcast(wt_ref[...].T.astype(BF16), jnp.uint32)


def _pack_gates_body(wif_ref, wdt_ref, o_ref):
    wdt = wdt_ref[...]
    rows = jnp.concatenate([wif_ref[...], wdt, wdt, jnp.zeros((LANES - LANE_END, D_MODEL), F32)], axis=0)
    o_ref[...] = pltpu.bitcast(rows.T.astype(BF16), jnp.uint32)


def _pack_in(wt):
    k = wt.shape[1]
    assert wt.shape[0] == D_IN
    starts = (0, IN_Z - IN_I, IN_GATE - IN_DT + IN_Z - IN_I)
    lens = (IN_I, IN_DT - IN_Z, D_IN - IN_GATE)
    assert all(s % SUBLANES == 0 for s in starts) and all(n % PACK_COLS == 0 for n in lens)
    b1 = lens[0] // PACK_COLS
    b2 = b1 + lens[1] // PACK_COLS

    def src_row(j):
        s0, s1, s2 = (s // SUBLANES for s in starts)
        return (j * (PACK_COLS // SUBLANES) + jnp.where(j < b1, s0, jnp.where(j < b2, s1, s2))) * SUBLANES

    big = pl.pallas_call(
        _pack_t_body, grid=(W_BIG_COLS // PACK_COLS,),
        in_specs=[pl.BlockSpec((pl.Element(PACK_COLS), pl.Element(k)), lambda j: (src_row(j), 0))],
        out_specs=pl.BlockSpec((k // 2, PACK_COLS), lambda j: (0, j)),
        out_shape=jax.ShapeDtypeStruct((k // 2, W_BIG_COLS), jnp.uint32),
        name="pack_in", compiler_params=_params(1),
    )(wt)
    small = pl.pallas_call(
        _pack_gates_body, out_shape=jax.ShapeDtypeStruct((k // 2, LANES), jnp.uint32), name="pack_gates",
    )(wt[IN_I:IN_Z], wt[IN_DT:IN_GATE])
    return big, small


def _params(dims):
    return pltpu.CompilerParams(dimension_semantics=("arbitrary",) * dims,
                                vmem_limit_bytes=VMEM_LIMIT_BYTES)


def _mixer_body(rows, n_valid,
                x_ref, g0_ref, b0_ref, wbig_ref, wsm_ref, bsm_ref, alog_ref,
                wmc_ref, bmc_ref, wsc_ref, bsc_ref, cin_qk_ref, cin_xbc_ref,
                c0_ref, n0_ref, m0_ref, s0_ref,
                mg_ref, sng_ref, dexp_ref, wpa_ref, wpb_ref, wout_ref, g1_ref, b1_ref,
                x1_ref, cout_qk_ref, cout_xbc_ref, c1_ref, n1_ref, m1_ref, s1_ref):
    @pl.when(pl.program_id(1) == 0)
    def _():
        cout_qk_ref[...] = cin_qk_ref[...]
        cout_xbc_ref[...] = cin_xbc_ref[...]
        c1_ref[...] = c0_ref[...]
        n1_ref[...] = n0_ref[...]
        m1_ref[...] = m0_ref[...]
        s1_ref[...] = s0_ref[...]

    xn = _layer_norm(x_ref[...], g0_ref[...], b0_ref[...])
    xb = xn.astype(BF16)
    valid = None
    if n_valid < rows:
        valid = lax.broadcasted_iota(jnp.int32, (rows, 1), 0) < n_valid
    gt = _gate_tile(xb, valid, wsm_ref, bsm_ref, alog_ref)

    u_qk = _wdot(xb, wbig_ref, OFF_QK, OFF_QK + 2 * M_QK)
    qk = _silu(_conv_carry(u_qk, cout_qk_ref, wmc_ref, bmc_ref, M_CONV, n_valid))
    q = (qk[:, :M_QK] * (M_DK ** -0.5)).astype(BF16)
    k = qk[:, M_QK:].astype(BF16)
    v = _wdot(xb, wbig_ref, OFF_V, OFF_V + M_V).astype(BF16)
    u_xbc = _wdot(xb, wbig_ref, OFF_XBC, OFF_XBC + S_XBC)
    xbc = _silu(_conv_carry(u_xbc, cout_xbc_ref, wsc_ref, bsc_ref, S_CONV, n_valid))
    xs = xbc[:, :S_INNER]
    bm = xbc[:, S_INNER:S_INNER + S_BC].astype(BF16)
    cm = xbc[:, S_INNER + S_BC:].astype(BF16)

    offs = [OFF_O, OFF_Z, OFF_Z + D_MODEL, OFF_GATE, OFF_GATE + D_MODEL]
    dense = {}
    nchunk = rows // CHUNK
    nslot = nchunk * FILLER_SLOTS
    hs, ys = [], []
    for ci in range(nchunk):
        sl = slice(ci * CHUNK, (ci + 1) * CHUNK)
        state = (c1_ref, n1_ref, m1_ref, s1_ref)

        def filler(slot):
            def issue():
                for off in offs[slot::nslot]:
                    dense[off] = _wdot(xb, wbig_ref, off, off + D_MODEL)
            return issue

        h, y = _chunk_recur(q[sl], k[sl], v[sl], gt[sl], xs[sl], bm[sl], cm[sl], [(state, state)], CHUNK,
                            [filler(ci * FILLER_SLOTS + i) for i in range(FILLER_SLOTS)])
        hs.append(h)
        ys.append(y)
    h = jnp.concatenate(hs, axis=0)
    y = jnp.concatenate(ys, axis=0)

    so = jax.nn.sigmoid(dense[OFF_O])
    sz = _silu(jnp.concatenate([dense[OFF_Z], dense[OFF_Z + D_MODEL]], axis=1))
    sg = jax.nn.sigmoid(jnp.concatenate([dense[OFF_GATE], dense[OFF_GATE + D_MODEL]], axis=1))
    x1_ref[...] = _post_math(h, y, xs, so, sz, sg, xn, mg_ref, sng_ref, dexp_ref,
                             wpa_ref, wpb_ref, wout_ref, g1_ref, b1_ref)


def _mixer(x_rows, cin_qk, cin_xbc, states, wts, *, nseq, rows, n_valid):
    total = x_rows.shape[0]
    nt = total // (nseq * rows)
    rspec = pl.BlockSpec((rows, D_MODEL), lambda b, c: (b * nt + c, 0))

    def per_seq(shape):
        nd = len(shape)
        return pl.BlockSpec((None,) + shape, lambda b, c: (b,) + (0,) * nd)

    def shared(shape):
        nd = len(shape)
        return pl.BlockSpec((None,) + shape, lambda b, c: (0,) * (nd + 1), pipeline_mode=pl.Buffered(1))

    tails = [(SUBLANES, 2 * M_QK), (SUBLANES, S_XBC)]
    in_specs = ([rspec, _const_spec((1, D_MODEL)), _const_spec((1, D_MODEL)),
                 _const_spec((D_MODEL // 2, W_BIG_COLS)), _const_spec((D_MODEL // 2, LANES)),
                 _const_spec((1, LANES)), _const_spec((1, LANES)),
                 _const_spec((M_CONV, 2 * M_QK)), _const_spec((1, 2 * M_QK)),
                 _const_spec((S_CONV, S_XBC)), _const_spec((1, S_XBC))]
                + [shared(s) for s in tails + STATE_SHAPES]
                + [_const_spec((1, M_V)), _const_spec((1, S_INNER)), _const_spec((1, S_INNER)),
                   _const_spec((M_V // 2, D_MODEL)), _const_spec((S_INNER // 2, D_MODEL)),
                   _const_spec((D_MODEL // 2, D_MODEL)), _const_spec((1, D_MODEL)), _const_spec((1, D_MODEL))])
    out_specs = [rspec] + [per_seq(s) for s in tails + STATE_SHAPES]
    out_shapes = ([jax.ShapeDtypeStruct((total, D_MODEL), F32)]
                  + [jax.ShapeDtypeStruct((nseq,) + s, F32) for s in tails + STATE_SHAPES])
    outs = pl.pallas_call(
        functools.partial(_mixer_body, rows, n_valid),
        grid=(nseq, nt), in_specs=in_specs, out_specs=out_specs, out_shape=out_shapes,
        name=f"mixer_{nseq}", compiler_params=_params(2),
    )(x_rows, wts["ln0_g"], wts["ln0_b"], wts["w_big"], wts["w_small"], wts["b_small"], wts["a_log"],
      wts["w_mconv"], wts["b_mconv"], wts["w_sconv"], wts["b_sconv"], cin_qk, cin_xbc, *states,
      wts["mnorm_g"], wts["snorm_g"], wts["d_exp"], wts["w_proj_a"], wts["w_proj_b"], wts["w_out"],
      wts["ln1_g"], wts["ln1_b"])
    return outs[0], outs[1:3], outs[3:]


def _in_proj_body(x_ref, g0_ref, b0_ref, wbig_ref, wsm_ref, bsm_ref, alog_ref,
                  wmc_ref, bmc_ref, wsc_ref, bsc_ref, st_qk_ref, st_xbc_ref,
                  q_ref, k_ref, v_ref, gate_ref, xs_ref, bm_ref, cm_ref,
                  xn_ref, xs_tm_ref, so_ref, sz_ref, sg_ref, new_qk_ref, new_xbc_ref):
    xn = _layer_norm(_time_major(x_ref), g0_ref[...], b0_ref[...])
    xn_ref[...] = xn
    xb = xn.astype(BF16)
    lane = lax.broadcasted_iota(jnp.int32, (1, LANES), 1)
    _store_seq_major(gate_ref, _gate_tile(xb, None, wsm_ref, bsm_ref, alog_ref),
                     jnp.where(lane < LANE_LF, -jnp.inf, 0.0))

    u_qk = _wdot(xb, wbig_ref, OFF_QK, OFF_QK + 2 * M_QK)
    qk = _silu(_conv_tm(u_qk, st_qk_ref, new_qk_ref, wmc_ref, bmc_ref, M_CONV, 0))
    _store_seq_major(q_ref, qk[:, :M_QK] * (M_DK ** -0.5))
    _store_seq_major(k_ref, qk[:, M_QK:])
    _store_seq_major(v_ref, _wdot(xb, wbig_ref, OFF_V, OFF_V + M_V))
    so_ref[...] = jax.nn.sigmoid(_wdot(xb, wbig_ref, OFF_O, OFF_O + M_V))
    sz_ref[...] = _silu(_wdot(xb, wbig_ref, OFF_Z, OFF_Z + S_INNER))
    u_xbc = _wdot(xb, wbig_ref, OFF_XBC, OFF_XBC + S_XBC)
    xbc = _silu(_conv_tm(u_xbc, st_xbc_ref, new_xbc_ref, wsc_ref, bsc_ref, S_CONV, 0))
    xs_tm_ref[...] = xbc[:, :S_INNER]
    _store_seq_major(xs_ref, xbc[:, :S_INNER])
    _store_seq_major(bm_ref, xbc[:, S_INNER:S_INNER + S_BC])
    _store_seq_major(cm_ref, xbc[:, S_INNER + S_BC:])
    sg_ref[...] = jax.nn.sigmoid(_wdot(xb, wbig_ref, OFF_GATE, OFF_GATE + 2 * D_MODEL))


def _in_proj(x, st_qk, st_xbc, wts):
    nseq = x.shape[0]
    nb = SAMPLE_BLOCK

    def rows_first(width, cols):
        return pl.BlockSpec((width - 1, nb, cols), lambda i: (0, i, 0))

    def sspec(rows, cols):
        return pl.BlockSpec((nb, rows, cols), lambda i: (i, 0, 0))

    def tspec(cols):
        return pl.BlockSpec((nb * DEC_SEQ, cols), lambda i: (i, 0))

    in_specs = [sspec(DEC_SEQ, D_MODEL), _const_spec((1, D_MODEL)), _const_spec((1, D_MODEL)),
                _const_spec((D_MODEL // 2, W_BIG_COLS)), _const_spec((D_MODEL // 2, LANES)),
                _const_spec((1, LANES)), _const_spec((1, LANES)),
                _const_spec((M_CONV, 2 * M_QK)), _const_spec((1, 2 * M_QK)),
                _const_spec((S_CONV, S_XBC)), _const_spec((1, S_XBC)),
                rows_first(M_CONV, 2 * M_QK), rows_first(S_CONV, S_XBC)]
    seq_cols = [M_QK, M_QK, M_V, LANES, S_INNER, S_BC, S_BC]
    tm_cols = [D_MODEL, S_INNER, M_V, S_INNER, 2 * D_MODEL]
    outs = pl.pallas_call(
        _in_proj_body, grid=(nseq // nb,), in_specs=in_specs,
        out_specs=([sspec(SEQ_PAD, c) for c in seq_cols] + [tspec(c) for c in tm_cols]
                   + [rows_first(M_CONV, 2 * M_QK), rows_first(S_CONV, S_XBC)]),
        out_shape=([jax.ShapeDtypeStruct((nseq, SEQ_PAD, c), F32) for c in seq_cols]
                   + [jax.ShapeDtypeStruct((nseq * DEC_SEQ, c), F32) for c in tm_cols]
                   + [jax.ShapeDtypeStruct((M_CONV - 1, nseq, 2 * M_QK), F32),
                      jax.ShapeDtypeStruct((S_CONV - 1, nseq, S_XBC), F32)]),
        name="sample_in_proj", compiler_params=_params(1),
    )(x, wts["ln0_g"], wts["ln0_b"], wts["w_big"], wts["w_small"], wts["b_small"], wts["a_log"],
      wts["w_mconv"], wts["b_mconv"], wts["w_sconv"], wts["b_sconv"], st_qk, st_xbc)
    return outs[:7], outs[7:12], outs[12:]


def _sample_recur_body(q_ref, k_ref, v_ref, gate_ref, xs_ref, bm_ref, cm_ref,
                       c0_ref, n0_ref, m0_ref, s0_ref,
                       h_ref, y_ref, c1_ref, n1_ref, m1_ref, s1_ref):
    nb = SAMPLE_SEQS_PER_STEP
    lane = lax.broadcasted_iota(jnp.int32, (1, LANES), 1)

    def stack(ref, dtype, fill=None):
        x = ref[...]
        c = x.shape[2]
        pad = jnp.zeros((1, 1, c), F32) if fill is None else fill[None]
        x = jnp.concatenate([x, jnp.broadcast_to(pad, (nb, SAMPLE_SUB - SEQ_PAD, c))], axis=1)
        return x.reshape(nb * SAMPLE_SUB, c).astype(dtype)

    def unstack(x):
        return x.reshape(nb, SAMPLE_SUB, x.shape[1])[:, :SEQ_PAD, :]

    states = [((c0_ref.at[i], n0_ref.at[i], m0_ref.at[i], s0_ref.at[i]),
               (c1_ref.at[i], n1_ref.at[i], m1_ref.at[i], s1_ref.at[i])) for i in range(nb)]
    h, y = _chunk_recur(stack(q_ref, BF16), stack(k_ref, BF16), stack(v_ref, BF16),
                        stack(gate_ref, F32, jnp.where(lane < LANE_LF, -jnp.inf, 0.0)),
                        stack(xs_ref, F32), stack(bm_ref, BF16), stack(cm_ref, BF16), states, SAMPLE_SUB)
    h_ref[...] = unstack(h)
    y_ref[...] = unstack(y)


def _sample_recur(acts, states, *, nseq):
    q, k, v, gate, xs, bm, cm = acts
    nb = SAMPLE_SEQS_PER_STEP
    assert nb * SAMPLE_SUB == CHUNK

    def rspec(cols):
        return pl.BlockSpec((nb, SEQ_PAD, cols), lambda i: (i, 0, 0))

    def sspec(shape):
        nd = len(shape)
        return pl.BlockSpec((nb,) + shape, lambda i: (i,) + (0,) * nd)

    in_specs = ([rspec(M_QK), rspec(M_QK), rspec(M_V), rspec(LANES), rspec(S_INNER), rspec(S_BC), rspec(S_BC)]
                + [sspec(s) for s in STATE_SHAPES])
    out_specs = [rspec(M_V), rspec(S_INNER)] + [sspec(s) for s in STATE_SHAPES]
    out_shapes = ([jax.ShapeDtypeStruct((nseq, SEQ_PAD, M_V), F32),
                   jax.ShapeDtypeStruct((nseq, SEQ_PAD, S_INNER), F32)]
                  + [jax.ShapeDtypeStruct((nseq,) + s, F32) for s in STATE_SHAPES])
    outs = pl.pallas_call(
        _sample_recur_body, grid=(nseq // nb,), in_specs=in_specs, out_specs=out_specs,
        out_shape=out_shapes, name="sample_recur", compiler_params=_params(1),
    )(q, k, v, gate, xs, bm, cm, *states)
    return outs[0], outs[1], outs[2:]


def _post_body(h_ref, y_ref, xs_ref, so_ref, sz_ref, sg_ref, xn_ref,
               mg_ref, sng_ref, dexp_ref, wpa_ref, wpb_ref, wout_ref, g1_ref, b1_ref, x1_ref):
    h = jnp.concatenate([h_ref[:, t, :] for t in range(DEC_SEQ)], axis=0)
    y = jnp.concatenate([y_ref[:, t, :] for t in range(DEC_SEQ)], axis=0)
    x1_ref[...] = _post_math(h, y, xs_ref[...], so_ref[...], sz_ref[...], sg_ref[...],
                             xn_ref[...], mg_ref, sng_ref, dexp_ref, wpa_ref, wpb_ref, wout_ref,
                             g1_ref, b1_ref)


def _post(h, y, xs, so, sz, sg, xn, wts):
    nseq = h.shape[0]
    nb = SAMPLE_BLOCK

    def sspec(cols):
        return pl.BlockSpec((nb, SEQ_PAD, cols), lambda i: (i, 0, 0))

    def tspec(cols):
        return pl.BlockSpec((nb * DEC_SEQ, cols), lambda i: (i, 0))

    in_specs = [sspec(M_V), sspec(S_INNER), tspec(S_INNER), tspec(M_V), tspec(S_INNER),
                tspec(2 * D_MODEL), tspec(D_MODEL),
                _const_spec((1, M_V)), _const_spec((1, S_INNER)), _const_spec((1, S_INNER)),
                _const_spec((M_V // 2, D_MODEL)), _const_spec((S_INNER // 2, D_MODEL)),
                _const_spec((D_MODEL // 2, D_MODEL)), _const_spec((1, D_MODEL)), _const_spec((1, D_MODEL))]
    return pl.pallas_call(
        _post_body, grid=(nseq // nb,), in_specs=in_specs, out_specs=tspec(D_MODEL),
        out_shape=jax.ShapeDtypeStruct((nseq * DEC_SEQ, D_MODEL), F32), name="sample_post",
        compiler_params=_params(1),
    )(h, y, xs, so, sz, sg, xn, wts["mnorm_g"], wts["snorm_g"], wts["d_exp"],
      wts["w_proj_a"], wts["w_proj_b"], wts["w_out"], wts["ln1_g"], wts["ln1_b"])


def _ffn_body(mode, rows, n_valid, sub_tiles,
              x1_ref, wup_ref, wfc_ref, bfc_ref, wdn_ref, g2_ref, b2_ref, cin_ref,
              y_ref, cout_ref):
    if mode == "seq":
        @pl.when(pl.program_id(1) == 0)
        def _():
            cout_ref[...] = cin_ref[...]

    def conv(u, cols):
        if mode == "tm":
            return _conv_tm(u, cin_ref, cout_ref, wfc_ref, bfc_ref, F_CONV, 1, cols)
        return _conv_carry(u, cout_ref, wfc_ref, bfc_ref, F_CONV, n_valid, cols)

    sub = rows // sub_tiles
    nblk = D_FF // FFN_BLOCK
    x1s = [x1_ref[i * sub:(i + 1) * sub, :] for i in range(sub_tiles)]
    x1bs = [x.astype(BF16) for x in x1s]

    def up_proj(n):
        x1b, lo = x1bs[n // nblk], (n % nblk) * FFN_BLOCK
        return (_wdot(x1b, wup_ref, lo, lo + FFN_BLOCK),
                _wdot(x1b, wup_ref, D_FF + lo, D_FF + lo + FFN_BLOCK))

    total = sub_tiles * nblk
    ups = [up_proj(n) for n in range(min(FFN_AHEAD, total))]
    for i in range(sub_tiles):
        ff = None
        for j in range(nblk):
            n = i * nblk + j
            lo = j * FFN_BLOCK
            ua, ub = ups[n]
            if n + FFN_AHEAD < total:
                ups.append(up_proj(n + FFN_AHEAD))
            va = conv(ua, slice(lo, lo + FFN_BLOCK))
            vb = conv(ub, slice(D_FF + lo, D_FF + lo + FFN_BLOCK))
            act = (_silu(va) * vb).astype(BF16)
            part = _dot(act, pltpu.bitcast(wdn_ref[lo // 2:(lo + FFN_BLOCK) // 2, :], BF16))
            ff = part if ff is None else ff + part
        y = _layer_norm(ALPHA * x1s[i] + ff, g2_ref[...], b2_ref[...])
        if mode == "tm":
            nseq = y_ref.shape[0]
            for t in range(DEC_SEQ):
                y_ref[:, t, :] = y[t * nseq:(t + 1) * nseq]
        else:
            y_ref[i * sub:(i + 1) * sub, :] = y


def _ffn(x1, cin, wts, *, mode, nseq, rows, n_valid, sub_tiles=1):
    assert sub_tiles == 1 or mode == "seq"
    total = x1.shape[0]
    nt = total // (nseq * rows)

    def rspec(cols):
        return pl.BlockSpec((rows, cols), lambda b, c: (b * nt + c, 0))

    if mode == "seq":
        cin_spec = pl.BlockSpec((None, SUBLANES, 2 * D_FF), lambda b, c: (0, 0, 0))
        cout_spec = pl.BlockSpec((None, SUBLANES, 2 * D_FF), lambda b, c: (b, 0, 0))
        cout_shape = jax.ShapeDtypeStruct((nseq, SUBLANES, 2 * D_FF), F32)
        y_spec = rspec(D_MODEL)
        y_shape = jax.ShapeDtypeStruct((total, D_MODEL), F32)
    else:
        nb = rows // DEC_SEQ
        sspec = lambda r, cols: pl.BlockSpec((nb, r, cols), lambda b, c: (b * nt + c, 0, 0))
        cin_spec = sspec(F_CONV - 1, 2 * D_FF)
        cout_spec = sspec(F_CONV - 1, 2 * D_FF)
        cout_shape = jax.ShapeDtypeStruct((total // DEC_SEQ, F_CONV - 1, 2 * D_FF), F32)
        y_spec = sspec(DEC_SEQ, D_MODEL)
        y_shape = jax.ShapeDtypeStruct((total // DEC_SEQ, DEC_SEQ, D_MODEL), F32)
    in_specs = [rspec(D_MODEL), _const_spec((D_MODEL // 2, 2 * D_FF)), _const_spec((F_CONV, 2 * D_FF)),
                _const_spec((1, 2 * D_FF)), _const_spec((D_FF // 2, D_MODEL)),
                _const_spec((1, D_MODEL)), _const_spec((1, D_MODEL)), cin_spec]
    return pl.pallas_call(
        functools.partial(_ffn_body, mode, rows, n_valid, sub_tiles),
        grid=(nseq, nt), in_specs=in_specs, out_specs=[y_spec, cout_spec],
        out_shape=[y_shape, cout_shape],
        name=f"ffn_{mode}_{nseq}", compiler_params=_params(2),
    )(x1, wts["w_up"], wts["w_fconv"], wts["b_fconv"], wts["w_down"], wts["ln2_g"], wts["ln2_b"], cin)


def kernel(x_prompt, x_sample, state_mlstm_conv, state_mlstm_C, state_mlstm_n, state_mlstm_m, state_ssm_conv, state_ssm, state_ffn_conv, meta_tokens, ln0_g, ln0_b, w_in, b_mlstm_if, w_mlstm_conv, b_mlstm_conv, mlstm_norm_g, w_proj_a, w_ssm_conv, b_ssm_conv, ssm_dt_bias, ssm_A_log, ssm_D, ssm_norm_g, w_proj_b, w_out, ln1_g, ln1_b, w_up, w_ffn_conv, b_ffn_conv, w_down, ln2_g, ln2_b):
    batch, seq, _ = x_prompt.shape
    dec_batch, dec_seq, _ = x_sample.shape
    prompt_tile = 2 * CHUNK
    ffn_tile = 2 * CHUNK
    assert dec_seq == DEC_SEQ and seq % (FFN_SUB_TILES * ffn_tile) == 0 and meta_tokens.shape[0] == N_META
    assert dec_batch % SAMPLE_SEQS_PER_STEP == 0 and dec_batch % SAMPLE_BLOCK == 0

    w_big, w_small = _pack_in(w_in[0].T)
    lane_pad = jnp.zeros((LANES - LANE_END,), F32)
    b_small = jnp.concatenate([b_mlstm_if[0], ssm_dt_bias[0], ssm_dt_bias[0], lane_pad])[None]
    a_log = jnp.concatenate([jnp.zeros((LANE_DTA,), F32), ssm_A_log[0],
                             jnp.zeros((LANES - LANE_DT,), F32)])[None]
    wts = {
        "ln0_g": ln0_g[None], "ln0_b": ln0_b[None], "w_big": w_big, "w_small": w_small,
        "b_small": b_small, "a_log": a_log,
        "w_mconv": w_mlstm_conv[0], "b_mconv": b_mlstm_conv, "w_sconv": w_ssm_conv[0], "b_sconv": b_ssm_conv,
        "mnorm_g": mlstm_norm_g, "snorm_g": ssm_norm_g, "d_exp": jnp.repeat(ssm_D[0], S_HEADDIM)[None],
        "w_proj_a": _pack_weight(w_proj_a[0], "proj_a"), "w_proj_b": _pack_weight(w_proj_b[0], "proj_b"),
        "w_out": _pack_weight(w_out[0], "out"), "ln1_g": ln1_g, "ln1_b": ln1_b,
        "w_up": _pack_weight(w_up[0], "up"), "w_fconv": w_ffn_conv[0], "b_fconv": b_ffn_conv,
        "w_down": _pack_weight(w_down[0], "down"), "ln2_g": ln2_g, "ln2_b": ln2_b,
    }

    x_meta = jnp.pad(meta_tokens, ((0, CHUNK - N_META), (0, 0)))
    zero_states = [jnp.zeros((1,) + s, F32) for s in STATE_SHAPES]
    x1_m, tails_m, states_m = _mixer(
        x_meta, jnp.zeros((1, SUBLANES, 2 * M_QK), F32), jnp.zeros((1, SUBLANES, S_XBC), F32),
        zero_states, wts, nseq=1, rows=CHUNK, n_valid=N_META)
    _, tail_ffn_m = _ffn(x1_m, jnp.zeros((1, SUBLANES, 2 * D_FF), F32), wts, mode="seq", nseq=1,
                         rows=CHUNK, n_valid=N_META)

    x1_p, tails_p, states_p = _mixer(
        x_prompt.reshape(batch * seq, D_MODEL), tails_m[0], tails_m[1], states_m, wts,
        nseq=batch, rows=prompt_tile, n_valid=prompt_tile)
    y_p, tail_ffn_p = _ffn(x1_p, tail_ffn_m, wts, mode="seq", nseq=batch, rows=FFN_SUB_TILES * ffn_tile,
                           n_valid=ffn_tile, sub_tiles=FFN_SUB_TILES)

    def rows_first(a):
        return jnp.swapaxes(a, 0, 1)

    acts, (xn, xs_tm, so, sz, sg), (new_qk, new_xbc) = _in_proj(
        x_sample, rows_first(state_mlstm_conv[0]), rows_first(state_ssm_conv[0]), wts)
    m_in = jnp.pad(state_mlstm_m[0], ((0, 0), (0, LANES - M_HEADS)))[:, None, :]
    h_s, yss_s, states_s = _sample_recur(
        acts, (state_mlstm_C[0], state_mlstm_n[0], m_in, state_ssm[0].reshape(dec_batch, S_INNER, S_STATE)),
        nseq=dec_batch)
    x1_s = _post(h_s, yss_s, xs_tm, so, sz, sg, xn, wts)
    y_s, new_ffn = _ffn(x1_s, state_ffn_conv[0], wts, mode="tm", nseq=1,
                        rows=SAMPLE_BLOCK * DEC_SEQ, n_valid=SAMPLE_BLOCK * DEC_SEQ)

    def pack_states(convs, states, n):
        c1, n1, m1, s1 = states
        return (convs[0][None], c1[None], n1[None], m1[:, 0, :M_HEADS][None], convs[1][None],
                s1.reshape(n, S_HEADS, S_HEADDIM, S_STATE)[None], convs[2][None])

    def tail_rows(t, width):
        return t[:, SUBLANES - (width - 1):, :]

    p_out = pack_states((tail_rows(tails_p[0], M_CONV), tail_rows(tails_p[1], S_CONV),
                         tail_rows(tail_ffn_p, F_CONV)), states_p, batch)
    s_out = pack_states((rows_first(new_qk), rows_first(new_xbc), new_ffn), states_s, dec_batch)
    y_prompt = y_p.reshape(batch, seq, D_MODEL)
    return (y_prompt, y_s) + p_out + s_out
```

```python
import functools

import jax
import jax.numpy as jnp
from jax import lax
from jax.experimental import pallas as pl
from jax.experimental.pallas import tpu as pltpu

F32 = jnp.float32
BF16 = jnp.bfloat16

D_MODEL = 1024
N_META = 16
M_HEADS = 4
M_DK = 128
M_DV = 256
M_QK = M_HEADS * M_DK
M_V = M_HEADS * M_DV
M_CONV = 4
S_INNER = 2048
S_HEADDIM = 64
S_HEADS = 32
S_GROUPS = 4
S_STATE = 128
S_CONV = 4
S_BC = S_GROUPS * S_STATE
S_XBC = S_INNER + 2 * S_BC
S_GROUP_W = S_INNER // S_GROUPS
S_HEADS_PER_GROUP = S_HEADS // S_GROUPS
D_FF = 2816
F_CONV = 3
ALPHA = 2.0 ** 0.25
LN_EPS = 1e-5
RMS_EPS = 1e-5

LANES = 128
SUBLANES = 8
CHUNK = 128
SEQ_PAD = SUBLANES
DEC_SEQ = 4
SAMPLE_SEQS_PER_STEP = 8
SAMPLE_SUB = 16
SAMPLE_BLOCK = 32
PACK_COLS = 512
FFN_BLOCK = 256
FFN_AHEAD = 3
FFN_SUB_TILES = 4
FILLER_SLOTS = 2
VMEM_LIMIT_BYTES = 60 * 1024 * 1024

OFF_QK = 0
OFF_V = OFF_QK + 2 * M_QK
OFF_O = OFF_V + M_V
OFF_Z = OFF_O + M_V
OFF_XBC = OFF_Z + S_INNER
OFF_GATE = OFF_XBC + S_XBC
W_BIG_COLS = OFF_GATE + 2 * D_MODEL
LANE_IG = 0
LANE_LF = M_HEADS
LANE_DTA = 2 * M_HEADS
LANE_DT = LANE_DTA + S_HEADS
LANE_END = LANE_DT + S_HEADS

IN_I = 2 * M_QK + 2 * M_V
IN_Z = IN_I + 2 * M_HEADS
IN_DT = IN_Z + S_INNER + S_XBC
IN_GATE = IN_DT + S_HEADS
D_IN = IN_GATE + 2 * D_MODEL

STATE_SHAPES = [(M_HEADS, M_DK, M_DV), (M_HEADS, M_DK), (1, LANES), (S_INNER, S_STATE)]


def _softplus(x):
    return jnp.maximum(x, 0.0) + jnp.log1p(jnp.exp(-jnp.abs(x)))


def _silu(x):
    return x * jax.nn.sigmoid(x)


def _layer_norm(x, g, b):
    mu = jnp.mean(x, axis=-1, keepdims=True)
    xc = x - mu
    var = jnp.mean(xc * xc, axis=-1, keepdims=True)
    return xc * lax.rsqrt(var + LN_EPS) * g + b


def _dot(a, b):
    return jnp.dot(a, b, preferred_element_type=F32)


def _wdot(a, w_ref, lo=None, hi=None):
    w = w_ref[...] if lo is None else w_ref[:, lo:hi]
    return _dot(a, pltpu.bitcast(w, BF16))


def _dot_nt(a, b):
    return lax.dot_general(a, b, (((1,), (1,)), ((), ())), preferred_element_type=F32)


def _dot_tn(a, b):
    return lax.dot_general(a, b, (((0,), (0,)), ((), ())), preferred_element_type=F32)


def _conv_carry(u, carry_ref, w_ref, b_ref, width, n_valid, cols=slice(None)):
    full = jnp.concatenate([carry_ref[:, cols], u], axis=0)
    acc = b_ref[:, cols] + u * w_ref[width - 1:width, cols]
    for k in range(1, width):
        acc = acc + pltpu.roll(full, k, 0)[SUBLANES:] * w_ref[width - 1 - k:width - k, cols]
    carry_ref[:, cols] = u[n_valid - SUBLANES:n_valid]
    return acc


def _time_major(ref):
    return jnp.concatenate([ref[:, t, :] for t in range(DEC_SEQ)], axis=0)


def _store_seq_major(ref, x, fill=None):
    nseq, _, c = ref.shape
    pad = jnp.zeros((1, c), x.dtype) if fill is None else fill
    ref[:, DEC_SEQ:, :] = jnp.broadcast_to(pad[None], (nseq, SEQ_PAD - DEC_SEQ, c))
    for t in range(DEC_SEQ):
        ref[:, t, :] = x[t * nseq:(t + 1) * nseq]


def _conv_tm(u, st_ref, new_ref, w_ref, b_ref, width, row_axis, cols=slice(None)):
    def idx(j):
        return (j, slice(None), cols) if row_axis == 0 else (slice(None), j, cols)

    nseq = st_ref.shape[1 - row_axis]
    full = ([st_ref[idx(j)] for j in range(width - 1)]
            + [u[t * nseq:(t + 1) * nseq] for t in range(DEC_SEQ)])
    outs = []
    for t in range(DEC_SEQ):
        acc = b_ref[:, cols]
        for j in range(width):
            acc = acc + full[t + j] * w_ref[j:j + 1, cols]
        outs.append(acc)
    for j in range(width - 1):
        new_ref[idx(j)] = full[DEC_SEQ + j]
    return jnp.concatenate(outs, axis=0)


def _gate_tile(xb, valid, wsm_ref, bsm_ref, alog_ref):
    g = _wdot(xb, wsm_ref) + bsm_ref[...]
    lane = lax.broadcasted_iota(jnp.int32, (1, LANES), 1)
    sp = _softplus(g)
    lsg = -_softplus(-g)
    a_row = jnp.where((lane >= LANE_DTA) & (lane < LANE_DT), -jnp.exp(alog_ref[...]), 0.0)
    tile = jnp.where(lane < LANE_LF, g,
                     jnp.where(lane < LANE_DTA, lsg,
                               jnp.where(lane < LANE_DT, sp * a_row,
                                         jnp.where(lane < LANE_END, sp, 0.0))))
    if valid is None:
        return tile
    return jnp.where(valid, tile, jnp.where(lane < LANE_LF, -jnp.inf, 0.0))


def _cumsum_rows(x, period):
    pos = lax.broadcasted_iota(jnp.int32, x.shape, 0) & (period - 1)
    k = 1
    while k < period:
        x = x + jnp.where(pos >= k, pltpu.roll(x, k, 0), 0.0)
        k *= 2
    return x


def _per_row(values, sub):
    if len(values) == 1:
        return values[0]
    return jnp.concatenate([jnp.broadcast_to(x, (sub, x.shape[1])) for x in values], axis=0)


def _chunk_recur(q, k, v, gt, xs, bm, cm, states, sub, fillers=None):
    staged = fillers is not None
    fillers = list(fillers or ())

    def fill():
        if fillers:
            fillers.pop(0)()

    def run(vector_stage, matmul_stage, items):
        if staged:
            ctxs = [vector_stage(i) for i in items]
            return [matmul_stage(c) for c in ctxs]
        return [matmul_stage(vector_stage(i)) for i in items]

    nsub = len(states)
    assert nsub * sub == CHUNK == q.shape[0]
    sub_rows = [slice(i * sub, (i + 1) * sub) for i in range(nsub)]
    lane = lax.broadcasted_iota(jnp.int32, (1, LANES), 1)
    cum_lanes = (lane >= LANE_LF) & (lane < LANE_DT)
    gc = jnp.where(cum_lanes, _cumsum_rows(jnp.where(cum_lanes, gt, 0.0), sub), gt)
    gct = gc.T
    lasts = [gc[r.stop - 1:r.stop, :] for r in sub_rows]
    last_rows = _per_row(lasts, sub)

    def head_dots(h):
        qh = q[:, h * M_DK:(h + 1) * M_DK]
        c0s = [st_in[0][h] for st_in, _ in states]
        qc = [_dot(qh[r], c0.astype(BF16)) for r, c0 in zip(sub_rows, c0s)]
        return _dot_nt(qh, k[:, h * M_DK:(h + 1) * M_DK]), jnp.concatenate(qc, axis=0), c0s

    def group_dots(g):
        cg = cm[:, g * S_STATE:(g + 1) * S_STATE]
        s_olds = [st_in[3][g * S_GROUP_W:(g + 1) * S_GROUP_W, :] for st_in, _ in states]
        yi = [_dot_nt(cg[r], s_old.astype(BF16)) for r, s_old in zip(sub_rows, s_olds)]
        return (_dot_nt(cg, bm[:, g * S_STATE:(g + 1) * S_STATE]),
                jnp.concatenate(yi, axis=0), s_olds)

    if staged:
        head_pre = [head_dots(h) for h in range(M_HEADS)]
        group_pre = [group_dots(g) for g in range(S_GROUPS)]
        head_dots = head_pre.__getitem__
        group_dots = group_pre.__getitem__
    fill()

    ti = lax.broadcasted_iota(jnp.int32, (CHUNK, CHUNK), 0)
    si = lax.broadcasted_iota(jnp.int32, (CHUNK, CHUNK), 1)
    causal = si <= ti
    if nsub > 1:
        causal = causal & ((si & -sub) == (ti & -sub))
    sub_lanes = [(lane >= r.start) & (lane < r.stop) for r in sub_rows]
    m_rows = [st_in[2][...] for st_in, _ in states]

    def head_vector(h):
        qh = q[:, h * M_DK:(h + 1) * M_DK]
        bt_c = gc[:, LANE_LF + h:LANE_LF + h + 1]
        ig_c = gc[:, LANE_IG + h:LANE_IG + h + 1]
        bt_r = gct[LANE_LF + h:LANE_LF + h + 1, :]
        ig_r = gct[LANE_IG + h:LANE_IG + h + 1, :]
        m0s = [m_row[:, h:h + 1] for m_row in m_rows]
        n0s = [st_in[1][h:h + 1, :] for st_in, _ in states]
        m0 = _per_row(m0s, sub)
        sqk, qc, c0s = head_dots(h)

        dmat = jnp.where(causal, bt_c - bt_r + ig_r, -jnp.inf)
        inter = bt_c + m0
        m_t = jnp.maximum(inter, jnp.max(dmat, axis=1, keepdims=True))
        w_intra = jnp.exp(dmat - m_t)
        w_inter = jnp.exp(inter - m_t)
        s = sqk * w_intra
        den = (jnp.sum(s, axis=1, keepdims=True)
               + jnp.sum(qh.astype(F32) * _per_row(n0s, sub), axis=1, keepdims=True) * w_inter)
        rdenom = 1.0 / jnp.maximum(jnp.abs(den), jnp.exp(-m_t))

        updates = []
        for i, r in enumerate(sub_rows):
            b_last = lasts[i][:, LANE_LF + h:LANE_LF + h + 1]
            d_last = b_last - bt_r + ig_r
            if nsub > 1:
                d_last = jnp.where(sub_lanes[i], d_last, -jnp.inf)
            m_new = jnp.maximum(b_last + m0s[i], jnp.max(d_last, axis=1, keepdims=True))
            w_last = jnp.exp(b_last - bt_c[r] + ig_c[r] - m_new)
            decay = jnp.exp(b_last + m0s[i] - m_new)
            kw = k[r, h * M_DK:(h + 1) * M_DK].astype(F32) * w_last
            states[i][1][1][h:h + 1, :] = decay * n0s[i] + jnp.sum(kw, axis=0, keepdims=True)
            updates.append((kw.astype(BF16), decay * c0s[i], m_new))
        return h, s.astype(BF16), qc * w_inter, rdenom, updates

    def head_matmul(ctx):
        h, s_b, inter_part, rdenom, updates = ctx
        num = _dot(s_b, v[:, h * M_DV:(h + 1) * M_DV]) + inter_part
        for i, (kw_b, c_decayed, _) in enumerate(updates):
            states[i][1][0][h] = c_decayed + _dot_tn(kw_b, v[sub_rows[i], h * M_DV:(h + 1) * M_DV])
        return num * rdenom, [m_new for _, _, m_new in updates]

    heads = run(head_vector, head_matmul, range(M_HEADS))
    for i in range(nsub):
        m_new_row = m_rows[i]
        for h, (_, m_news) in enumerate(heads):
            m_new_row = jnp.where(lane == h, m_news[i], m_new_row)
        states[i][1][2][...] = m_new_row
    fill()

    lo_half = lax.broadcasted_iota(jnp.int32, (CHUNK, LANES), 1) < S_HEADDIM
    xs_b = xs.astype(BF16)
    e_tile = jnp.exp(gc)
    wd_tile = jnp.exp(last_rows - gc) * pltpu.roll(gc, LANES - (LANE_DT - LANE_DTA), 1)
    hi_half = jnp.where(lo_half, 0, 1)

    def pair_lanes(tile, hh):
        return jnp.take_along_axis(tile, hi_half + (LANE_DTA + hh), axis=1)

    def group_vector(g):
        rows_g = slice(g * S_GROUP_W, (g + 1) * S_GROUP_W)
        cb, y_inter, s_olds = group_dots(g)
        pairs, wd_parts = [], []
        dec_parts = [[] for _ in range(nsub)]
        for p in range(S_HEADS_PER_GROUP // 2):
            ms = []
            for r in (2 * p, 2 * p + 1):
                hh = g * S_HEADS_PER_GROUP + r
                bt_c = gc[:, LANE_DTA + hh:LANE_DTA + hh + 1]
                bt_r = gct[LANE_DTA + hh:LANE_DTA + hh + 1, :]
                dt_r = gct[LANE_DT + hh:LANE_DT + hh + 1, :]
                dec = jnp.exp(jnp.where(causal, bt_c - bt_r, -jnp.inf))
                ms.append((cb * dec * dt_r).astype(BF16))
                for i in range(nsub):
                    dec_parts[i].append(jnp.broadcast_to(
                        jnp.exp(lasts[i][:, LANE_DTA + hh:LANE_DTA + hh + 1]), (S_HEADDIM, 1)))
            col = g * S_GROUP_W + p * LANES
            xp = xs_b[:, col:col + LANES]
            zero = jnp.zeros_like(xp)
            rhs = jnp.concatenate([jnp.where(lo_half, xp, zero), jnp.where(lo_half, zero, xp)], axis=0)
            hh = g * S_HEADS_PER_GROUP + 2 * p
            pairs.append((jnp.concatenate(ms, axis=1), rhs,
                          y_inter[:, p * LANES:(p + 1) * LANES] * pair_lanes(e_tile, hh)))
            wd_parts.append(pair_lanes(wd_tile, hh))
        a_mat = (xs[:, rows_g] * jnp.concatenate(wd_parts, axis=1)).astype(BF16)
        return g, pairs, a_mat, [jnp.concatenate(d, axis=0) * s_old for d, s_old in zip(dec_parts, s_olds)]

    def group_matmul(ctx):
        g, pairs, a_mat, s_decayed = ctx
        ys = [_dot(lhs, rhs) + inter_part for lhs, rhs, inter_part in pairs]
        for i, r in enumerate(sub_rows):
            states[i][1][3][g * S_GROUP_W:(g + 1) * S_GROUP_W, :] = (
                s_decayed[i] + _dot_tn(a_mat[r], bm[r, g * S_STATE:(g + 1) * S_STATE]))
        return jnp.concatenate(ys, axis=1)

    groups = run(group_vector, group_matmul, range(S_GROUPS))
    return jnp.concatenate([hp for hp, _ in heads], axis=1), jnp.concatenate(groups, axis=1)


def _post_math(h, y, xs, so, sz, sg, xn, mg_ref, sng_ref, dexp_ref, wpa_ref, wpb_ref, wout_ref,
               g1_ref, b1_ref):
    hs = []
    for i in range(M_HEADS):
        hh = h[:, i * M_DV:(i + 1) * M_DV]
        mu = jnp.mean(hh, axis=-1, keepdims=True)
        hc = hh - mu
        var = jnp.mean(hc * hc, axis=-1, keepdims=True)
        hs.append(hc * lax.rsqrt(var + LN_EPS))
    hn = jnp.concatenate(hs, axis=1) * mg_ref[...]
    ya = _wdot((so * hn).astype(BF16), wpa_ref)

    y = (y + dexp_ref[...] * xs) * sz
    ys = []
    for g in range(S_GROUPS):
        yg = y[:, g * S_GROUP_W:(g + 1) * S_GROUP_W]
        ys.append(yg * lax.rsqrt(jnp.mean(yg * yg, axis=-1, keepdims=True) + RMS_EPS))
    yn = jnp.concatenate(ys, axis=1) * sng_ref[...]
    yb = _wdot(yn.astype(BF16), wpb_ref)

    mixed = _wdot((sg[:, :D_MODEL] * ya + sg[:, D_MODEL:] * yb).astype(BF16), wout_ref)
    return _layer_norm(ALPHA * xn + mixed, g1_ref[...], b1_ref[...])


def _const_spec(shape):
    nd = len(shape)
    return pl.BlockSpec(shape, lambda *_: (0,) * nd, pipeline_mode=pl.Buffered(1))


def _pack_body(w_ref, o_ref):
    o_ref[...] = pltpu.bitcast(w_ref[...].astype(BF16), jnp.uint32)


def _pack_weight(w, name):
    k, n = w.shape
    bn = min(n, PACK_COLS)
    return pl.pallas_call(
        _pack_body, grid=(n // bn,), in_specs=[pl.BlockSpec((k, bn), lambda j: (0, j))],
        out_specs=pl.BlockSpec((k // 2, bn), lambda j: (0, j)),
        out_shape=jax.ShapeDtypeStruct((k // 2, n), jnp.uint32), name=f"pack_{name}",
        compiler_params=_params(1),
    )(w)


def _pack_t_body(wt_ref, o_ref):
    o_ref[...] = pltpu.bitcast(wt_ref[...].T.astype(BF16), jnp.uint32)


def _pack_gates_body(wif_ref, wdt_ref, o_ref):
    wdt = wdt_ref[...]
    rows = jnp.concatenate([wif_ref[...], wdt, wdt, jnp.zeros((LANES - LANE_END, D_MODEL), F32)], axis=0)
    o_ref[...] = pltpu.bitcast(rows.T.astype(BF16), jnp.uint32)


def _pack_in(wt):
    k = wt.shape[1]
    assert wt.shape[0] == D_IN
    starts = (0, IN_Z - IN_I, IN_GATE - IN_DT + IN_Z - IN_I)
    lens = (IN_I, IN_DT - IN_Z, D_IN - IN_GATE)
    assert all(s % SUBLANES == 0 for s in starts) and all(n % PACK_COLS == 0 for n in lens)
    b1 = lens[0] // PACK_COLS
    b2 = b1 + lens[1] // PACK_COLS

    def src_row(j):
        s0, s1, s2 = (s // SUBLANES for s in starts)
        return (j * (PACK_COLS // SUBLANES) + jnp.where(j < b1, s0, jnp.where(j < b2, s1, s2))) * SUBLANES

    big = pl.pallas_call(
        _pack_t_body, grid=(W_BIG_COLS // PACK_COLS,),
        in_specs=[pl.BlockSpec((pl.Element(PACK_COLS), pl.Element(k)), lambda j: (src_row(j), 0))],
        out_specs=pl.BlockSpec((k // 2, PACK_COLS), lambda j: (0, j)),
        out_shape=jax.ShapeDtypeStruct((k // 2, W_BIG_COLS), jnp.uint32),
        name="pack_in", compiler_params=_params(1),
    )(wt)
    small = pl.pallas_call(
        _pack_gates_body, out_shape=jax.ShapeDtypeStruct((k // 2, LANES), jnp.uint32), name="pack_gates",
    )(wt[IN_I:IN_Z], wt[IN_DT:IN_GATE])
    return big, small


def _params(dims):
    return pltpu.CompilerParams(dimension_semantics=("arbitrary",) * dims,
                                vmem_limit_bytes=VMEM_LIMIT_BYTES)


def _mixer_body(rows, n_valid,
                x_ref, g0_ref, b0_ref, wbig_ref, wsm_ref, bsm_ref, alog_ref,
                wmc_ref, bmc_ref, wsc_ref, bsc_ref, cin_qk_ref, cin_xbc_ref,
                c0_ref, n0_ref, m0_ref, s0_ref,
                mg_ref, sng_ref, dexp_ref, wpa_ref, wpb_ref, wout_ref, g1_ref, b1_ref,
                x1_ref, cout_qk_ref, cout_xbc_ref, c1_ref, n1_ref, m1_ref, s1_ref):
    @pl.when(pl.program_id(1) == 0)
    def _():
        cout_qk_ref[...] = cin_qk_ref[...]
        cout_xbc_ref[...] = cin_xbc_ref[...]
        c1_ref[...] = c0_ref[...]
        n1_ref[...] = n0_ref[...]
        m1_ref[...] = m0_ref[...]
        s1_ref[...] = s0_ref[...]

    xn = _layer_norm(x_ref[...], g0_ref[...], b0_ref[...])
    xb = xn.astype(BF16)
    valid = None
    if n_valid < rows:
        valid = lax.broadcasted_iota(jnp.int32, (rows, 1), 0) < n_valid
    gt = _gate_tile(xb, valid, wsm_ref, bsm_ref, alog_ref)

    u_qk = _wdot(xb, wbig_ref, OFF_QK, OFF_QK + 2 * M_QK)
    qk = _silu(_conv_carry(u_qk, cout_qk_ref, wmc_ref, bmc_ref, M_CONV, n_valid))
    q = (qk[:, :M_QK] * (M_DK ** -0.5)).astype(BF16)
    k = qk[:, M_QK:].astype(BF16)
    v = _wdot(xb, wbig_ref, OFF_V, OFF_V + M_V).astype(BF16)
    u_xbc = _wdot(xb, wbig_ref, OFF_XBC, OFF_XBC + S_XBC)
    xbc = _silu(_conv_carry(u_xbc, cout_xbc_ref, wsc_ref, bsc_ref, S_CONV, n_valid))
    xs = xbc[:, :S_INNER]
    bm = xbc[:, S_INNER:S_INNER + S_BC].astype(BF16)
    cm = xbc[:, S_INNER + S_BC:].astype(BF16)

    offs = [OFF_O, OFF_Z, OFF_Z + D_MODEL, OFF_GATE, OFF_GATE + D_MODEL]
    dense = {}
    nchunk = rows // CHUNK
    nslot = nchunk * FILLER_SLOTS
    hs, ys = [], []
    for ci in range(nchunk):
        sl = slice(ci * CHUNK, (ci + 1) * CHUNK)
        state = (c1_ref, n1_ref, m1_ref, s1_ref)

        def filler(slot):
            def issue():
                for off in offs[slot::nslot]:
                    dense[off] = _wdot(xb, wbig_ref, off, off + D_MODEL)
            return issue

        h, y = _chunk_recur(q[sl], k[sl], v[sl], gt[sl], xs[sl], bm[sl], cm[sl], [(state, state)], CHUNK,
                            [filler(ci * FILLER_SLOTS + i) for i in range(FILLER_SLOTS)])
        hs.append(h)
        ys.append(y)
    h = jnp.concatenate(hs, axis=0)
    y = jnp.concatenate(ys, axis=0)

    so = jax.nn.sigmoid(dense[OFF_O])
    sz = _silu(jnp.concatenate([dense[OFF_Z], dense[OFF_Z + D_MODEL]], axis=1))
    sg = jax.nn.sigmoid(jnp.concatenate([dense[OFF_GATE], dense[OFF_GATE + D_MODEL]], axis=1))
    x1_ref[...] = _post_math(h, y, xs, so, sz, sg, xn, mg_ref, sng_ref, dexp_ref,
                             wpa_ref, wpb_ref, wout_ref, g1_ref, b1_ref)


def _mixer(x_rows, cin_qk, cin_xbc, states, wts, *, nseq, rows, n_valid):
    total = x_rows.shape[0]
    nt = total // (nseq * rows)
    rspec = pl.BlockSpec((rows, D_MODEL), lambda b, c: (b * nt + c, 0))

    def per_seq(shape):
        nd = len(shape)
        return pl.BlockSpec((None,) + shape, lambda b, c: (b,) + (0,) * nd)

    def shared(shape):
        nd = len(shape)
        return pl.BlockSpec((None,) + shape, lambda b, c: (0,) * (nd + 1), pipeline_mode=pl.Buffered(1))

    tails = [(SUBLANES, 2 * M_QK), (SUBLANES, S_XBC)]
    in_specs = ([rspec, _const_spec((1, D_MODEL)), _const_spec((1, D_MODEL)),
                 _const_spec((D_MODEL // 2, W_BIG_COLS)), _const_spec((D_MODEL // 2, LANES)),
                 _const_spec((1, LANES)), _const_spec((1, LANES)),
                 _const_spec((M_CONV, 2 * M_QK)), _const_spec((1, 2 * M_QK)),
                 _const_spec((S_CONV, S_XBC)), _const_spec((1, S_XBC))]
                + [shared(s) for s in tails + STATE_SHAPES]
                + [_const_spec((1, M_V)), _const_spec((1, S_INNER)), _const_spec((1, S_INNER)),
                   _const_spec((M_V // 2, D_MODEL)), _const_spec((S_INNER // 2, D_MODEL)),
                   _const_spec((D_MODEL // 2, D_MODEL)), _const_spec((1, D_MODEL)), _const_spec((1, D_MODEL))])
    out_specs = [rspec] + [per_seq(s) for s in tails + STATE_SHAPES]
    out_shapes = ([jax.ShapeDtypeStruct((total, D_MODEL), F32)]
                  + [jax.ShapeDtypeStruct((nseq,) + s, F32) for s in tails + STATE_SHAPES])
    outs = pl.pallas_call(
        functools.partial(_mixer_body, rows, n_valid),
        grid=(nseq, nt), in_specs=in_specs, out_specs=out_specs, out_shape=out_shapes,
        name=f"mixer_{nseq}", compiler_params=_params(2),
    )(x_rows, wts["ln0_g"], wts["ln0_b"], wts["w_big"], wts["w_small"], wts["b_small"], wts["a_log"],
      wts["w_mconv"], wts["b_mconv"], wts["w_sconv"], wts["b_sconv"], cin_qk, cin_xbc, *states,
      wts["mnorm_g"], wts["snorm_g"], wts["d_exp"], wts["w_proj_a"], wts["w_proj_b"], wts["w_out"],
      wts["ln1_g"], wts["ln1_b"])
    return outs[0], outs[1:3], outs[3:]


def _in_proj_body(x_ref, g0_ref, b0_ref, wbig_ref, wsm_ref, bsm_ref, alog_ref,
                  wmc_ref, bmc_ref, wsc_ref, bsc_ref, st_qk_ref, st_xbc_ref,
                  q_ref, k_ref, v_ref, gate_ref, xs_ref, bm_ref, cm_ref,
                  xn_ref, xs_tm_ref, so_ref, sz_ref, sg_ref, new_qk_ref, new_xbc_ref):
    xn = _layer_norm(_time_major(x_ref), g0_ref[...], b0_ref[...])
    xn_ref[...] = xn
    xb = xn.astype(BF16)
    lane = lax.broadcasted_iota(jnp.int32, (1, LANES), 1)
    _store_seq_major(gate_ref, _gate_tile(xb, None, wsm_ref, bsm_ref, alog_ref),
                     jnp.where(lane < LANE_LF, -jnp.inf, 0.0))

    u_qk = _wdot(xb, wbig_ref, OFF_QK, OFF_QK + 2 * M_QK)
    qk = _silu(_conv_tm(u_qk, st_qk_ref, new_qk_ref, wmc_ref, bmc_ref, M_CONV, 0))
    _store_seq_major(q_ref, qk[:, :M_QK] * (M_DK ** -0.5))
    _store_seq_major(k_ref, qk[:, M_QK:])
    _store_seq_major(v_ref, _wdot(xb, wbig_ref, OFF_V, OFF_V + M_V))
    so_ref[...] = jax.nn.sigmoid(_wdot(xb, wbig_ref, OFF_O, OFF_O + M_V))
    sz_ref[...] = _silu(_wdot(xb, wbig_ref, OFF_Z, OFF_Z + S_INNER))
    u_xbc = _wdot(xb, wbig_ref, OFF_XBC, OFF_XBC + S_XBC)
    xbc = _silu(_conv_tm(u_xbc, st_xbc_ref, new_xbc_ref, wsc_ref, bsc_ref, S_CONV, 0))
    xs_tm_ref[...] = xbc[:, :S_INNER]
    _store_seq_major(xs_ref, xbc[:, :S_INNER])
    _store_seq_major(bm_ref, xbc[:, S_INNER:S_INNER + S_BC])
    _store_seq_major(cm_ref, xbc[:, S_INNER + S_BC:])
    sg_ref[...] = jax.nn.sigmoid(_wdot(xb, wbig_ref, OFF_GATE, OFF_GATE + 2 * D_MODEL))


def _in_proj(x, st_qk, st_xbc, wts):
    nseq = x.shape[0]
    nb = SAMPLE_BLOCK

    def rows_first(width, cols):
        return pl.BlockSpec((width - 1, nb, cols), lambda i: (0, i, 0))

    def sspec(rows, cols):
        return pl.BlockSpec((nb, rows, cols), lambda i: (i, 0, 0))

    def tspec(cols):
        return pl.BlockSpec((nb * DEC_SEQ, cols), lambda i: (i, 0))

    in_specs = [sspec(DEC_SEQ, D_MODEL), _const_spec((1, D_MODEL)), _const_spec((1, D_MODEL)),
                _const_spec((D_MODEL // 2, W_BIG_COLS)), _const_spec((D_MODEL // 2, LANES)),
                _const_spec((1, LANES)), _const_spec((1, LANES)),
                _const_spec((M_CONV, 2 * M_QK)), _const_spec((1, 2 * M_QK)),
                _const_spec((S_CONV, S_XBC)), _const_spec((1, S_XBC)),
                rows_first(M_CONV, 2 * M_QK), rows_first(S_CONV, S_XBC)]
    seq_cols = [M_QK, M_QK, M_V, LANES, S_INNER, S_BC, S_BC]
    tm_cols = [D_MODEL, S_INNER, M_V, S_INNER, 2 * D_MODEL]
    outs = pl.pallas_call(
        _in_proj_body, grid=(nseq // nb,), in_specs=in_specs,
        out_specs=([sspec(SEQ_PAD, c) for c in seq_cols] + [tspec(c) for c in tm_cols]
                   + [rows_first(M_CONV, 2 * M_QK), rows_first(S_CONV, S_XBC)]),
        out_shape=([jax.ShapeDtypeStruct((nseq, SEQ_PAD, c), F32) for c in seq_cols]
                   + [jax.ShapeDtypeStruct((nseq * DEC_SEQ, c), F32) for c in tm_cols]
                   + [jax.ShapeDtypeStruct((M_CONV - 1, nseq, 2 * M_QK), F32),
                      jax.ShapeDtypeStruct((S_CONV - 1, nseq, S_XBC), F32)]),
        name="sample_in_proj", compiler_params=_params(1),
    )(x, wts["ln0_g"], wts["ln0_b"], wts["w_big"], wts["w_small"], wts["b_small"], wts["a_log"],
      wts["w_mconv"], wts["b_mconv"], wts["w_sconv"], wts["b_sconv"], st_qk, st_xbc)
    return outs[:7], outs[7:12], outs[12:]


def _sample_recur_body(q_ref, k_ref, v_ref, gate_ref, xs_ref, bm_ref, cm_ref,
                       c0_ref, n0_ref, m0_ref, s0_ref,
                       h_ref, y_ref, c1_ref, n1_ref, m1_ref, s1_ref):
    nb = SAMPLE_SEQS_PER_STEP
    lane = lax.broadcasted_iota(jnp.int32, (1, LANES), 1)

    def stack(ref, dtype, fill=None):
        x = ref[...]
        c = x.shape[2]
        pad = jnp.zeros((1, 1, c), F32) if fill is None else fill[None]
        x = jnp.concatenate([x, jnp.broadcast_to(pad, (nb, SAMPLE_SUB - SEQ_PAD, c))], axis=1)
        return x.reshape(nb * SAMPLE_SUB, c).astype(dtype)

    def unstack(x):
        return x.reshape(nb, SAMPLE_SUB, x.shape[1])[:, :SEQ_PAD, :]

    states = [((c0_ref.at[i], n0_ref.at[i], m0_ref.at[i], s0_ref.at[i]),
               (c1_ref.at[i], n1_ref.at[i], m1_ref.at[i], s1_ref.at[i])) for i in range(nb)]
    h, y = _chunk_recur(stack(q_ref, BF16), stack(k_ref, BF16), stack(v_ref, BF16),
                        stack(gate_ref, F32, jnp.where(lane < LANE_LF, -jnp.inf, 0.0)),
                        stack(xs_ref, F32), stack(bm_ref, BF16), stack(cm_ref, BF16), states, SAMPLE_SUB)
    h_ref[...] = unstack(h)
    y_ref[...] = unstack(y)


def _sample_recur(acts, states, *, nseq):
    q, k, v, gate, xs, bm, cm = acts
    nb = SAMPLE_SEQS_PER_STEP
    assert nb * SAMPLE_SUB == CHUNK

    def rspec(cols):
        return pl.BlockSpec((nb, SEQ_PAD, cols), lambda i: (i, 0, 0))

    def sspec(shape):
        nd = len(shape)
        return pl.BlockSpec((nb,) + shape, lambda i: (i,) + (0,) * nd)

    in_specs = ([rspec(M_QK), rspec(M_QK), rspec(M_V), rspec(LANES), rspec(S_INNER), rspec(S_BC), rspec(S_BC)]
                + [sspec(s) for s in STATE_SHAPES])
    out_specs = [rspec(M_V), rspec(S_INNER)] + [sspec(s) for s in STATE_SHAPES]
    out_shapes = ([jax.ShapeDtypeStruct((nseq, SEQ_PAD, M_V), F32),
                   jax.ShapeDtypeStruct((nseq, SEQ_PAD, S_INNER), F32)]
                  + [jax.ShapeDtypeStruct((nseq,) + s, F32) for s in STATE_SHAPES])
    outs = pl.pallas_call(
        _sample_recur_body, grid=(nseq // nb,), in_specs=in_specs, out_specs=out_specs,
        out_shape=out_shapes, name="sample_recur", compiler_params=_params(1),
    )(q, k, v, gate, xs, bm, cm, *states)
    return outs[0], outs[1], outs[2:]


def _post_body(h_ref, y_ref, xs_ref, so_ref, sz_ref, sg_ref, xn_ref,
               mg_ref, sng_ref, dexp_ref, wpa_ref, wpb_ref, wout_ref, g1_ref, b1_ref, x1_ref):
    h = jnp.concatenate([h_ref[:, t, :] for t in range(DEC_SEQ)], axis=0)
    y = jnp.concatenate([y_ref[:, t, :] for t in range(DEC_SEQ)], axis=0)
    x1_ref[...] = _post_math(h, y, xs_ref[...], so_ref[...], sz_ref[...], sg_ref[...],
                             xn_ref[...], mg_ref, sng_ref, dexp_ref, wpa_ref, wpb_ref, wout_ref,
                             g1_ref, b1_ref)


def _post(h, y, xs, so, sz, sg, xn, wts):
    nseq = h.shape[0]
    nb = SAMPLE_BLOCK

    def sspec(cols):
        return pl.BlockSpec((nb, SEQ_PAD, cols), lambda i: (i, 0, 0))

    def tspec(cols):
        return pl.BlockSpec((nb * DEC_SEQ, cols), lambda i: (i, 0))

    in_specs = [sspec(M_V), sspec(S_INNER), tspec(S_INNER), tspec(M_V), tspec(S_INNER),
                tspec(2 * D_MODEL), tspec(D_MODEL),
                _const_spec((1, M_V)), _const_spec((1, S_INNER)), _const_spec((1, S_INNER)),
                _const_spec((M_V // 2, D_MODEL)), _const_spec((S_INNER // 2, D_MODEL)),
                _const_spec((D_MODEL // 2, D_MODEL)), _const_spec((1, D_MODEL)), _const_spec((1, D_MODEL))]
    return pl.pallas_call(
        _post_body, grid=(nseq // nb,), in_specs=in_specs, out_specs=tspec(D_MODEL),
        out_shape=jax.ShapeDtypeStruct((nseq * DEC_SEQ, D_MODEL), F32), name="sample_post",
        compiler_params=_params(1),
    )(h, y, xs, so, sz, sg, xn, wts["mnorm_g"], wts["snorm_g"], wts["d_exp"],
      wts["w_proj_a"], wts["w_proj_b"], wts["w_out"], wts["ln1_g"], wts["ln1_b"])


def _ffn_body(mode, rows, n_valid, sub_tiles,
              x1_ref, wup_ref, wfc_ref, bfc_ref, wdn_ref, g2_ref, b2_ref, cin_ref,
              y_ref, cout_ref):
    if mode == "seq":
        @pl.when(pl.program_id(1) == 0)
        def _():
            cout_ref[...] = cin_ref[...]

    def conv(u, cols):
        if mode == "tm":
            return _conv_tm(u, cin_ref, cout_ref, wfc_ref, bfc_ref, F_CONV, 1, cols)
        return _conv_carry(u, cout_ref, wfc_ref, bfc_ref, F_CONV, n_valid, cols)

    sub = rows // sub_tiles
    nblk = D_FF // FFN_BLOCK
    x1s = [x1_ref[i * sub:(i + 1) * sub, :] for i in range(sub_tiles)]
    x1bs = [x.astype(BF16) for x in x1s]

    def up_proj(n):
        x1b, lo = x1bs[n // nblk], (n % nblk) * FFN_BLOCK
        return (_wdot(x1b, wup_ref, lo, lo + FFN_BLOCK),
                _wdot(x1b, wup_ref, D_FF + lo, D_FF + lo + FFN_BLOCK))

    total = sub_tiles * nblk
    ups = [up_proj(n) for n in range(min(FFN_AHEAD, total))]
    for i in range(sub_tiles):
        ff = None
        for j in range(nblk):
            n = i * nblk + j
            lo = j * FFN_BLOCK
            ua, ub = ups[n]
            if n + FFN_AHEAD < total:
                ups.append(up_proj(n + FFN_AHEAD))
            va = conv(ua, slice(lo, lo + FFN_BLOCK))
            vb = conv(ub, slice(D_FF + lo, D_FF + lo + FFN_BLOCK))
            act = (_silu(va) * vb).astype(BF16)
            part = _dot(act, pltpu.bitcast(wdn_ref[lo // 2:(lo + FFN_BLOCK) // 2, :], BF16))
            ff = part if ff is None else ff + part
        y = _layer_norm(ALPHA * x1s[i] + ff, g2_ref[...], b2_ref[...])
        if mode == "tm":
            nseq = y_ref.shape[0]
            for t in range(DEC_SEQ):
                y_ref[:, t, :] = y[t * nseq:(t + 1) * nseq]
        else:
            y_ref[i * sub:(i + 1) * sub, :] = y


def _ffn(x1, cin, wts, *, mode, nseq, rows, n_valid, sub_tiles=1):
    assert sub_tiles == 1 or mode == "seq"
    total = x1.shape[0]
    nt = total // (nseq * rows)

    def rspec(cols):
        return pl.BlockSpec((rows, cols), lambda b, c: (b * nt + c, 0))

    if mode == "seq":
        cin_spec = pl.BlockSpec((None, SUBLANES, 2 * D_FF), lambda b, c: (0, 0, 0))
        cout_spec = pl.BlockSpec((None, SUBLANES, 2 * D_FF), lambda b, c: (b, 0, 0))
        cout_shape = jax.ShapeDtypeStruct((nseq, SUBLANES, 2 * D_FF), F32)
        y_spec = rspec(D_MODEL)
        y_shape = jax.ShapeDtypeStruct((total, D_MODEL), F32)
    else:
        nb = rows // DEC_SEQ
        sspec = lambda r, cols: pl.BlockSpec((nb, r, cols), lambda b, c: (b * nt + c, 0, 0))
        cin_spec = sspec(F_CONV - 1, 2 * D_FF)
        cout_spec = sspec(F_CONV - 1, 2 * D_FF)
        cout_shape = jax.ShapeDtypeStruct((total // DEC_SEQ, F_CONV - 1, 2 * D_FF), F32)
        y_spec = sspec(DEC_SEQ, D_MODEL)
        y_shape = jax.ShapeDtypeStruct((total // DEC_SEQ, DEC_SEQ, D_MODEL), F32)
    in_specs = [rspec(D_MODEL), _const_spec((D_MODEL // 2, 2 * D_FF)), _const_spec((F_CONV, 2 * D_FF)),
                _const_spec((1, 2 * D_FF)), _const_spec((D_FF // 2, D_MODEL)),
                _const_spec((1, D_MODEL)), _const_spec((1, D_MODEL)), cin_spec]
    return pl.pallas_call(
        functools.partial(_ffn_body, mode, rows, n_valid, sub_tiles),
        grid=(nseq, nt), in_specs=in_specs, out_specs=[y_spec, cout_spec],
        out_shape=[y_shape, cout_shape],
        name=f"ffn_{mode}_{nseq}", compiler_params=_params(2),
    )(x1, wts["w_up"], wts["w_fconv"], wts["b_fconv"], wts["w_down"], wts["ln2_g"], wts["ln2_b"], cin)


def kernel(x_prompt, x_sample, state_mlstm_conv, state_mlstm_C, state_mlstm_n, state_mlstm_m, state_ssm_conv, state_ssm, state_ffn_conv, meta_tokens, ln0_g, ln0_b, w_in, b_mlstm_if, w_mlstm_conv, b_mlstm_conv, mlstm_norm_g, w_proj_a, w_ssm_conv, b_ssm_conv, ssm_dt_bias, ssm_A_log, ssm_D, ssm_norm_g, w_proj_b, w_out, ln1_g, ln1_b, w_up, w_ffn_conv, b_ffn_conv, w_down, ln2_g, ln2_b):
    batch, seq, _ = x_prompt.shape
    dec_batch, dec_seq, _ = x_sample.shape
    prompt_tile = 2 * CHUNK
    ffn_tile = 2 * CHUNK
    assert dec_seq == DEC_SEQ and seq % (FFN_SUB_TILES * ffn_tile) == 0 and meta_tokens.shape[0] == N_META
    assert dec_batch % SAMPLE_SEQS_PER_STEP == 0 and dec_batch % SAMPLE_BLOCK == 0

    w_big, w_small = _pack_in(w_in[0].T)
    lane_pad = jnp.zeros((LANES - LANE_END,), F32)
    b_small = jnp.concatenate([b_mlstm_if[0], ssm_dt_bias[0], ssm_dt_bias[0], lane_pad])[None]
    a_log = jnp.concatenate([jnp.zeros((LANE_DTA,), F32), ssm_A_log[0],
                             jnp.zeros((LANES - LANE_DT,), F32)])[None]
    wts = {
        "ln0_g": ln0_g[None], "ln0_b": ln0_b[None], "w_big": w_big, "w_small": w_small,
        "b_small": b_small, "a_log": a_log,
        "w_mconv": w_mlstm_conv[0], "b_mconv": b_mlstm_conv, "w_sconv": w_ssm_conv[0], "b_sconv": b_ssm_conv,
        "mnorm_g": mlstm_norm_g, "snorm_g": ssm_norm_g, "d_exp": jnp.repeat(ssm_D[0], S_HEADDIM)[None],
        "w_proj_a": _pack_weight(w_proj_a[0], "proj_a"), "w_proj_b": _pack_weight(w_proj_b[0], "proj_b"),
        "w_out": _pack_weight(w_out[0], "out"), "ln1_g": ln1_g, "ln1_b": ln1_b,
        "w_up": _pack_weight(w_up[0], "up"), "w_fconv": w_ffn_conv[0], "b_fconv": b_ffn_conv,
        "w_down": _pack_weight(w_down[0], "down"), "ln2_g": ln2_g, "ln2_b": ln2_b,
    }

    x_meta = jnp.pad(meta_tokens, ((0, CHUNK - N_META), (0, 0)))
    zero_states = [jnp.zeros((1,) + s, F32) for s in STATE_SHAPES]
    x1_m, tails_m, states_m = _mixer(
        x_meta, jnp.zeros((1, SUBLANES, 2 * M_QK), F32), jnp.zeros((1, SUBLANES, S_XBC), F32),
        zero_states, wts, nseq=1, rows=CHUNK, n_valid=N_META)
    _, tail_ffn_m = _ffn(x1_m, jnp.zeros((1, SUBLANES, 2 * D_FF), F32), wts, mode="seq", nseq=1,
                         rows=CHUNK, n_valid=N_META)

    x1_p, tails_p, states_p = _mixer(
        x_prompt.reshape(batch * seq, D_MODEL), tails_m[0], tails_m[1], states_m, wts,
        nseq=batch, rows=prompt_tile, n_valid=prompt_tile)
    y_p, tail_ffn_p = _ffn(x1_p, tail_ffn_m, wts, mode="seq", nseq=batch, rows=FFN_SUB_TILES * ffn_tile,
                           n_valid=ffn_tile, sub_tiles=FFN_SUB_TILES)

    def rows_first(a):
        return jnp.swapaxes(a, 0, 1)

    acts, (xn, xs_tm, so, sz, sg), (new_qk, new_xbc) = _in_proj(
        x_sample, rows_first(state_mlstm_conv[0]), rows_first(state_ssm_conv[0]), wts)
    m_in = jnp.pad(state_mlstm_m[0], ((0, 0), (0, LANES - M_HEADS)))[:, None, :]
    h_s, yss_s, states_s = _sample_recur(
        acts, (state_mlstm_C[0], state_mlstm_n[0], m_in, state_ssm[0].reshape(dec_batch, S_INNER, S_STATE)),
        nseq=dec_batch)
    x1_s = _post(h_s, yss_s, xs_tm, so, sz, sg, xn, wts)
    y_s, new_ffn = _ffn(x1_s, state_ffn_conv[0], wts, mode="tm", nseq=1,
                        rows=SAMPLE_BLOCK * DEC_SEQ, n_valid=SAMPLE_BLOCK * DEC_SEQ)

    def pack_states(convs, states, n):
        c1, n1, m1, s1 = states
        return (convs[0][None], c1[None], n1[None], m1[:, 0, :M_HEADS][None], convs[1][None],
                s1.reshape(n, S_HEADS, S_HEADDIM, S_STATE)[None], convs[2][None])

    def tail_rows(t, width):
        return t[:, SUBLANES - (width - 1):, :]

    p_out = pack_states((tail_rows(tails_p[0], M_CONV), tail_rows(tails_p[1], S_CONV),
                         tail_rows(tail_ffn_p, F_CONV)), states_p, batch)
    s_out = pack_states((rows_first(new_qk), rows_first(new_xbc), new_ffn), states_s, dec_batch)
    y_prompt = y_p.reshape(batch, seq, D_MODEL)
    return (y_prompt, y_s) + p_out + s_out
```

```python
import functools

import jax
import jax.numpy as jnp
from jax import lax
from jax.experimental import pallas as pl
from jax.experimental.pallas import tpu as pltpu

F32 = jnp.float32
BF16 = jnp.bfloat16

D_MODEL = 1024
N_META = 16
M_HEADS = 4
M_DK = 128
M_DV = 256
M_QK = M_HEADS * M_DK
M_V = M_HEADS * M_DV
M_CONV = 4
S_INNER = 2048
S_HEADDIM = 64
S_HEADS = 32
S_GROUPS = 4
S_STATE = 128
S_CONV = 4
S_BC = S_GROUPS * S_STATE
S_XBC = S_INNER + 2 * S_BC
S_GROUP_W = S_INNER // S_GROUPS
S_HEADS_PER_GROUP = S_HEADS // S_GROUPS
D_FF = 2816
F_CONV = 3
ALPHA = 2.0 ** 0.25
LN_EPS = 1e-5
RMS_EPS = 1e-5

LANES = 128
SUBLANES = 8
CHUNK = 128
SEQ_PAD = SUBLANES
DEC_SEQ = 4
SAMPLE_SEQS_PER_STEP = 8
SAMPLE_SUB = 16
SAMPLE_BLOCK = 32
PACK_COLS = 512
FFN_BLOCK = 256
FFN_AHEAD = 3
FFN_SUB_TILES = 4
FILLER_SLOTS = 2
VMEM_LIMIT_BYTES = 60 * 1024 * 1024

OFF_QK = 0
OFF_V = OFF_QK + 2 * M_QK
OFF_O = OFF_V + M_V
OFF_Z = OFF_O + M_V
OFF_XBC = OFF_Z + S_INNER
OFF_GATE = OFF_XBC + S_XBC
W_BIG_COLS = OFF_GATE + 2 * D_MODEL
LANE_IG = 0
LANE_LF = M_HEADS
LANE_DTA = 2 * M_HEADS
LANE_DT = LANE_DTA + S_HEADS
LANE_END = LANE_DT + S_HEADS

IN_I = 2 * M_QK + 2 * M_V
IN_Z = IN_I + 2 * M_HEADS
IN_DT = IN_Z + S_INNER + S_XBC
IN_GATE = IN_DT + S_HEADS
D_IN = IN_GATE + 2 * D_MODEL

STATE_SHAPES = [(M_HEADS, M_DK, M_DV), (M_HEADS, M_DK), (1, LANES), (S_INNER, S_STATE)]


def _softplus(x):
    return jnp.maximum(x, 0.0) + jnp.log1p(jnp.exp(-jnp.abs(x)))


def _sigmoid(x):
    return 0.5 * jnp.tanh(0.5 * x) + 0.5


def _silu(x):
    h = 0.5 * x
    return h * jnp.tanh(h) + h


def _layer_norm(x, g, b):
    mu = jnp.mean(x, axis=-1, keepdims=True)
    xc = x - mu
    var = jnp.mean(xc * xc, axis=-1, keepdims=True)
    return xc * lax.rsqrt(var + LN_EPS) * g + b


def _dot(a, b):
    return jnp.dot(a, b, preferred_element_type=F32)


def _wdot(a, w_ref, lo=None, hi=None):
    w = w_ref[...] if lo is None else w_ref[:, lo:hi]
    return _dot(a, pltpu.bitcast(w, BF16))


def _dot_nt(a, b):
    return lax.dot_general(a, b, (((1,), (1,)), ((), ())), preferred_element_type=F32)


def _dot_tn(a, b):
    return lax.dot_general(a, b, (((0,), (0,)), ((), ())), preferred_element_type=F32)


def _conv_carry(u, carry_ref, w_ref, b_ref, width, n_valid, cols=slice(None)):
    full = jnp.concatenate([carry_ref[:, cols], u], axis=0)
    acc = b_ref[:, cols] + u * w_ref[width - 1:width, cols]
    for k in range(1, width):
        acc = acc + pltpu.roll(full, k, 0)[SUBLANES:] * w_ref[width - 1 - k:width - k, cols]
    carry_ref[:, cols] = u[n_valid - SUBLANES:n_valid]
    return acc


def _time_major(ref):
    return jnp.concatenate([ref[:, t, :] for t in range(DEC_SEQ)], axis=0)


def _store_seq_major(ref, x, fill=None):
    nseq, _, c = ref.shape
    pad = jnp.zeros((1, c), x.dtype) if fill is None else fill
    ref[:, DEC_SEQ:, :] = jnp.broadcast_to(pad[None], (nseq, SEQ_PAD - DEC_SEQ, c))
    for t in range(DEC_SEQ):
        ref[:, t, :] = x[t * nseq:(t + 1) * nseq]


def _conv_tm(u, st_ref, new_ref, w_ref, b_ref, width, row_axis, cols=slice(None)):
    def idx(j):
        return (j, slice(None), cols) if row_axis == 0 else (slice(None), j, cols)

    nseq = st_ref.shape[1 - row_axis]
    full = ([st_ref[idx(j)] for j in range(width - 1)]
            + [u[t * nseq:(t + 1) * nseq] for t in range(DEC_SEQ)])
    outs = []
    for t in range(DEC_SEQ):
        acc = b_ref[:, cols]
        for j in range(width):
            acc = acc + full[t + j] * w_ref[j:j + 1, cols]
        outs.append(acc)
    for j in range(width - 1):
        new_ref[idx(j)] = full[DEC_SEQ + j]
    return jnp.concatenate(outs, axis=0)


def _gate_tile(xb, valid, wsm_ref, bsm_ref, alog_ref):
    g = _wdot(xb, wsm_ref) + bsm_ref[...]
    lane = lax.broadcasted_iota(jnp.int32, (1, LANES), 1)
    sp = _softplus(g)
    lsg = -_softplus(-g)
    a_row = jnp.where((lane >= LANE_DTA) & (lane < LANE_DT), -jnp.exp(alog_ref[...]), 0.0)
    tile = jnp.where(lane < LANE_LF, g,
                     jnp.where(lane < LANE_DTA, lsg,
                               jnp.where(lane < LANE_DT, sp * a_row,
                                         jnp.where(lane < LANE_END, sp, 0.0))))
    if valid is None:
        return tile
    return jnp.where(valid, tile, jnp.where(lane < LANE_LF, -jnp.inf, 0.0))


def _cumsum_rows(x, period):
    pos = lax.broadcasted_iota(jnp.int32, x.shape, 0) & (period - 1)
    k = 1
    while k < period:
        x = x + jnp.where(pos >= k, pltpu.roll(x, k, 0), 0.0)
        k *= 2
    return x


def _per_row(values, sub):
    if len(values) == 1:
        return values[0]
    return jnp.concatenate([jnp.broadcast_to(x, (sub, x.shape[1])) for x in values], axis=0)


def _chunk_recur(q, k, v, gt, xs, bm, cm, states, sub, fillers=None):
    staged = fillers is not None
    fillers = list(fillers or ())

    def fill():
        if fillers:
            fillers.pop(0)()

    def run(vector_stage, matmul_stage, items):
        if staged:
            ctxs = [vector_stage(i) for i in items]
            return [matmul_stage(c) for c in ctxs]
        return [matmul_stage(vector_stage(i)) for i in items]

    nsub = len(states)
    assert nsub * sub == CHUNK == q.shape[0]
    sub_rows = [slice(i * sub, (i + 1) * sub) for i in range(nsub)]
    lane = lax.broadcasted_iota(jnp.int32, (1, LANES), 1)
    cum_lanes = (lane >= LANE_LF) & (lane < LANE_DT)
    gc = jnp.where(cum_lanes, _cumsum_rows(jnp.where(cum_lanes, gt, 0.0), sub), gt)
    gct = gc.T
    lasts = [gc[r.stop - 1:r.stop, :] for r in sub_rows]
    last_rows = _per_row(lasts, sub)

    def head_dots(h):
        qh = q[:, h * M_DK:(h + 1) * M_DK]
        c0s = [st_in[0][h] for st_in, _ in states]
        qc = [_dot(qh[r], c0.astype(BF16)) for r, c0 in zip(sub_rows, c0s)]
        return _dot_nt(qh, k[:, h * M_DK:(h + 1) * M_DK]), jnp.concatenate(qc, axis=0), c0s

    def group_dots(g):
        cg = cm[:, g * S_STATE:(g + 1) * S_STATE]
        s_olds = [st_in[3][g * S_GROUP_W:(g + 1) * S_GROUP_W, :] for st_in, _ in states]
        yi = [_dot_nt(cg[r], s_old.astype(BF16)) for r, s_old in zip(sub_rows, s_olds)]
        return (_dot_nt(cg, bm[:, g * S_STATE:(g + 1) * S_STATE]),
                jnp.concatenate(yi, axis=0), s_olds)

    if staged:
        head_pre = [head_dots(h) for h in range(M_HEADS)]
        group_pre = [group_dots(g) for g in range(S_GROUPS)]
        head_dots = head_pre.__getitem__
        group_dots = group_pre.__getitem__
    fill()

    ti = lax.broadcasted_iota(jnp.int32, (CHUNK, CHUNK), 0)
    si = lax.broadcasted_iota(jnp.int32, (CHUNK, CHUNK), 1)
    causal = si <= ti
    if nsub > 1:
        causal = causal & ((si & -sub) == (ti & -sub))
    sub_lanes = [(lane >= r.start) & (lane < r.stop) for r in sub_rows]
    m_rows = [st_in[2][...] for st_in, _ in states]

    def head_vector(h):
        qh = q[:, h * M_DK:(h + 1) * M_DK]
        bt_c = gc[:, LANE_LF + h:LANE_LF + h + 1]
        ig_c = gc[:, LANE_IG + h:LANE_IG + h + 1]
        bt_r = gct[LANE_LF + h:LANE_LF + h + 1, :]
        ig_r = gct[LANE_IG + h:LANE_IG + h + 1, :]
        m0s = [m_row[:, h:h + 1] for m_row in m_rows]
        n0s = [st_in[1][h:h + 1, :] for st_in, _ in states]
        m0 = _per_row(m0s, sub)
        sqk, qc, c0s = head_dots(h)

        dmat = jnp.where(causal, bt_c - bt_r + ig_r, -jnp.inf)
        inter = bt_c + m0
        m_t = jnp.maximum(inter, jnp.max(dmat, axis=1, keepdims=True))
        w_intra = jnp.exp(dmat - m_t)
        w_inter = jnp.exp(inter - m_t)
        s = sqk * w_intra
        den = (jnp.sum(s, axis=1, keepdims=True)
               + jnp.sum(qh.astype(F32) * _per_row(n0s, sub), axis=1, keepdims=True) * w_inter)
        rdenom = 1.0 / jnp.maximum(jnp.abs(den), jnp.exp(-m_t))

        updates = []
        for i, r in enumerate(sub_rows):
            b_last = lasts[i][:, LANE_LF + h:LANE_LF + h + 1]
            d_last = b_last - bt_r + ig_r
            if nsub > 1:
                d_last = jnp.where(sub_lanes[i], d_last, -jnp.inf)
            m_new = jnp.maximum(b_last + m0s[i], jnp.max(d_last, axis=1, keepdims=True))
            w_last = jnp.exp(b_last - bt_c[r] + ig_c[r] - m_new)
            decay = jnp.exp(b_last + m0s[i] - m_new)
            kw = k[r, h * M_DK:(h + 1) * M_DK].astype(F32) * w_last
            states[i][1][1][h:h + 1, :] = decay * n0s[i] + jnp.sum(kw, axis=0, keepdims=True)
            updates.append((kw.astype(BF16), decay * c0s[i], m_new))
        return h, s.astype(BF16), qc * w_inter, rdenom, updates

    def head_matmul(ctx):
        h, s_b, inter_part, rdenom, updates = ctx
        num = _dot(s_b, v[:, h * M_DV:(h + 1) * M_DV]) + inter_part
        for i, (kw_b, c_decayed, _) in enumerate(updates):
            states[i][1][0][h] = c_decayed + _dot_tn(kw_b, v[sub_rows[i], h * M_DV:(h + 1) * M_DV])
        return num * rdenom, [m_new for _, _, m_new in updates]

    heads = run(head_vector, head_matmul, range(M_HEADS))
    for i in range(nsub):
        m_new_row = m_rows[i]
        for h, (_, m_news) in enumerate(heads):
            m_new_row = jnp.where(lane == h, m_news[i], m_new_row)
        states[i][1][2][...] = m_new_row
    fill()

    lo_half = lax.broadcasted_iota(jnp.int32, (CHUNK, LANES), 1) < S_HEADDIM
    xs_b = xs.astype(BF16)
    e_tile = jnp.exp(gc)
    wd_tile = jnp.exp(last_rows - gc) * pltpu.roll(gc, LANES - (LANE_DT - LANE_DTA), 1)
    hi_half = jnp.where(lo_half, 0, 1)

    def pair_lanes(tile, hh):
        return jnp.take_along_axis(tile, hi_half + (LANE_DTA + hh), axis=1)

    def group_vector(g):
        rows_g = slice(g * S_GROUP_W, (g + 1) * S_GROUP_W)
        cb, y_inter, s_olds = group_dots(g)
        pairs, wd_parts = [], []
        dec_parts = [[] for _ in range(nsub)]
        for p in range(S_HEADS_PER_GROUP // 2):
            ms = []
            for r in (2 * p, 2 * p + 1):
                hh = g * S_HEADS_PER_GROUP + r
                bt_c = gc[:, LANE_DTA + hh:LANE_DTA + hh + 1]
                bt_r = gct[LANE_DTA + hh:LANE_DTA + hh + 1, :]
                dt_r = gct[LANE_DT + hh:LANE_DT + hh + 1, :]
                dec = jnp.exp(jnp.where(causal, bt_c - bt_r, -jnp.inf))
                ms.append((cb * dec * dt_r).astype(BF16))
                for i in range(nsub):
                    dec_parts[i].append(jnp.broadcast_to(
                        jnp.exp(lasts[i][:, LANE_DTA + hh:LANE_DTA + hh + 1]), (S_HEADDIM, 1)))
            col = g * S_GROUP_W + p * LANES
            xp = xs_b[:, col:col + LANES]
            zero = jnp.zeros_like(xp)
            rhs = jnp.concatenate([jnp.where(lo_half, xp, zero), jnp.where(lo_half, zero, xp)], axis=0)
            hh = g * S_HEADS_PER_GROUP + 2 * p
            pairs.append((jnp.concatenate(ms, axis=1), rhs,
                          y_inter[:, p * LANES:(p + 1) * LANES] * pair_lanes(e_tile, hh)))
            wd_parts.append(pair_lanes(wd_tile, hh))
        a_mat = (xs[:, rows_g] * jnp.concatenate(wd_parts, axis=1)).astype(BF16)
        return g, pairs, a_mat, [jnp.concatenate(d, axis=0) * s_old for d, s_old in zip(dec_parts, s_olds)]

    def group_matmul(ctx):
        g, pairs, a_mat, s_decayed = ctx
        ys = [_dot(lhs, rhs) + inter_part for lhs, rhs, inter_part in pairs]
        for i, r in enumerate(sub_rows):
            states[i][1][3][g * S_GROUP_W:(g + 1) * S_GROUP_W, :] = (
                s_decayed[i] + _dot_tn(a_mat[r], bm[r, g * S_STATE:(g + 1) * S_STATE]))
        return jnp.concatenate(ys, axis=1)

    groups = run(group_vector, group_matmul, range(S_GROUPS))
    return jnp.concatenate([hp for hp, _ in heads], axis=1), jnp.concatenate(groups, axis=1)


def _post_math(h, y, xs, so, sz, sg, xn, mg_ref, sng_ref, dexp_ref, wpa_ref, wpb_ref, wout_ref,
               g1_ref, b1_ref):
    hs = []
    for i in range(M_HEADS):
        hh = h[:, i * M_DV:(i + 1) * M_DV]
        mu = jnp.mean(hh, axis=-1, keepdims=True)
        hc = hh - mu
        var = jnp.mean(hc * hc, axis=-1, keepdims=True)
        hs.append(hc * lax.rsqrt(var + LN_EPS))
    hn = jnp.concatenate(hs, axis=1) * mg_ref[...]
    ya = _wdot((so * hn).astype(BF16), wpa_ref)

    y = (y + dexp_ref[...] * xs) * sz
    ys = []
    for g in range(S_GROUPS):
        yg = y[:, g * S_GROUP_W:(g + 1) * S_GROUP_W]
        ys.append(yg * lax.rsqrt(jnp.mean(yg * yg, axis=-1, keepdims=True) + RMS_EPS))
    yn = jnp.concatenate(ys, axis=1) * sng_ref[...]
    yb = _wdot(yn.astype(BF16), wpb_ref)

    mixed = _wdot((sg[:, :D_MODEL] * ya + sg[:, D_MODEL:] * yb).astype(BF16), wout_ref)
    return _layer_norm(ALPHA * xn + mixed, g1_ref[...], b1_ref[...])


def _const_spec(shape):
    nd = len(shape)
    return pl.BlockSpec(shape, lambda *_: (0,) * nd, pipeline_mode=pl.Buffered(1))


def _pack_body(w_ref, o_ref):
    o_ref[...] = pltpu.bitcast(w_ref[...].astype(BF16), jnp.uint32)


def _pack_weight(w, name):
    k, n = w.shape
    bn = min(n, PACK_COLS)
    return pl.pallas_call(
        _pack_body, grid=(n // bn,), in_specs=[pl.BlockSpec((k, bn), lambda j: (0, j))],
        out_specs=pl.BlockSpec((k // 2, bn), lambda j: (0, j)),
        out_shape=jax.ShapeDtypeStruct((k // 2, n), jnp.uint32), name=f"pack_{name}",
        compiler_params=_params(1),
    )(w)


def _pack_t_body(wt_ref, o_ref):
    o_ref[...] = pltpu.bitcast(wt_ref[...].T.astype(BF16), jnp.uint32)


def _pack_gates_body(wif_ref, wdt_ref, o_ref):
    wdt = wdt_ref[...]
    rows = jnp.concatenate([wif_ref[...], wdt, wdt, jnp.zeros((LANES - LANE_END, D_MODEL), F32)], axis=0)
    o_ref[...] = pltpu.bitcast(rows.T.astype(BF16), jnp.uint32)


def _pack_in(wt):
    k = wt.shape[1]
    assert wt.shape[0] == D_IN
    starts = (0, IN_Z - IN_I, IN_GATE - IN_DT + IN_Z - IN_I)
    lens = (IN_I, IN_DT - IN_Z, D_IN - IN_GATE)
    assert all(s % SUBLANES == 0 for s in starts) and all(n % PACK_COLS == 0 for n in lens)
    b1 = lens[0] // PACK_COLS
    b2 = b1 + lens[1] // PACK_COLS

    def src_row(j):
        s0, s1, s2 = (s // SUBLANES for s in starts)
        return (j * (PACK_COLS // SUBLANES) + jnp.where(j < b1, s0, jnp.where(j < b2, s1, s2))) * SUBLANES

    big = pl.pallas_call(
        _pack_t_body, grid=(W_BIG_COLS // PACK_COLS,),
        in_specs=[pl.BlockSpec((pl.Element(PACK_COLS), pl.Element(k)), lambda j: (src_row(j), 0))],
        out_specs=pl.BlockSpec((k // 2, PACK_COLS), lambda j: (0, j)),
        out_shape=jax.ShapeDtypeStruct((k // 2, W_BIG_COLS), jnp.uint32),
        name="pack_in", compiler_params=_params(1),
    )(wt)
    small = pl.pallas_call(
        _pack_gates_body, out_shape=jax.ShapeDtypeStruct((k // 2, LANES), jnp.uint32), name="pack_gates",
    )(wt[IN_I:IN_Z], wt[IN_DT:IN_GATE])
    return big, small


def _params(dims):
    return pltpu.CompilerParams(dimension_semantics=("arbitrary",) * dims,
                                vmem_limit_bytes=VMEM_LIMIT_BYTES)


def _mixer_body(rows, n_valid,
                x_ref, g0_ref, b0_ref, wbig_ref, wsm_ref, bsm_ref, alog_ref,
                wmc_ref, bmc_ref, wsc_ref, bsc_ref, cin_qk_ref, cin_xbc_ref,
                c0_ref, n0_ref, m0_ref, s0_ref,
                mg_ref, sng_ref, dexp_ref, wpa_ref, wpb_ref, wout_ref, g1_ref, b1_ref,
                x1_ref, cout_qk_ref, cout_xbc_ref, c1_ref, n1_ref, m1_ref, s1_ref):
    @pl.when(pl.program_id(1) == 0)
    def _():
        cout_qk_ref[...] = cin_qk_ref[...]
        cout_xbc_ref[...] = cin_xbc_ref[...]
        c1_ref[...] = c0_ref[...]
        n1_ref[...] = n0_ref[...]
        m1_ref[...] = m0_ref[...]
        s1_ref[...] = s0_ref[...]

    xn = _layer_norm(x_ref[...], g0_ref[...], b0_ref[...])
    xb = xn.astype(BF16)
    valid = None
    if n_valid < rows:
        valid = lax.broadcasted_iota(jnp.int32, (rows, 1), 0) < n_valid
    gt = _gate_tile(xb, valid, wsm_ref, bsm_ref, alog_ref)

    u_qk = _wdot(xb, wbig_ref, OFF_QK, OFF_QK + 2 * M_QK)
    qk = _silu(_conv_carry(u_qk, cout_qk_ref, wmc_ref, bmc_ref, M_CONV, n_valid))
    q = (qk[:, :M_QK] * (M_DK ** -0.5)).astype(BF16)
    k = qk[:, M_QK:].astype(BF16)
    v = _wdot(xb, wbig_ref, OFF_V, OFF_V + M_V).astype(BF16)
    u_xbc = _wdot(xb, wbig_ref, OFF_XBC, OFF_XBC + S_XBC)
    xbc = _silu(_conv_carry(u_xbc, cout_xbc_ref, wsc_ref, bsc_ref, S_CONV, n_valid))
    xs = xbc[:, :S_INNER]
    bm = xbc[:, S_INNER:S_INNER + S_BC].astype(BF16)
    cm = xbc[:, S_INNER + S_BC:].astype(BF16)

    offs = [OFF_O, OFF_Z, OFF_Z + D_MODEL, OFF_GATE, OFF_GATE + D_MODEL]
    dense = {}
    nchunk = rows // CHUNK
    nslot = nchunk * FILLER_SLOTS
    hs, ys = [], []
    for ci in range(nchunk):
        sl = slice(ci * CHUNK, (ci + 1) * CHUNK)
        state = (c1_ref, n1_ref, m1_ref, s1_ref)

        def filler(slot):
            def issue():
                for off in offs[slot::nslot]:
                    dense[off] = _wdot(xb, wbig_ref, off, off + D_MODEL)
            return issue

        h, y = _chunk_recur(q[sl], k[sl], v[sl], gt[sl], xs[sl], bm[sl], cm[sl], [(state, state)], CHUNK,
                            [filler(ci * FILLER_SLOTS + i) for i in range(FILLER_SLOTS)])
        hs.append(h)
        ys.append(y)
    h = jnp.concatenate(hs, axis=0)
    y = jnp.concatenate(ys, axis=0)

    so = _sigmoid(dense[OFF_O])
    sz = _silu(jnp.concatenate([dense[OFF_Z], dense[OFF_Z + D_MODEL]], axis=1))
    sg = _sigmoid(jnp.concatenate([dense[OFF_GATE], dense[OFF_GATE + D_MODEL]], axis=1))
    x1_ref[...] = _post_math(h, y, xs, so, sz, sg, xn, mg_ref, sng_ref, dexp_ref,
                             wpa_ref, wpb_ref, wout_ref, g1_ref, b1_ref)


def _mixer(x_rows, cin_qk, cin_xbc, states, wts, *, nseq, rows, n_valid):
    total = x_rows.shape[0]
    nt = total // (nseq * rows)
    rspec = pl.BlockSpec((rows, D_MODEL), lambda b, c: (b * nt + c, 0))

    def per_seq(shape):
        nd = len(shape)
        return pl.BlockSpec((None,) + shape, lambda b, c: (b,) + (0,) * nd)

    def shared(shape):
        nd = len(shape)
        return pl.BlockSpec((None,) + shape, lambda b, c: (0,) * (nd + 1), pipeline_mode=pl.Buffered(1))

    tails = [(SUBLANES, 2 * M_QK), (SUBLANES, S_XBC)]
    in_specs = ([rspec, _const_spec((1, D_MODEL)), _const_spec((1, D_MODEL)),
                 _const_spec((D_MODEL // 2, W_BIG_COLS)), _const_spec((D_MODEL // 2, LANES)),
                 _const_spec((1, LANES)), _const_spec((1, LANES)),
                 _const_spec((M_CONV, 2 * M_QK)), _const_spec((1, 2 * M_QK)),
                 _const_spec((S_CONV, S_XBC)), _const_spec((1, S_XBC))]
                + [shared(s) for s in tails + STATE_SHAPES]
                + [_const_spec((1, M_V)), _const_spec((1, S_INNER)), _const_spec((1, S_INNER)),
                   _const_spec((M_V // 2, D_MODEL)), _const_spec((S_INNER // 2, D_MODEL)),
                   _const_spec((D_MODEL // 2, D_MODEL)), _const_spec((1, D_MODEL)), _const_spec((1, D_MODEL))])
    out_specs = [rspec] + [per_seq(s) for s in tails + STATE_SHAPES]
    out_shapes = ([jax.ShapeDtypeStruct((total, D_MODEL), F32)]
                  + [jax.ShapeDtypeStruct((nseq,) + s, F32) for s in tails + STATE_SHAPES])
    outs = pl.pallas_call(
        functools.partial(_mixer_body, rows, n_valid),
        grid=(nseq, nt), in_specs=in_specs, out_specs=out_specs, out_shape=out_shapes,
        name=f"mixer_{nseq}", compiler_params=_params(2),
    )(x_rows, wts["ln0_g"], wts["ln0_b"], wts["w_big"], wts["w_small"], wts["b_small"], wts["a_log"],
      wts["w_mconv"], wts["b_mconv"], wts["w_sconv"], wts["b_sconv"], cin_qk, cin_xbc, *states,
      wts["mnorm_g"], wts["snorm_g"], wts["d_exp"], wts["w_proj_a"], wts["w_proj_b"], wts["w_out"],
      wts["ln1_g"], wts["ln1_b"])
    return outs[0], outs[1:3], outs[3:]


def _in_proj_body(x_ref, g0_ref, b0_ref, wbig_ref, wsm_ref, bsm_ref, alog_ref,
                  wmc_ref, bmc_ref, wsc_ref, bsc_ref, st_qk_ref, st_xbc_ref,
                  q_ref, k_ref, v_ref, gate_ref, xs_ref, bm_ref, cm_ref,
                  xn_ref, xs_tm_ref, so_ref, sz_ref, sg_ref, new_qk_ref, new_xbc_ref):
    xn = _layer_norm(_time_major(x_ref), g0_ref[...], b0_ref[...])
    xn_ref[...] = xn
    xb = xn.astype(BF16)
    lane = lax.broadcasted_iota(jnp.int32, (1, LANES), 1)
    _store_seq_major(gate_ref, _gate_tile(xb, None, wsm_ref, bsm_ref, alog_ref),
                     jnp.where(lane < LANE_LF, -jnp.inf, 0.0))

    u_qk = _wdot(xb, wbig_ref, OFF_QK, OFF_QK + 2 * M_QK)
    qk = _silu(_conv_tm(u_qk, st_qk_ref, new_qk_ref, wmc_ref, bmc_ref, M_CONV, 0))
    _store_seq_major(q_ref, qk[:, :M_QK] * (M_DK ** -0.5))
    _store_seq_major(k_ref, qk[:, M_QK:])
    _store_seq_major(v_ref, _wdot(xb, wbig_ref, OFF_V, OFF_V + M_V))
    so_ref[...] = _sigmoid(_wdot(xb, wbig_ref, OFF_O, OFF_O + M_V))
    sz_ref[...] = _silu(_wdot(xb, wbig_ref, OFF_Z, OFF_Z + S_INNER))
    u_xbc = _wdot(xb, wbig_ref, OFF_XBC, OFF_XBC + S_XBC)
    xbc = _silu(_conv_tm(u_xbc, st_xbc_ref, new_xbc_ref, wsc_ref, bsc_ref, S_CONV, 0))
    xs_tm_ref[...] = xbc[:, :S_INNER]
    _store_seq_major(xs_ref, xbc[:, :S_INNER])
    _store_seq_major(bm_ref, xbc[:, S_INNER:S_INNER + S_BC])
    _store_seq_major(cm_ref, xbc[:, S_INNER + S_BC:])
    sg_ref[...] = _sigmoid(_wdot(xb, wbig_ref, OFF_GATE, OFF_GATE + 2 * D_MODEL))


def _in_proj(x, st_qk, st_xbc, wts):
    nseq = x.shape[0]
    nb = SAMPLE_BLOCK

    def rows_first(width, cols):
        return pl.BlockSpec((width - 1, nb, cols), lambda i: (0, i, 0))

    def sspec(rows, cols):
        return pl.BlockSpec((nb, rows, cols), lambda i: (i, 0, 0))

    def tspec(cols):
        return pl.BlockSpec((nb * DEC_SEQ, cols), lambda i: (i, 0))

    in_specs = [sspec(DEC_SEQ, D_MODEL), _const_spec((1, D_MODEL)), _const_spec((1, D_MODEL)),
                _const_spec((D_MODEL // 2, W_BIG_COLS)), _const_spec((D_MODEL // 2, LANES)),
                _const_spec((1, LANES)), _const_spec((1, LANES)),
                _const_spec((M_CONV, 2 * M_QK)), _const_spec((1, 2 * M_QK)),
                _const_spec((S_CONV, S_XBC)), _const_spec((1, S_XBC)),
                rows_first(M_CONV, 2 * M_QK), rows_first(S_CONV, S_XBC)]
    seq_cols = [M_QK, M_QK, M_V, LANES, S_INNER, S_BC, S_BC]
    tm_cols = [D_MODEL, S_INNER, M_V, S_INNER, 2 * D_MODEL]
    outs = pl.pallas_call(
        _in_proj_body, grid=(nseq // nb,), in_specs=in_specs,
        out_specs=([sspec(SEQ_PAD, c) for c in seq_cols] + [tspec(c) for c in tm_cols]
                   + [rows_first(M_CONV, 2 * M_QK), rows_first(S_CONV, S_XBC)]),
        out_shape=([jax.ShapeDtypeStruct((nseq, SEQ_PAD, c), F32) for c in seq_cols]
                   + [jax.ShapeDtypeStruct((nseq * DEC_SEQ, c), F32) for c in tm_cols]
                   + [jax.ShapeDtypeStruct((M_CONV - 1, nseq, 2 * M_QK), F32),
                      jax.ShapeDtypeStruct((S_CONV - 1, nseq, S_XBC), F32)]),
        name="sample_in_proj", compiler_params=_params(1),
    )(x, wts["ln0_g"], wts["ln0_b"], wts["w_big"], wts["w_small"], wts["b_small"], wts["a_log"],
      wts["w_mconv"], wts["b_mconv"], wts["w_sconv"], wts["b_sconv"], st_qk, st_xbc)
    return outs[:7], outs[7:12], outs[12:]


def _sample_recur_body(q_ref, k_ref, v_ref, gate_ref, xs_ref, bm_ref, cm_ref,
                       c0_ref, n0_ref, m0_ref, s0_ref,
                       h_ref, y_ref, c1_ref, n1_ref, m1_ref, s1_ref):
    nb = SAMPLE_SEQS_PER_STEP
    lane = lax.broadcasted_iota(jnp.int32, (1, LANES), 1)

    def stack(ref, dtype, fill=None):
        x = ref[...]
        c = x.shape[2]
        pad = jnp.zeros((1, 1, c), F32) if fill is None else fill[None]
        x = jnp.concatenate([x, jnp.broadcast_to(pad, (nb, SAMPLE_SUB - SEQ_PAD, c))], axis=1)
        return x.reshape(nb * SAMPLE_SUB, c).astype(dtype)

    def unstack(x):
        return x.reshape(nb, SAMPLE_SUB, x.shape[1])[:, :SEQ_PAD, :]

    states = [((c0_ref.at[i], n0_ref.at[i], m0_ref.at[i], s0_ref.at[i]),
               (c1_ref.at[i], n1_ref.at[i], m1_ref.at[i], s1_ref.at[i])) for i in range(nb)]
    h, y = _chunk_recur(stack(q_ref, BF16), stack(k_ref, BF16), stack(v_ref, BF16),
                        stack(gate_ref, F32, jnp.where(lane < LANE_LF, -jnp.inf, 0.0)),
                        stack(xs_ref, F32), stack(bm_ref, BF16), stack(cm_ref, BF16), states, SAMPLE_SUB)
    h_ref[...] = unstack(h)
    y_ref[...] = unstack(y)


def _sample_recur(acts, states, *, nseq):
    q, k, v, gate, xs, bm, cm = acts
    nb = SAMPLE_SEQS_PER_STEP
    assert nb * SAMPLE_SUB == CHUNK

    def rspec(cols):
        return pl.BlockSpec((nb, SEQ_PAD, cols), lambda i: (i, 0, 0))

    def sspec(shape):
        nd = len(shape)
        return pl.BlockSpec((nb,) + shape, lambda i: (i,) + (0,) * nd)

    in_specs = ([rspec(M_QK), rspec(M_QK), rspec(M_V), rspec(LANES), rspec(S_INNER), rspec(S_BC), rspec(S_BC)]
                + [sspec(s) for s in STATE_SHAPES])
    out_specs = [rspec(M_V), rspec(S_INNER)] + [sspec(s) for s in STATE_SHAPES]
    out_shapes = ([jax.ShapeDtypeStruct((nseq, SEQ_PAD, M_V), F32),
                   jax.ShapeDtypeStruct((nseq, SEQ_PAD, S_INNER), F32)]
                  + [jax.ShapeDtypeStruct((nseq,) + s, F32) for s in STATE_SHAPES])
    outs = pl.pallas_call(
        _sample_recur_body, grid=(nseq // nb,), in_specs=in_specs, out_specs=out_specs,
        out_shape=out_shapes, name="sample_recur", compiler_params=_params(1),
    )(q, k, v, gate, xs, bm, cm, *states)
    return outs[0], outs[1], outs[2:]


def _post_body(h_ref, y_ref, xs_ref, so_ref, sz_ref, sg_ref, xn_ref,
               mg_ref, sng_ref, dexp_ref, wpa_ref, wpb_ref, wout_ref, g1_ref, b1_ref, x1_ref):
    h = jnp.concatenate([h_ref[:, t, :] for t in range(DEC_SEQ)], axis=0)
    y = jnp.concatenate([y_ref[:, t, :] for t in range(DEC_SEQ)], axis=0)
    x1_ref[...] = _post_math(h, y, xs_ref[...], so_ref[...], sz_ref[...], sg_ref[...],
                             xn_ref[...], mg_ref, sng_ref, dexp_ref, wpa_ref, wpb_ref, wout_ref,
                             g1_ref, b1_ref)


def _post(h, y, xs, so, sz, sg, xn, wts):
    nseq = h.shape[0]
    nb = SAMPLE_BLOCK

    def sspec(cols):
        return pl.BlockSpec((nb, SEQ_PAD, cols), lambda i: (i, 0, 0))

    def tspec(cols):
        return pl.BlockSpec((nb * DEC_SEQ, cols), lambda i: (i, 0))

    in_specs = [sspec(M_V), sspec(S_INNER), tspec(S_INNER), tspec(M_V), tspec(S_INNER),
                tspec(2 * D_MODEL), tspec(D_MODEL),
                _const_spec((1, M_V)), _const_spec((1, S_INNER)), _const_spec((1, S_INNER)),
                _const_spec((M_V // 2, D_MODEL)), _const_spec((S_INNER // 2, D_MODEL)),
                _const_spec((D_MODEL // 2, D_MODEL)), _const_spec((1, D_MODEL)), _const_spec((1, D_MODEL))]
    return pl.pallas_call(
        _post_body, grid=(nseq // nb,), in_specs=in_specs, out_specs=tspec(D_MODEL),
        out_shape=jax.ShapeDtypeStruct((nseq * DEC_SEQ, D_MODEL), F32), name="sample_post",
        compiler_params=_params(1),
    )(h, y, xs, so, sz, sg, xn, wts["mnorm_g"], wts["snorm_g"], wts["d_exp"],
      wts["w_proj_a"], wts["w_proj_b"], wts["w_out"], wts["ln1_g"], wts["ln1_b"])


def _ffn_body(mode, rows, n_valid, sub_tiles,
              x1_ref, wup_ref, wfc_ref, bfc_ref, wdn_ref, g2_ref, b2_ref, cin_ref,
              y_ref, cout_ref):
    if mode == "seq":
        @pl.when(pl.program_id(1) == 0)
        def _():
            cout_ref[...] = cin_ref[...]

    def conv(u, cols):
        if mode == "tm":
            return _conv_tm(u, cin_ref, cout_ref, wfc_ref, bfc_ref, F_CONV, 1, cols)
        return _conv_carry(u, cout_ref, wfc_ref, bfc_ref, F_CONV, n_valid, cols)

    sub = rows // sub_tiles
    nblk = D_FF // FFN_BLOCK
    x1s = [x1_ref[i * sub:(i + 1) * sub, :] for i in range(sub_tiles)]
    x1bs = [x.astype(BF16) for x in x1s]

    def up_proj(n):
        x1b, lo = x1bs[n // nblk], (n % nblk) * FFN_BLOCK
        return (_wdot(x1b, wup_ref, lo, lo + FFN_BLOCK),
                _wdot(x1b, wup_ref, D_FF + lo, D_FF + lo + FFN_BLOCK))

    total = sub_tiles * nblk
    ups = [up_proj(n) for n in range(min(FFN_AHEAD, total))]
    for i in range(sub_tiles):
        ff = None
        for j in range(nblk):
            n = i * nblk + j
            lo = j * FFN_BLOCK
            ua, ub = ups[n]
            if n + FFN_AHEAD < total:
                ups.append(up_proj(n + FFN_AHEAD))
            va = conv(ua, slice(lo, lo + FFN_BLOCK))
            vb = conv(ub, slice(D_FF + lo, D_FF + lo + FFN_BLOCK))
            act = (_silu(va) * vb).astype(BF16)
            part = _dot(act, pltpu.bitcast(wdn_ref[lo // 2:(lo + FFN_BLOCK) // 2, :], BF16))
            ff = part if ff is None else ff + part
        y = _layer_norm(ALPHA * x1s[i] + ff, g2_ref[...], b2_ref[...])
        if mode == "tm":
            nseq = y_ref.shape[0]
            for t in range(DEC_SEQ):
                y_ref[:, t, :] = y[t * nseq:(t + 1) * nseq]
        else:
            y_ref[i * sub:(i + 1) * sub, :] = y


def _ffn(x1, cin, wts, *, mode, nseq, rows, n_valid, sub_tiles=1):
    assert sub_tiles == 1 or mode == "seq"
    total = x1.shape[0]
    nt = total // (nseq * rows)

    def rspec(cols):
        return pl.BlockSpec((rows, cols), lambda b, c: (b * nt + c, 0))

    if mode == "seq":
        cin_spec = pl.BlockSpec((None, SUBLANES, 2 * D_FF), lambda b, c: (0, 0, 0))
        cout_spec = pl.BlockSpec((None, SUBLANES, 2 * D_FF), lambda b, c: (b, 0, 0))
        cout_shape = jax.ShapeDtypeStruct((nseq, SUBLANES, 2 * D_FF), F32)
        y_spec = rspec(D_MODEL)
        y_shape = jax.ShapeDtypeStruct((total, D_MODEL), F32)
    else:
        nb = rows // DEC_SEQ
        sspec = lambda r, cols: pl.BlockSpec((nb, r, cols), lambda b, c: (b * nt + c, 0, 0))
        cin_spec = sspec(F_CONV - 1, 2 * D_FF)
        cout_spec = sspec(F_CONV - 1, 2 * D_FF)
        cout_shape = jax.ShapeDtypeStruct((total // DEC_SEQ, F_CONV - 1, 2 * D_FF), F32)
        y_spec = sspec(DEC_SEQ, D_MODEL)
        y_shape = jax.ShapeDtypeStruct((total // DEC_SEQ, DEC_SEQ, D_MODEL), F32)
    in_specs = [rspec(D_MODEL), _const_spec((D_MODEL // 2, 2 * D_FF)), _const_spec((F_CONV, 2 * D_FF)),
                _const_spec((1, 2 * D_FF)), _const_spec((D_FF // 2, D_MODEL)),
                _const_spec((1, D_MODEL)), _const_spec((1, D_MODEL)), cin_spec]
    return pl.pallas_call(
        functools.partial(_ffn_body, mode, rows, n_valid, sub_tiles),
        grid=(nseq, nt), in_specs=in_specs, out_specs=[y_spec, cout_spec],
        out_shape=[y_shape, cout_shape],
        name=f"ffn_{mode}_{nseq}", compiler_params=_params(2),
    )(x1, wts["w_up"], wts["w_fconv"], wts["b_fconv"], wts["w_down"], wts["ln2_g"], wts["ln2_b"], cin)


def kernel(x_prompt, x_sample, state_mlstm_conv, state_mlstm_C, state_mlstm_n, state_mlstm_m, state_ssm_conv, state_ssm, state_ffn_conv, meta_tokens, ln0_g, ln0_b, w_in, b_mlstm_if, w_mlstm_conv, b_mlstm_conv, mlstm_norm_g, w_proj_a, w_ssm_conv, b_ssm_conv, ssm_dt_bias, ssm_A_log, ssm_D, ssm_norm_g, w_proj_b, w_out, ln1_g, ln1_b, w_up, w_ffn_conv, b_ffn_conv, w_down, ln2_g, ln2_b):
    batch, seq, _ = x_prompt.shape
    dec_batch, dec_seq, _ = x_sample.shape
    prompt_tile = 2 * CHUNK
    ffn_tile = 2 * CHUNK
    assert dec_seq == DEC_SEQ and seq % (FFN_SUB_TILES * ffn_tile) == 0 and meta_tokens.shape[0] == N_META
    assert dec_batch % SAMPLE_SEQS_PER_STEP == 0 and dec_batch % SAMPLE_BLOCK == 0

    w_big, w_small = _pack_in(w_in[0].T)
    lane_pad = jnp.zeros((LANES - LANE_END,), F32)
    b_small = jnp.concatenate([b_mlstm_if[0], ssm_dt_bias[0], ssm_dt_bias[0], lane_pad])[None]
    a_log = jnp.concatenate([jnp.zeros((LANE_DTA,), F32), ssm_A_log[0],
                             jnp.zeros((LANES - LANE_DT,), F32)])[None]
    wts = {
        "ln0_g": ln0_g[None], "ln0_b": ln0_b[None], "w_big": w_big, "w_small": w_small,
        "b_small": b_small, "a_log": a_log,
        "w_mconv": w_mlstm_conv[0], "b_mconv": b_mlstm_conv, "w_sconv": w_ssm_conv[0], "b_sconv": b_ssm_conv,
        "mnorm_g": mlstm_norm_g, "snorm_g": ssm_norm_g, "d_exp": jnp.repeat(ssm_D[0], S_HEADDIM)[None],
        "w_proj_a": _pack_weight(w_proj_a[0], "proj_a"), "w_proj_b": _pack_weight(w_proj_b[0], "proj_b"),
        "w_out": _pack_weight(w_out[0], "out"), "ln1_g": ln1_g, "ln1_b": ln1_b,
        "w_up": _pack_weight(w_up[0], "up"), "w_fconv": w_ffn_conv[0], "b_fconv": b_ffn_conv,
        "w_down": _pack_weight(w_down[0], "down"), "ln2_g": ln2_g, "ln2_b": ln2_b,
    }

    x_meta = jnp.pad(meta_tokens, ((0, CHUNK - N_META), (0, 0)))
    zero_states = [jnp.zeros((1,) + s, F32) for s in STATE_SHAPES]
    x1_m, tails_m, states_m = _mixer(
        x_meta, jnp.zeros((1, SUBLANES, 2 * M_QK), F32), jnp.zeros((1, SUBLANES, S_XBC), F32),
        zero_states, wts, nseq=1, rows=CHUNK, n_valid=N_META)
    _, tail_ffn_m = _ffn(x1_m, jnp.zeros((1, SUBLANES, 2 * D_FF), F32), wts, mode="seq", nseq=1,
                         rows=CHUNK, n_valid=N_META)

    x1_p, tails_p, states_p = _mixer(
        x_prompt.reshape(batch * seq, D_MODEL), tails_m[0], tails_m[1], states_m, wts,
        nseq=batch, rows=prompt_tile, n_valid=prompt_tile)
    y_p, tail_ffn_p = _ffn(x1_p, tail_ffn_m, wts, mode="seq", nseq=batch, rows=FFN_SUB_TILES * ffn_tile,
                           n_valid=ffn_tile, sub_tiles=FFN_SUB_TILES)

    def rows_first(a):
        return jnp.swapaxes(a, 0, 1)

    acts, (xn, xs_tm, so, sz, sg), (new_qk, new_xbc) = _in_proj(
        x_sample, rows_first(state_mlstm_conv[0]), rows_first(state_ssm_conv[0]), wts)
    m_in = jnp.pad(state_mlstm_m[0], ((0, 0), (0, LANES - M_HEADS)))[:, None, :]
    h_s, yss_s, states_s = _sample_recur(
        acts, (state_mlstm_C[0], state_mlstm_n[0], m_in, state_ssm[0].reshape(dec_batch, S_INNER, S_STATE)),
        nseq=dec_batch)
    x1_s = _post(h_s, yss_s, xs_tm, so, sz, sg, xn, wts)
    y_s, new_ffn = _ffn(x1_s, state_ffn_conv[0], wts, mode="tm", nseq=1,
                        rows=SAMPLE_BLOCK * DEC_SEQ, n_valid=SAMPLE_BLOCK * DEC_SEQ)

    def pack_states(convs, states, n):
        c1, n1, m1, s1 = states
        return (convs[0][None], c1[None], n1[None], m1[:, 0, :M_HEADS][None], convs[1][None],
                s1.reshape(n, S_HEADS, S_HEADDIM, S_STATE)[None], convs[2][None])

    def tail_rows(t, width):
        return t[:, SUBLANES - (width - 1):, :]

    p_out = pack_states((tail_rows(tails_p[0], M_CONV), tail_rows(tails_p[1], S_CONV),
                         tail_rows(tail_ffn_p, F_CONV)), states_p, batch)
    s_out = pack_states((rows_first(new_qk), rows_first(new_xbc), new_ffn), states_s, dec_batch)
    y_prompt = y_p.reshape(batch, seq, D_MODEL)
    return (y_prompt, y_s) + p_out + s_out
```

```python
import functools

import jax
import jax.numpy as jnp
from jax import lax
from jax.experimental import pallas as pl
from jax.experimental.pallas import tpu as pltpu

F32 = jnp.float32
BF16 = jnp.bfloat16

D_MODEL = 1024
N_META = 16
M_HEADS = 4
M_DK = 128
M_DV = 256
M_QK = M_HEADS * M_DK
M_V = M_HEADS * M_DV
M_CONV = 4
S_INNER = 2048
S_HEADDIM = 64
S_HEADS = 32
S_GROUPS = 4
S_STATE = 128
S_CONV = 4
S_BC = S_GROUPS * S_STATE
S_XBC = S_INNER + 2 * S_BC
S_GROUP_W = S_INNER // S_GROUPS
S_HEADS_PER_GROUP = S_HEADS // S_GROUPS
D_FF = 2816
F_CONV = 3
ALPHA = 2.0 ** 0.25
LN_EPS = 1e-5
RMS_EPS = 1e-5

LANES = 128
SUBLANES = 8
CHUNK = 128
SEQ_PAD = SUBLANES
DEC_SEQ = 4
SAMPLE_SEQS_PER_STEP = 8
SAMPLE_SUB = 16
SAMPLE_BLOCK = 32
PACK_COLS = 512
FFN_BLOCK = 256
FFN_AHEAD = 3
FFN_SUB_TILES = 4
FILLER_SLOTS = 2
VMEM_LIMIT_BYTES = 60 * 1024 * 1024

OFF_QK = 0
OFF_V = OFF_QK + 2 * M_QK
OFF_O = OFF_V + M_V
OFF_Z = OFF_O + M_V
OFF_XBC = OFF_Z + S_INNER
OFF_GATE = OFF_XBC + S_XBC
W_BIG_COLS = OFF_GATE + 2 * D_MODEL
LANE_IG = 0
LANE_LF = M_HEADS
LANE_DTA = 2 * M_HEADS
LANE_DT = LANE_DTA + S_HEADS
LANE_END = LANE_DT + S_HEADS

IN_I = 2 * M_QK + 2 * M_V
IN_Z = IN_I + 2 * M_HEADS
IN_DT = IN_Z + S_INNER + S_XBC
IN_GATE = IN_DT + S_HEADS
D_IN = IN_GATE + 2 * D_MODEL

STATE_SHAPES = [(M_HEADS, M_DK, M_DV), (M_HEADS, M_DK), (1, LANES), (S_INNER, S_STATE)]


def _softplus(x):
    return jnp.maximum(x, 0.0) + jnp.log1p(jnp.exp(-jnp.abs(x)))


def _sigmoid(x):
    return 0.5 * jnp.tanh(0.5 * x) + 0.5


def _silu(x):
    h = 0.5 * x
    return h * jnp.tanh(h) + h


def _layer_norm(x, g, b):
    mu = jnp.mean(x, axis=-1, keepdims=True)
    xc = x - mu
    var = jnp.mean(xc * xc, axis=-1, keepdims=True)
    return xc * lax.rsqrt(var + LN_EPS) * g + b


def _dot(a, b):
    return jnp.dot(a, b, preferred_element_type=F32)


def _wdot(a, w_ref, lo=None, hi=None):
    w = w_ref[...] if lo is None else w_ref[:, lo:hi]
    return _dot(a, pltpu.bitcast(w, BF16))


def _dot_nt(a, b):
    return lax.dot_general(a, b, (((1,), (1,)), ((), ())), preferred_element_type=F32)


def _dot_tn(a, b):
    return lax.dot_general(a, b, (((0,), (0,)), ((), ())), preferred_element_type=F32)


def _conv_carry(u, carry_ref, w_ref, b_ref, width, n_valid, cols=slice(None)):
    full = jnp.concatenate([carry_ref[:, cols], u], axis=0)
    acc = b_ref[:, cols] + u * w_ref[width - 1:width, cols]
    for k in range(1, width):
        acc = acc + pltpu.roll(full, k, 0)[SUBLANES:] * w_ref[width - 1 - k:width - k, cols]
    carry_ref[:, cols] = u[n_valid - SUBLANES:n_valid]
    return acc


def _time_major(ref):
    return jnp.concatenate([ref[:, t, :] for t in range(DEC_SEQ)], axis=0)


def _store_seq_major(ref, x, fill=None):
    nseq, _, c = ref.shape
    pad = jnp.zeros((1, c), x.dtype) if fill is None else fill
    ref[:, DEC_SEQ:, :] = jnp.broadcast_to(pad[None], (nseq, SEQ_PAD - DEC_SEQ, c))
    for t in range(DEC_SEQ):
        ref[:, t, :] = x[t * nseq:(t + 1) * nseq]


def _conv_tm(u, st_ref, new_ref, w_ref, b_ref, width, row_axis, cols=slice(None)):
    def idx(j):
        return (j, slice(None), cols) if row_axis == 0 else (slice(None), j, cols)

    nseq = st_ref.shape[1 - row_axis]
    full = ([st_ref[idx(j)] for j in range(width - 1)]
            + [u[t * nseq:(t + 1) * nseq] for t in range(DEC_SEQ)])
    outs = []
    for t in range(DEC_SEQ):
        acc = b_ref[:, cols]
        for j in range(width):
            acc = acc + full[t + j] * w_ref[j:j + 1, cols]
        outs.append(acc)
    for j in range(width - 1):
        new_ref[idx(j)] = full[DEC_SEQ + j]
    return jnp.concatenate(outs, axis=0)


def _gate_tile(xb, valid, wsm_ref, bsm_ref, alog_ref):
    g = _wdot(xb, wsm_ref) + bsm_ref[...]
    lane = lax.broadcasted_iota(jnp.int32, (1, LANES), 1)
    sp = _softplus(g)
    lsg = -_softplus(-g)
    a_row = jnp.where((lane >= LANE_DTA) & (lane < LANE_DT), -jnp.exp(alog_ref[...]), 0.0)
    tile = jnp.where(lane < LANE_LF, g,
                     jnp.where(lane < LANE_DTA, lsg,
                               jnp.where(lane < LANE_DT, sp * a_row,
                                         jnp.where(lane < LANE_END, sp, 0.0))))
    if valid is None:
        return tile
    return jnp.where(valid, tile, jnp.where(lane < LANE_LF, -jnp.inf, 0.0))


def _cumsum_rows(x, period):
    pos = lax.broadcasted_iota(jnp.int32, x.shape, 0) & (period - 1)
    k = 1
    while k < period:
        x = x + jnp.where(pos >= k, pltpu.roll(x, k, 0), 0.0)
        k *= 2
    return x


def _per_row(values, sub):
    if len(values) == 1:
        return values[0]
    return jnp.concatenate([jnp.broadcast_to(x, (sub, x.shape[1])) for x in values], axis=0)


def _chunk_recur(q, k, v, gt, xs, bm, cm, states, sub, fillers=None):
    staged = fillers is not None
    fillers = list(fillers or ())

    def fill():
        if fillers:
            fillers.pop(0)()

    def run(vector_stage, matmul_stage, items):
        if staged:
            ctxs = [vector_stage(i) for i in items]
            return [matmul_stage(c) for c in ctxs]
        return [matmul_stage(vector_stage(i)) for i in items]

    nsub = len(states)
    assert nsub * sub == CHUNK == q.shape[0]
    sub_rows = [slice(i * sub, (i + 1) * sub) for i in range(nsub)]
    lane = lax.broadcasted_iota(jnp.int32, (1, LANES), 1)
    cum_lanes = (lane >= LANE_LF) & (lane < LANE_DT)
    gc = jnp.where(cum_lanes, _cumsum_rows(jnp.where(cum_lanes, gt, 0.0), sub), gt)
    gct = gc.T
    lasts = [gc[r.stop - 1:r.stop, :] for r in sub_rows]
    last_rows = _per_row(lasts, sub)

    def head_dots(h):
        qh = q[:, h * M_DK:(h + 1) * M_DK]
        c0s = [st_in[0][h] for st_in, _ in states]
        qc = [_dot(qh[r], c0.astype(BF16)) for r, c0 in zip(sub_rows, c0s)]
        return _dot_nt(qh, k[:, h * M_DK:(h + 1) * M_DK]), jnp.concatenate(qc, axis=0), c0s

    def group_dots(g):
        cg = cm[:, g * S_STATE:(g + 1) * S_STATE]
        s_olds = [st_in[3][g * S_GROUP_W:(g + 1) * S_GROUP_W, :] for st_in, _ in states]
        yi = [_dot_nt(cg[r], s_old.astype(BF16)) for r, s_old in zip(sub_rows, s_olds)]
        return (_dot_nt(cg, bm[:, g * S_STATE:(g + 1) * S_STATE]),
                jnp.concatenate(yi, axis=0), s_olds)

    if staged:
        head_pre = [head_dots(h) for h in range(M_HEADS)]
        group_pre = [group_dots(g) for g in range(S_GROUPS)]
        head_dots = head_pre.__getitem__
        group_dots = group_pre.__getitem__
    fill()

    ti = lax.broadcasted_iota(jnp.int32, (CHUNK, CHUNK), 0)
    si = lax.broadcasted_iota(jnp.int32, (CHUNK, CHUNK), 1)
    causal = si <= ti
    if nsub > 1:
        causal = causal & ((si & -sub) == (ti & -sub))
    sub_lanes = [(lane >= r.start) & (lane < r.stop) for r in sub_rows]
    m_rows = [st_in[2][...] for st_in, _ in states]

    def head_vector(h):
        qh = q[:, h * M_DK:(h + 1) * M_DK]
        bt_c = gc[:, LANE_LF + h:LANE_LF + h + 1]
        ig_c = gc[:, LANE_IG + h:LANE_IG + h + 1]
        bt_r = gct[LANE_LF + h:LANE_LF + h + 1, :]
        ig_r = gct[LANE_IG + h:LANE_IG + h + 1, :]
        m0s = [m_row[:, h:h + 1] for m_row in m_rows]
        n0s = [st_in[1][h:h + 1, :] for st_in, _ in states]
        m0 = _per_row(m0s, sub)
        sqk, qc, c0s = head_dots(h)

        dmat = jnp.where(causal, bt_c - bt_r + ig_r, -jnp.inf)
        inter = bt_c + m0
        m_t = jnp.maximum(inter, jnp.max(dmat, axis=1, keepdims=True))
        w_intra = jnp.exp(dmat - m_t)
        w_inter = jnp.exp(inter - m_t)
        s = sqk * w_intra
        den = (jnp.sum(s, axis=1, keepdims=True)
               + jnp.sum(qh.astype(F32) * _per_row(n0s, sub), axis=1, keepdims=True) * w_inter)
        rdenom = 1.0 / jnp.maximum(jnp.abs(den), jnp.exp(-m_t))

        updates = []
        for i, r in enumerate(sub_rows):
            b_last = lasts[i][:, LANE_LF + h:LANE_LF + h + 1]
            d_last = b_last - bt_r + ig_r
            if nsub > 1:
                d_last = jnp.where(sub_lanes[i], d_last, -jnp.inf)
            m_new = jnp.maximum(b_last + m0s[i], jnp.max(d_last, axis=1, keepdims=True))
            w_last = jnp.exp(b_last - bt_c[r] + ig_c[r] - m_new)
            decay = jnp.exp(b_last + m0s[i] - m_new)
            kw = k[r, h * M_DK:(h + 1) * M_DK].astype(F32) * w_last
            states[i][1][1][h:h + 1, :] = decay * n0s[i] + jnp.sum(kw, axis=0, keepdims=True)
            updates.append((kw.astype(BF16), decay * c0s[i], m_new))
        return h, s.astype(BF16), qc * w_inter, rdenom, updates

    def head_matmul(ctx):
        h, s_b, inter_part, rdenom, updates = ctx
        num = _dot(s_b, v[:, h * M_DV:(h + 1) * M_DV]) + inter_part
        for i, (kw_b, c_decayed, _) in enumerate(updates):
            states[i][1][0][h] = c_decayed + _dot_tn(kw_b, v[sub_rows[i], h * M_DV:(h + 1) * M_DV])
        return num * rdenom, [m_new for _, _, m_new in updates]

    heads = run(head_vector, head_matmul, range(M_HEADS))
    for i in range(nsub):
        m_new_row = m_rows[i]
        for h, (_, m_news) in enumerate(heads):
            m_new_row = jnp.where(lane == h, m_news[i], m_new_row)
        states[i][1][2][...] = m_new_row
    fill()

    lo_half = lax.broadcasted_iota(jnp.int32, (CHUNK, LANES), 1) < S_HEADDIM
    xs_b = xs.astype(BF16)
    e_tile = jnp.exp(gc)
    wd_tile = jnp.exp(last_rows - gc) * pltpu.roll(gc, LANES - (LANE_DT - LANE_DTA), 1)
    hi_half = jnp.where(lo_half, 0, 1)

    def pair_lanes(tile, hh):
        return jnp.take_along_axis(tile, hi_half + (LANE_DTA + hh), axis=1)

    def group_vector(g):
        rows_g = slice(g * S_GROUP_W, (g + 1) * S_GROUP_W)
        cb, y_inter, s_olds = group_dots(g)
        pairs, wd_parts = [], []
        dec_parts = [[] for _ in range(nsub)]
        for p in range(S_HEADS_PER_GROUP // 2):
            ms = []
            for r in (2 * p, 2 * p + 1):
                hh = g * S_HEADS_PER_GROUP + r
                bt_c = gc[:, LANE_DTA + hh:LANE_DTA + hh + 1]
                bt_r = gct[LANE_DTA + hh:LANE_DTA + hh + 1, :]
                dt_r = gct[LANE_DT + hh:LANE_DT + hh + 1, :]
                dec = jnp.exp(jnp.where(causal, bt_c - bt_r, -jnp.inf))
                ms.append((cb * dec * dt_r).astype(BF16))
                for i in range(nsub):
                    dec_parts[i].append(jnp.broadcast_to(
                        jnp.exp(lasts[i][:, LANE_DTA + hh:LANE_DTA + hh + 1]), (S_HEADDIM, 1)))
            col = g * S_GROUP_W + p * LANES
            xp = xs_b[:, col:col + LANES]
            zero = jnp.zeros_like(xp)
            rhs = jnp.concatenate([jnp.where(lo_half, xp, zero), jnp.where(lo_half, zero, xp)], axis=0)
            hh = g * S_HEADS_PER_GROUP + 2 * p
            pairs.append((jnp.concatenate(ms, axis=1), rhs,
                          y_inter[:, p * LANES:(p + 1) * LANES] * pair_lanes(e_tile, hh)))
            wd_parts.append(pair_lanes(wd_tile, hh))
        a_mat = (xs[:, rows_g] * jnp.concatenate(wd_parts, axis=1)).astype(BF16)
        return g, pairs, a_mat, [jnp.concatenate(d, axis=0) * s_old for d, s_old in zip(dec_parts, s_olds)]

    def group_matmul(ctx):
        g, pairs, a_mat, s_decayed = ctx
        ys = [_dot(lhs, rhs) + inter_part for lhs, rhs, inter_part in pairs]
        for i, r in enumerate(sub_rows):
            states[i][1][3][g * S_GROUP_W:(g + 1) * S_GROUP_W, :] = (
                s_decayed[i] + _dot_tn(a_mat[r], bm[r, g * S_STATE:(g + 1) * S_STATE]))
        return jnp.concatenate(ys, axis=1)

    groups = run(group_vector, group_matmul, range(S_GROUPS))
    return jnp.concatenate([hp for hp, _ in heads], axis=1), jnp.concatenate(groups, axis=1)


def _post_math(h, y, xs, so, sz, sg, xn, mg_ref, sng_ref, dexp_ref, wpa_ref, wpb_ref, wout_ref,
               g1_ref, b1_ref):
    hs = []
    for i in range(M_HEADS):
        hh = h[:, i * M_DV:(i + 1) * M_DV]
        mu = jnp.mean(hh, axis=-1, keepdims=True)
        hc = hh - mu
        var = jnp.mean(hc * hc, axis=-1, keepdims=True)
        hs.append(hc * lax.rsqrt(var + LN_EPS))
    hn = jnp.concatenate(hs, axis=1) * mg_ref[...]
    ya = _wdot((so * hn).astype(BF16), wpa_ref)

    y = (y + dexp_ref[...] * xs) * sz
    ys = []
    for g in range(S_GROUPS):
        yg = y[:, g * S_GROUP_W:(g + 1) * S_GROUP_W]
        ys.append(yg * lax.rsqrt(jnp.mean(yg * yg, axis=-1, keepdims=True) + RMS_EPS))
    yn = jnp.concatenate(ys, axis=1) * sng_ref[...]
    yb = _wdot(yn.astype(BF16), wpb_ref)

    mixed = _wdot((sg[:, :D_MODEL] * ya + sg[:, D_MODEL:] * yb).astype(BF16), wout_ref)
    return _layer_norm(ALPHA * xn + mixed, g1_ref[...], b1_ref[...])


def _const_spec(shape):
    nd = len(shape)
    return pl.BlockSpec(shape, lambda *_: (0,) * nd, pipeline_mode=pl.Buffered(1))


def _pack_body(w_ref, o_ref):
    o_ref[...] = pltpu.bitcast(w_ref[...].astype(BF16), jnp.uint32)


def _pack_weight(w, name):
    k, n = w.shape
    bn = min(n, PACK_COLS)
    return pl.pallas_call(
        _pack_body, grid=(n // bn,), in_specs=[pl.BlockSpec((k, bn), lambda j: (0, j))],
        out_specs=pl.BlockSpec((k // 2, bn), lambda j: (0, j)),
        out_shape=jax.ShapeDtypeStruct((k // 2, n), jnp.uint32), name=f"pack_{name}",
        compiler_params=_params(1),
    )(w)


def _pack_t_body(wt_ref, o_ref):
    o_ref[...] = pltpu.bitcast(wt_ref[...].T.astype(BF16), jnp.uint32)


def _pack_gates_body(wif_ref, wdt_ref, o_ref):
    wdt = wdt_ref[...]
    rows = jnp.concatenate([wif_ref[...], wdt, wdt, jnp.zeros((LANES - LANE_END, D_MODEL), F32)], axis=0)
    o_ref[...] = pltpu.bitcast(rows.T.astype(BF16), jnp.uint32)


def _pack_in(wt):
    k = wt.shape[1]
    assert wt.shape[0] == D_IN
    starts = (0, IN_Z - IN_I, IN_GATE - IN_DT + IN_Z - IN_I)
    lens = (IN_I, IN_DT - IN_Z, D_IN - IN_GATE)
    assert all(s % SUBLANES == 0 for s in starts) and all(n % PACK_COLS == 0 for n in lens)
    b1 = lens[0] // PACK_COLS
    b2 = b1 + lens[1] // PACK_COLS

    def src_row(j):
        s0, s1, s2 = (s // SUBLANES for s in starts)
        return (j * (PACK_COLS // SUBLANES) + jnp.where(j < b1, s0, jnp.where(j < b2, s1, s2))) * SUBLANES

    big = pl.pallas_call(
        _pack_t_body, grid=(W_BIG_COLS // PACK_COLS,),
        in_specs=[pl.BlockSpec((pl.Element(PACK_COLS), pl.Element(k)), lambda j: (src_row(j), 0))],
        out_specs=pl.BlockSpec((k // 2, PACK_COLS), lambda j: (0, j)),
        out_shape=jax.ShapeDtypeStruct((k // 2, W_BIG_COLS), jnp.uint32),
        name="pack_in", compiler_params=_params(1),
    )(wt)
    small = pl.pallas_call(
        _pack_gates_body, out_shape=jax.ShapeDtypeStruct((k // 2, LANES), jnp.uint32), name="pack_gates",
    )(wt[IN_I:IN_Z], wt[IN_DT:IN_GATE])
    return big, small


def _params(dims):
    return pltpu.CompilerParams(dimension_semantics=("arbitrary",) * dims,
                                vmem_limit_bytes=VMEM_LIMIT_BYTES)


def _mixer_body(rows, n_valid,
                x_ref, g0_ref, b0_ref, wbig_ref, wsm_ref, bsm_ref, alog_ref,
                wmc_ref, bmc_ref, wsc_ref, bsc_ref, cin_qk_ref, cin_xbc_ref,
                c0_ref, n0_ref, m0_ref, s0_ref,
                mg_ref, sng_ref, dexp_ref, wpa_ref, wpb_ref, wout_ref, g1_ref, b1_ref,
                x1_ref, cout_qk_ref, cout_xbc_ref, c1_ref, n1_ref, m1_ref, s1_ref):
    @pl.when(pl.program_id(1) == 0)
    def _():
        cout_qk_ref[...] = cin_qk_ref[...]
        cout_xbc_ref[...] = cin_xbc_ref[...]
        c1_ref[...] = c0_ref[...]
        n1_ref[...] = n0_ref[...]
        m1_ref[...] = m0_ref[...]
        s1_ref[...] = s0_ref[...]

    xn = _layer_norm(x_ref[...], g0_ref[...], b0_ref[...])
    xb = xn.astype(BF16)
    valid = None
    if n_valid < rows:
        valid = lax.broadcasted_iota(jnp.int32, (rows, 1), 0) < n_valid
    gt = _gate_tile(xb, valid, wsm_ref, bsm_ref, alog_ref)

    u_qk = _wdot(xb, wbig_ref, OFF_QK, OFF_QK + 2 * M_QK)
    qk = _silu(_conv_carry(u_qk, cout_qk_ref, wmc_ref, bmc_ref, M_CONV, n_valid))
    q = (qk[:, :M_QK] * (M_DK ** -0.5)).astype(BF16)
    k = qk[:, M_QK:].astype(BF16)
    v = _wdot(xb, wbig_ref, OFF_V, OFF_V + M_V).astype(BF16)
    u_xbc = _wdot(xb, wbig_ref, OFF_XBC, OFF_XBC + S_XBC)
    xbc = _silu(_conv_carry(u_xbc, cout_xbc_ref, wsc_ref, bsc_ref, S_CONV, n_valid))
    xs = xbc[:, :S_INNER]
    bm = xbc[:, S_INNER:S_INNER + S_BC].astype(BF16)
    cm = xbc[:, S_INNER + S_BC:].astype(BF16)

    offs = [OFF_O, OFF_Z, OFF_Z + D_MODEL, OFF_GATE, OFF_GATE + D_MODEL]
    dense = {}
    nchunk = rows // CHUNK
    nslot = nchunk * FILLER_SLOTS
    hs, ys = [], []
    for ci in range(nchunk):
        sl = slice(ci * CHUNK, (ci + 1) * CHUNK)
        state = (c1_ref, n1_ref, m1_ref, s1_ref)

        def filler(slot):
            def issue():
                for off in offs[slot::nslot]:
                    dense[off] = _wdot(xb, wbig_ref, off, off + D_MODEL)
            return issue

        h, y = _chunk_recur(q[sl], k[sl], v[sl], gt[sl], xs[sl], bm[sl], cm[sl], [(state, state)], CHUNK,
                            [filler(ci * FILLER_SLOTS + i) for i in range(FILLER_SLOTS)])
        hs.append(h)
        ys.append(y)
    h = jnp.concatenate(hs, axis=0)
    y = jnp.concatenate(ys, axis=0)

    so = _sigmoid(dense[OFF_O])
    sz = _silu(jnp.concatenate([dense[OFF_Z], dense[OFF_Z + D_MODEL]], axis=1))
    sg = _sigmoid(jnp.concatenate([dense[OFF_GATE], dense[OFF_GATE + D_MODEL]], axis=1))
    x1_ref[...] = _post_math(h, y, xs, so, sz, sg, xn, mg_ref, sng_ref, dexp_ref,
                             wpa_ref, wpb_ref, wout_ref, g1_ref, b1_ref)


def _mixer(x_rows, cin_qk, cin_xbc, states, wts, *, nseq, rows, n_valid):
    total = x_rows.shape[0]
    nt = total // (nseq * rows)
    rspec = pl.BlockSpec((rows, D_MODEL), lambda b, c: (b * nt + c, 0))

    def per_seq(shape):
        nd = len(shape)
        return pl.BlockSpec((None,) + shape, lambda b, c: (b,) + (0,) * nd)

    def shared(shape):
        nd = len(shape)
        return pl.BlockSpec((None,) + shape, lambda b, c: (0,) * (nd + 1), pipeline_mode=pl.Buffered(1))

    tails = [(SUBLANES, 2 * M_QK), (SUBLANES, S_XBC)]
    in_specs = ([rspec, _const_spec((1, D_MODEL)), _const_spec((1, D_MODEL)),
                 _const_spec((D_MODEL // 2, W_BIG_COLS)), _const_spec((D_MODEL // 2, LANES)),
                 _const_spec((1, LANES)), _const_spec((1, LANES)),
                 _const_spec((M_CONV, 2 * M_QK)), _const_spec((1, 2 * M_QK)),
                 _const_spec((S_CONV, S_XBC)), _const_spec((1, S_XBC))]
                + [shared(s) for s in tails + STATE_SHAPES]
                + [_const_spec((1, M_V)), _const_spec((1, S_INNER)), _const_spec((1, S_INNER)),
                   _const_spec((M_V // 2, D_MODEL)), _const_spec((S_INNER // 2, D_MODEL)),
                   _const_spec((D_MODEL // 2, D_MODEL)), _const_spec((1, D_MODEL)), _const_spec((1, D_MODEL))])
    out_specs = [rspec] + [per_seq(s) for s in tails + STATE_SHAPES]
    out_shapes = ([jax.ShapeDtypeStruct((total, D_MODEL), F32)]
                  + [jax.ShapeDtypeStruct((nseq,) + s, F32) for s in tails + STATE_SHAPES])
    outs = pl.pallas_call(
        functools.partial(_mixer_body, rows, n_valid),
        grid=(nseq, nt), in_specs=in_specs, out_specs=out_specs, out_shape=out_shapes,
        name=f"mixer_{nseq}", compiler_params=_params(2),
    )(x_rows, wts["ln0_g"], wts["ln0_b"], wts["w_big"], wts["w_small"], wts["b_small"], wts["a_log"],
      wts["w_mconv"], wts["b_mconv"], wts["w_sconv"], wts["b_sconv"], cin_qk, cin_xbc, *states,
      wts["mnorm_g"], wts["snorm_g"], wts["d_exp"], wts["w_proj_a"], wts["w_proj_b"], wts["w_out"],
      wts["ln1_g"], wts["ln1_b"])
    return outs[0], outs[1:3], outs[3:]


def _in_proj_body(x_ref, g0_ref, b0_ref, wbig_ref, wsm_ref, bsm_ref, alog_ref,
                  wmc_ref, bmc_ref, wsc_ref, bsc_ref, st_qk_ref, st_xbc_ref,
                  q_ref, k_ref, v_ref, gate_ref, xs_ref, bm_ref, cm_ref,
                  xn_ref, xs_tm_ref, so_ref, sz_ref, sg_ref, new_qk_ref, new_xbc_ref):
    xn = _layer_norm(_time_major(x_ref), g0_ref[...], b0_ref[...])
    xn_ref[...] = xn
    xb = xn.astype(BF16)
    lane = lax.broadcasted_iota(jnp.int32, (1, LANES), 1)
    _store_seq_major(gate_ref, _gate_tile(xb, None, wsm_ref, bsm_ref, alog_ref),
                     jnp.where(lane < LANE_LF, -jnp.inf, 0.0))

    u_qk = _wdot(xb, wbig_ref, OFF_QK, OFF_QK + 2 * M_QK)
    qk = _silu(_conv_tm(u_qk, st_qk_ref, new_qk_ref, wmc_ref, bmc_ref, M_CONV, 0))
    _store_seq_major(q_ref, qk[:, :M_QK] * (M_DK ** -0.5))
    _store_seq_major(k_ref, qk[:, M_QK:])
    _store_seq_major(v_ref, _wdot(xb, wbig_ref, OFF_V, OFF_V + M_V))
    so_ref[...] = _sigmoid(_wdot(xb, wbig_ref, OFF_O, OFF_O + M_V))
    sz_ref[...] = _silu(_wdot(xb, wbig_ref, OFF_Z, OFF_Z + S_INNER))
    u_xbc = _wdot(xb, wbig_ref, OFF_XBC, OFF_XBC + S_XBC)
    xbc = _silu(_conv_tm(u_xbc, st_xbc_ref, new_xbc_ref, wsc_ref, bsc_ref, S_CONV, 0))
    xs_tm_ref[...] = xbc[:, :S_INNER]
    _store_seq_major(xs_ref, xbc[:, :S_INNER])
    _store_seq_major(bm_ref, xbc[:, S_INNER:S_INNER + S_BC])
    _store_seq_major(cm_ref, xbc[:, S_INNER + S_BC:])
    sg_ref[...] = _sigmoid(_wdot(xb, wbig_ref, OFF_GATE, OFF_GATE + 2 * D_MODEL))


def _in_proj(x, st_qk, st_xbc, wts):
    nseq = x.shape[0]
    nb = SAMPLE_BLOCK

    def rows_first(width, cols):
        return pl.BlockSpec((width - 1, nb, cols), lambda i: (0, i, 0))

    def sspec(rows, cols):
        return pl.BlockSpec((nb, rows, cols), lambda i: (i, 0, 0))

    def tspec(cols):
        return pl.BlockSpec((nb * DEC_SEQ, cols), lambda i: (i, 0))

    in_specs = [sspec(DEC_SEQ, D_MODEL), _const_spec((1, D_MODEL)), _const_spec((1, D_MODEL)),
                _const_spec((D_MODEL // 2, W_BIG_COLS)), _const_spec((D_MODEL // 2, LANES)),
                _const_spec((1, LANES)), _const_spec((1, LANES)),
                _const_spec((M_CONV, 2 * M_QK)), _const_spec((1, 2 * M_QK)),
                _const_spec((S_CONV, S_XBC)), _const_spec((1, S_XBC)),
                rows_first(M_CONV, 2 * M_QK), rows_first(S_CONV, S_XBC)]
    seq_cols = [M_QK, M_QK, M_V, LANES, S_INNER, S_BC, S_BC]
    tm_cols = [D_MODEL, S_INNER, M_V, S_INNER, 2 * D_MODEL]
    outs = pl.pallas_call(
        _in_proj_body, grid=(nseq // nb,), in_specs=in_specs,
        out_specs=([sspec(SEQ_PAD, c) for c in seq_cols] + [tspec(c) for c in tm_cols]
                   + [rows_first(M_CONV, 2 * M_QK), rows_first(S_CONV, S_XBC)]),
        out_shape=([jax.ShapeDtypeStruct((nseq, SEQ_PAD, c), F32) for c in seq_cols]
                   + [jax.ShapeDtypeStruct((nseq * DEC_SEQ, c), F32) for c in tm_cols]
                   + [jax.ShapeDtypeStruct((M_CONV - 1, nseq, 2 * M_QK), F32),
                      jax.ShapeDtypeStruct((S_CONV - 1, nseq, S_XBC), F32)]),
        name="sample_in_proj", compiler_params=_params(1),
    )(x, wts["ln0_g"], wts["ln0_b"], wts["w_big"], wts["w_small"], wts["b_small"], wts["a_log"],
      wts["w_mconv"], wts["b_mconv"], wts["w_sconv"], wts["b_sconv"], st_qk, st_xbc)
    return outs[:7], outs[7:12], outs[12:]


def _sample_recur_body(q_ref, k_ref, v_ref, gate_ref, xs_ref, bm_ref, cm_ref,
                       c0_ref, n0_ref, m0_ref, s0_ref,
                       h_ref, y_ref, c1_ref, n1_ref, m1_ref, s1_ref):
    nb = SAMPLE_SEQS_PER_STEP
    lane = lax.broadcasted_iota(jnp.int32, (1, LANES), 1)

    def stack(ref, dtype, fill=None):
        x = ref[...]
        c = x.shape[2]
        pad = jnp.zeros((1, 1, c), F32) if fill is None else fill[None]
        x = jnp.concatenate([x, jnp.broadcast_to(pad, (nb, SAMPLE_SUB - SEQ_PAD, c))], axis=1)
        return x.reshape(nb * SAMPLE_SUB, c).astype(dtype)

    def unstack(x):
        return x.reshape(nb, SAMPLE_SUB, x.shape[1])[:, :SEQ_PAD, :]

    states = [((c0_ref.at[i], n0_ref.at[i], m0_ref.at[i], s0_ref.at[i]),
               (c1_ref.at[i], n1_ref.at[i], m1_ref.at[i], s1_ref.at[i])) for i in range(nb)]
    h, y = _chunk_recur(stack(q_ref, BF16), stack(k_ref, BF16), stack(v_ref, BF16),
                        stack(gate_ref, F32, jnp.where(lane < LANE_LF, -jnp.inf, 0.0)),
                        stack(xs_ref, F32), stack(bm_ref, BF16), stack(cm_ref, BF16), states, SAMPLE_SUB)
    h_ref[...] = unstack(h)
    y_ref[...] = unstack(y)


def _sample_recur(acts, states, *, nseq):
    q, k, v, gate, xs, bm, cm = acts
    nb = SAMPLE_SEQS_PER_STEP
    assert nb * SAMPLE_SUB == CHUNK

    def rspec(cols):
        return pl.BlockSpec((nb, SEQ_PAD, cols), lambda i: (i, 0, 0))

    def sspec(shape):
        nd = len(shape)
        return pl.BlockSpec((nb,) + shape, lambda i: (i,) + (0,) * nd)

    in_specs = ([rspec(M_QK), rspec(M_QK), rspec(M_V), rspec(LANES), rspec(S_INNER), rspec(S_BC), rspec(S_BC)]
                + [sspec(s) for s in STATE_SHAPES])
    out_specs = [rspec(M_V), rspec(S_INNER)] + [sspec(s) for s in STATE_SHAPES]
    out_shapes = ([jax.ShapeDtypeStruct((nseq, SEQ_PAD, M_V), F32),
                   jax.ShapeDtypeStruct((nseq, SEQ_PAD, S_INNER), F32)]
                  + [jax.ShapeDtypeStruct((nseq,) + s, F32) for s in STATE_SHAPES])
    outs = pl.pallas_call(
        _sample_recur_body, grid=(nseq // nb,), in_specs=in_specs, out_specs=out_specs,
        out_shape=out_shapes, name="sample_recur", compiler_params=_params(1),
    )(q, k, v, gate, xs, bm, cm, *states)
    return outs[0], outs[1], outs[2:]


def _post_body(h_ref, y_ref, xs_ref, so_ref, sz_ref, sg_ref, xn_ref,
               mg_ref, sng_ref, dexp_ref, wpa_ref, wpb_ref, wout_ref, g1_ref, b1_ref, x1_ref):
    h = jnp.concatenate([h_ref[:, t, :] for t in range(DEC_SEQ)], axis=0)
    y = jnp.concatenate([y_ref[:, t, :] for t in range(DEC_SEQ)], axis=0)
    x1_ref[...] = _post_math(h, y, xs_ref[...], so_ref[...], sz_ref[...], sg_ref[...],
                             xn_ref[...], mg_ref, sng_ref, dexp_ref, wpa_ref, wpb_ref, wout_ref,
                             g1_ref, b1_ref)


def _post(h, y, xs, so, sz, sg, xn, wts):
    nseq = h.shape[0]
    nb = SAMPLE_BLOCK

    def sspec(cols):
        return pl.BlockSpec((nb, SEQ_PAD, cols), lambda i: (i, 0, 0))

    def tspec(cols):
        return pl.BlockSpec((nb * DEC_SEQ, cols), lambda i: (i, 0))

    in_specs = [sspec(M_V), sspec(S_INNER), tspec(S_INNER), tspec(M_V), tspec(S_INNER),
                tspec(2 * D_MODEL), tspec(D_MODEL),
                _const_spec((1, M_V)), _const_spec((1, S_INNER)), _const_spec((1, S_INNER)),
                _const_spec((M_V // 2, D_MODEL)), _const_spec((S_INNER // 2, D_MODEL)),
                _const_spec((D_MODEL // 2, D_MODEL)), _const_spec((1, D_MODEL)), _const_spec((1, D_MODEL))]
    return pl.pallas_call(
        _post_body, grid=(nseq // nb,), in_specs=in_specs, out_specs=tspec(D_MODEL),
        out_shape=jax.ShapeDtypeStruct((nseq * DEC_SEQ, D_MODEL), F32), name="sample_post",
        compiler_params=_params(1),
    )(h, y, xs, so, sz, sg, xn, wts["mnorm_g"], wts["snorm_g"], wts["d_exp"],
      wts["w_proj_a"], wts["w_proj_b"], wts["w_out"], wts["ln1_g"], wts["ln1_b"])


def _ffn_body(mode, rows, n_valid, sub_tiles,
              x1_ref, wup_ref, wfc_ref, bfc_ref, wdn_ref, g2_ref, b2_ref, cin_ref,
              y_ref, cout_ref):
    if mode == "seq":
        @pl.when(pl.program_id(1) == 0)
        def _():
            cout_ref[...] = cin_ref[...]

    def conv(u, cols):
        if mode == "tm":
            return _conv_tm(u, cin_ref, cout_ref, wfc_ref, bfc_ref, F_CONV, 1, cols)
        return _conv_carry(u, cout_ref, wfc_ref, bfc_ref, F_CONV, n_valid, cols)

    sub = rows // sub_tiles
    nblk = D_FF // FFN_BLOCK
    x1s = [x1_ref[i * sub:(i + 1) * sub, :] for i in range(sub_tiles)]
    x1bs = [x.astype(BF16) for x in x1s]

    def up_proj(n):
        x1b, lo = x1bs[n // nblk], (n % nblk) * FFN_BLOCK
        return (_wdot(x1b, wup_ref, lo, lo + FFN_BLOCK),
                _wdot(x1b, wup_ref, D_FF + lo, D_FF + lo + FFN_BLOCK))

    total = sub_tiles * nblk
    ups = [up_proj(n) for n in range(min(FFN_AHEAD, total))]
    for i in range(sub_tiles):
        ff = None
        for j in range(nblk):
            n = i * nblk + j
            lo = j * FFN_BLOCK
            ua, ub = ups[n]
            if n + FFN_AHEAD < total:
                ups.append(up_proj(n + FFN_AHEAD))
            va = conv(ua, slice(lo, lo + FFN_BLOCK))
            vb = conv(ub, slice(D_FF + lo, D_FF + lo + FFN_BLOCK))
            act = (_silu(va) * vb).astype(BF16)
            part = _dot(act, pltpu.bitcast(wdn_ref[lo // 2:(lo + FFN_BLOCK) // 2, :], BF16))
            ff = part if ff is None else ff + part
        y = _layer_norm(ALPHA * x1s[i] + ff, g2_ref[...], b2_ref[...])
        if mode == "tm":
            nseq = y_ref.shape[0]
            for t in range(DEC_SEQ):
                y_ref[:, t, :] = y[t * nseq:(t + 1) * nseq]
        else:
            y_ref[i * sub:(i + 1) * sub, :] = y


def _ffn(x1, cin, wts, *, mode, nseq, rows, n_valid, sub_tiles=1):
    assert sub_tiles == 1 or mode == "seq"
    total = x1.shape[0]
    nt = total // (nseq * rows)

    def rspec(cols):
        return pl.BlockSpec((rows, cols), lambda b, c: (b * nt + c, 0))

    if mode == "seq":
        cin_spec = pl.BlockSpec((None, SUBLANES, 2 * D_FF), lambda b, c: (0, 0, 0))
        cout_spec = pl.BlockSpec((None, SUBLANES, 2 * D_FF), lambda b, c: (b, 0, 0))
        cout_shape = jax.ShapeDtypeStruct((nseq, SUBLANES, 2 * D_FF), F32)
        y_spec = rspec(D_MODEL)
        y_shape = jax.ShapeDtypeStruct((total, D_MODEL), F32)
    else:
        nb = rows // DEC_SEQ
        sspec = lambda r, cols: pl.BlockSpec((nb, r, cols), lambda b, c: (b * nt + c, 0, 0))
        cin_spec = sspec(F_CONV - 1, 2 * D_FF)
        cout_spec = sspec(F_CONV - 1, 2 * D_FF)
        cout_shape = jax.ShapeDtypeStruct((total // DEC_SEQ, F_CONV - 1, 2 * D_FF), F32)
        y_spec = sspec(DEC_SEQ, D_MODEL)
        y_shape = jax.ShapeDtypeStruct((total // DEC_SEQ, DEC_SEQ, D_MODEL), F32)
    in_specs = [rspec(D_MODEL), _const_spec((D_MODEL // 2, 2 * D_FF)), _const_spec((F_CONV, 2 * D_FF)),
                _const_spec((1, 2 * D_FF)), _const_spec((D_FF // 2, D_MODEL)),
                _const_spec((1, D_MODEL)), _const_spec((1, D_MODEL)), cin_spec]
    return pl.pallas_call(
        functools.partial(_ffn_body, mode, rows, n_valid, sub_tiles),
        grid=(nseq, nt), in_specs=in_specs, out_specs=[y_spec, cout_spec],
        out_shape=[y_shape, cout_shape],
        name=f"ffn_{mode}_{nseq}", compiler_params=_params(2),
    )(x1, wts["w_up"], wts["w_fconv"], wts["b_fconv"], wts["w_down"], wts["ln2_g"], wts["ln2_b"], cin)


def kernel(x_prompt, x_sample, state_mlstm_conv, state_mlstm_C, state_mlstm_n, state_mlstm_m, state_ssm_conv, state_ssm, state_ffn_conv, meta_tokens, ln0_g, ln0_b, w_in, b_mlstm_if, w_mlstm_conv, b_mlstm_conv, mlstm_norm_g, w_proj_a, w_ssm_conv, b_ssm_conv, ssm_dt_bias, ssm_A_log, ssm_D, ssm_norm_g, w_proj_b, w_out, ln1_g, ln1_b, w_up, w_ffn_conv, b_ffn_conv, w_down, ln2_g, ln2_b):
    batch, seq, _ = x_prompt.shape
    dec_batch, dec_seq, _ = x_sample.shape
    prompt_tile = CHUNK
    ffn_tile = 2 * CHUNK
    assert dec_seq == DEC_SEQ and seq % (FFN_SUB_TILES * ffn_tile) == 0 and meta_tokens.shape[0] == N_META
    assert dec_batch % SAMPLE_SEQS_PER_STEP == 0 and dec_batch % SAMPLE_BLOCK == 0

    w_big, w_small = _pack_in(w_in[0].T)
    lane_pad = jnp.zeros((LANES - LANE_END,), F32)
    b_small = jnp.concatenate([b_mlstm_if[0], ssm_dt_bias[0], ssm_dt_bias[0], lane_pad])[None]
    a_log = jnp.concatenate([jnp.zeros((LANE_DTA,), F32), ssm_A_log[0],
                             jnp.zeros((LANES - LANE_DT,), F32)])[None]
    wts = {
        "ln0_g": ln0_g[None], "ln0_b": ln0_b[None], "w_big": w_big, "w_small": w_small,
        "b_small": b_small, "a_log": a_log,
        "w_mconv": w_mlstm_conv[0], "b_mconv": b_mlstm_conv, "w_sconv": w_ssm_conv[0], "b_sconv": b_ssm_conv,
        "mnorm_g": mlstm_norm_g, "snorm_g": ssm_norm_g, "d_exp": jnp.repeat(ssm_D[0], S_HEADDIM)[None],
        "w_proj_a": _pack_weight(w_proj_a[0], "proj_a"), "w_proj_b": _pack_weight(w_proj_b[0], "proj_b"),
        "w_out": _pack_weight(w_out[0], "out"), "ln1_g": ln1_g, "ln1_b": ln1_b,
        "w_up": _pack_weight(w_up[0], "up"), "w_fconv": w_ffn_conv[0], "b_fconv": b_ffn_conv,
        "w_down": _pack_weight(w_down[0], "down"), "ln2_g": ln2_g, "ln2_b": ln2_b,
    }

    x_meta = jnp.pad(meta_tokens, ((0, CHUNK - N_META), (0, 0)))
    zero_states = [jnp.zeros((1,) + s, F32) for s in STATE_SHAPES]
    x1_m, tails_m, states_m = _mixer(
        x_meta, jnp.zeros((1, SUBLANES, 2 * M_QK), F32), jnp.zeros((1, SUBLANES, S_XBC), F32),
        zero_states, wts, nseq=1, rows=CHUNK, n_valid=N_META)
    _, tail_ffn_m = _ffn(x1_m, jnp.zeros((1, SUBLANES, 2 * D_FF), F32), wts, mode="seq", nseq=1,
                         rows=CHUNK, n_valid=N_META)

    x1_p, tails_p, states_p = _mixer(
        x_prompt.reshape(batch * seq, D_MODEL), tails_m[0], tails_m[1], states_m, wts,
        nseq=batch, rows=prompt_tile, n_valid=prompt_tile)
    y_p, tail_ffn_p = _ffn(x1_p, tail_ffn_m, wts, mode="seq", nseq=batch, rows=FFN_SUB_TILES * ffn_tile,
                           n_valid=ffn_tile, sub_tiles=FFN_SUB_TILES)

    def rows_first(a):
        return jnp.swapaxes(a, 0, 1)

    acts, (xn, xs_tm, so, sz, sg), (new_qk, new_xbc) = _in_proj(
        x_sample, rows_first(state_mlstm_conv[0]), rows_first(state_ssm_conv[0]), wts)
    m_in = jnp.pad(state_mlstm_m[0], ((0, 0), (0, LANES - M_HEADS)))[:, None, :]
    h_s, yss_s, states_s = _sample_recur(
        acts, (state_mlstm_C[0], state_mlstm_n[0], m_in, state_ssm[0].reshape(dec_batch, S_INNER, S_STATE)),
        nseq=dec_batch)
    x1_s = _post(h_s, yss_s, xs_tm, so, sz, sg, xn, wts)
    y_s, new_ffn = _ffn(x1_s, state_ffn_conv[0], wts, mode="tm", nseq=1,
                        rows=SAMPLE_BLOCK * DEC_SEQ, n_valid=SAMPLE_BLOCK * DEC_SEQ)

    def pack_states(convs, states, n):
        c1, n1, m1, s1 = states
        return (convs[0][None], c1[None], n1[None], m1[:, 0, :M_HEADS][None], convs[1][None],
                s1.reshape(n, S_HEADS, S_HEADDIM, S_STATE)[None], convs[2][None])

    def tail_rows(t, width):
        return t[:, SUBLANES - (width - 1):, :]

    p_out = pack_states((tail_rows(tails_p[0], M_CONV), tail_rows(tails_p[1], S_CONV),
                         tail_rows(tail_ffn_p, F_CONV)), states_p, batch)
    s_out = pack_states((rows_first(new_qk), rows_first(new_xbc), new_ffn), states_s, dec_batch)
    y_prompt = y_p.reshape(batch, seq, D_MODEL)
    return (y_prompt, y_s) + p_out + s_out
```
